```python
import math
import jax, jax.numpy as jnp
from jax import lax
import numpy as np


D_MODEL = 1024
BATCH = 8
SEQ = 8192
DEPTH = 1
DEC_BATCH = 1
DEC_SEQ = 16384
PAST_LEN = 128

S5_WIDTH = 512
S5_GROUP = 16
S5_GROUPS = S5_WIDTH // S5_GROUP
S5_STATE = 64
HY_WIDTH = 512
HY_ORDER = 2
HY_SHORT = 3
FILT_BANDS = 8
FILT_EMB = 1 + 2 * FILT_BANDS
FILT_HIDDEN = 64
FILT_SHIFT = 0.05
FILT_CH = 2 * HY_ORDER * HY_WIDTH
N_EXPERTS = 16
EXPERT_FF = 2048
EC_CAPACITY = 2
ALPHA = (2.0 * DEPTH) ** 0.25
BETA = (8.0 * DEPTH) ** -0.25
LN_EPS = 1e-5
IN_COLS = S5_WIDTH + 3 * HY_WIDTH + 2 * D_MODEL

kernel_name = 'hybrid_s5_hyena_ec_encoder'


def _layer_norm(x, g, b):
    xf = x.astype(jnp.float32)
    mu = jnp.mean(xf, axis=-1, keepdims=True)
    var = jnp.mean(jnp.square(xf - mu), axis=-1, keepdims=True)
    return ((xf - mu) * lax.rsqrt(var + LN_EPS) * g.astype(jnp.float32) + b.astype(jnp.float32)).astype(x.dtype)


def _s5_discretise(lam_re, lam_im, log_dt, b_re, b_im):
    f32 = jnp.float32
    lam = lax.complex(-jnp.abs(lam_re.astype(f32)), lam_im.astype(f32))
    dt = jnp.exp(log_dt.astype(f32))[:, None]
    a_bar = jnp.exp(lam * dt)
    b = lax.complex(b_re.astype(f32), b_im.astype(f32))
    b_bar = ((a_bar - 1.0) / lam)[..., None] * b
    return a_bar, b_bar


def _linear_recurrence(a_bar, bu, reverse):
    def combine(left, right):
        a_l, b_l = left
        a_r, b_r = right
        return a_r * a_l, a_r * b_l + b_r
    a = jnp.broadcast_to(a_bar, bu.shape)
    _, s = lax.associative_scan(combine, (a, bu), reverse=reverse, axis=0)
    return s


def _s5_sequence(u, a_f, bb_f, a_b, bb_b, c, d):
    bu_f = jnp.einsum('lgh,gph->lgp', u, bb_f)
    bu_b = jnp.einsum('lgh,gph->lgp', u, bb_b)
    s = _linear_recurrence(a_f, bu_f, False) + _linear_recurrence(a_b, bu_b, True)
    return jnp.einsum('ghp,lgp->lgh', c, s).real + d * u


def _short_conv(x, w, b):
    L = x.shape[1]
    xp = jnp.pad(x, ((0, 0), (1, 1), (0, 0)))
    return xp[:, :L] * w[0] + xp[:, 1:L + 1] * w[1] + xp[:, 2:] * w[2] + b


def _hyena_filters(L, w1, b1, f1, w2, b2, f2, w3, decay):
    f32 = jnp.float32
    pos = jnp.arange(L, dtype=f32)
    t = pos / (L - 1)
    bands = jnp.linspace(1e-4, FILT_BANDS - 1, FILT_BANDS, dtype=f32)
    ang = 2.0 * math.pi * (pos / L)[:, None] * bands[None, :]
    z = jnp.concatenate([t[:, None], jnp.cos(ang), -jnp.sin(ang)], axis=-1)
    h = jnp.sin(f1.astype(f32) * (z @ w1.astype(f32) + b1.astype(f32)))
    h = jnp.sin(f2.astype(f32) * (h @ w2.astype(f32) + b2.astype(f32)))
    h = h @ w3.astype(f32)
    window = jnp.exp(-t[:, None] * jnp.abs(decay.astype(f32))[None, :]) + FILT_SHIFT
    return (h * window).reshape(L, 2, HY_ORDER, HY_WIDTH)


def _two_sided_filter_fft(h_fwd, h_bwd):
    circ = jnp.concatenate([h_fwd, jnp.zeros_like(h_fwd[:1]), h_bwd[:0:-1]], axis=0)
    return jnp.fft.rfft(circ, axis=0)


def _fft_long_conv(u, h_fft, bias):
    L = u.shape[1]
    U = jnp.fft.rfft(u, n=2 * L, axis=1)
    y = jnp.fft.irfft(U * h_fft[None], n=2 * L, axis=1)[:, :L]
    return y + u * bias.astype(jnp.float32)


def _expert_choice_ffn(x, w_router, ex_w1, ex_w3, ex_w2):
    B, L, D = x.shape
    n = B * L
    cap = EC_CAPACITY * n // N_EXPERTS
    xt = x.reshape(n, D)
    aff = jax.nn.softmax(xt.astype(jnp.float32) @ w_router.astype(jnp.float32), axis=-1)
    gate, idx = lax.top_k(aff.T, cap)
    xe = xt[idx]
    h = jax.nn.silu(jnp.einsum('ecd,edf->ecf', xe, ex_w1)) * jnp.einsum('ecd,edf->ecf', xe, ex_w3)
    ye = jnp.einsum('ecf,efd->ecd', h, ex_w2) * gate[..., None].astype(x.dtype)
    out = jnp.zeros_like(xt).at[idx.reshape(-1)].add(ye.reshape(-1, D))
    return out.reshape(B, L, D)


def _encoder_layer(x, w_in, s5_lambda_re, s5_lambda_im, s5_log_dt, s5_b_re, s5_b_im,
                   s5_c_re, s5_c_im, s5_d, s5_w_glu, w_s5_proj,
                   hy_short_w, hy_short_b, filt_w1, filt_b1, filt_freq1, filt_w2, filt_b2,
                   filt_freq2, filt_w3, filt_decay, hy_bias, w_hy_proj, w_out,
                   ln1_g, ln1_b, w_router, ex_w1, ex_w3, ex_w2, ln2_g, ln2_b):
    f32 = jnp.float32
    B, L, _ = x.shape
    proj = x @ w_in
    o1 = S5_WIDTH
    o2 = o1 + 3 * HY_WIDTH
    o3 = o2 + D_MODEL
    u_s5 = proj[..., :o1]
    u_hy = proj[..., o1:o2]
    g_s = jax.nn.sigmoid(proj[..., o2:o3])
    g_h = jax.nn.sigmoid(proj[..., o3:])

    a_f, bb_f = _s5_discretise(s5_lambda_re[0], s5_lambda_im[0], s5_log_dt[0], s5_b_re[0], s5_b_im[0])
    a_b, bb_b = _s5_discretise(s5_lambda_re[1], s5_lambda_im[1], s5_log_dt[1], s5_b_re[1], s5_b_im[1])
    c = lax.complex(s5_c_re.astype(f32), s5_c_im.astype(f32))
    d = s5_d.astype(f32)
    u = u_s5.astype(f32).reshape(B, L, S5_GROUPS, S5_GROUP)
    y_s5 = lax.map(lambda ub: _s5_sequence(ub, a_f, bb_f, a_b, bb_b, c, d), u)
    y_s5 = jax.nn.gelu(y_s5.reshape(B, L, S5_WIDTH).astype(x.dtype))
    branch_s = (y_s5 * jax.nn.sigmoid(y_s5 @ s5_w_glu)) @ w_s5_proj

    hy = _short_conv(u_hy, hy_short_w, hy_short_b)
    v = hy[..., :HY_WIDTH]
    gates = (hy[..., HY_WIDTH:2 * HY_WIDTH], hy[..., 2 * HY_WIDTH:])
    filt = _hyena_filters(L, filt_w1, filt_b1, filt_freq1, filt_w2, filt_b2, filt_freq2, filt_w3, filt_decay)
    z = v.astype(f32)
    for o in range(HY_ORDER):
        h_fft = _two_sided_filter_fft(filt[:, 0, o], filt[:, 1, o])
        z = gates[o].astype(f32) * _fft_long_conv(z, h_fft, hy_bias[o])
    branch_h = z.astype(x.dtype) @ w_hy_proj

    mix = (g_s * branch_s + g_h * branch_h) @ w_out
    x = _layer_norm(ALPHA * x + mix, ln1_g, ln1_b)

    x = _layer_norm(ALPHA * x + _expert_choice_ffn(x, w_router, ex_w1, ex_w3, ex_w2), ln2_g, ln2_b)
    return x


def _run_trunk(x, layer_params):
    for l in range(DEPTH):
        x = _encoder_layer(x, *[p[l] for p in layer_params])
    return x


def setup_inputs(seed: int = 0) -> dict:
    key = jax.random.key(seed)
    ks = iter(jax.random.split(key, 48))
    f32 = jnp.float32

    def nrm(shape, scale):
        return jax.random.normal(next(ks), shape, f32) * scale

    G, P, H = S5_GROUPS, S5_STATE, S5_GROUP
    lam_im0 = jnp.broadcast_to(math.pi * jnp.arange(P, dtype=f32), (DEPTH, 2, G, P))
    decay_mag = jnp.linspace(abs(math.log(1e-2) / 1.5), abs(math.log(1e-2) / 0.3), HY_WIDTH, dtype=f32)
    decay0 = jnp.broadcast_to(jnp.tile(decay_mag, 2 * HY_ORDER), (DEPTH, FILT_CH))
    return {
        'x_prompt': nrm((BATCH, SEQ, D_MODEL), 1.0),
        'x_sample': nrm((DEC_BATCH, DEC_SEQ, D_MODEL), 1.0),
        'w_in': nrm((DEPTH, D_MODEL, IN_COLS), D_MODEL ** -0.5),
        's5_lambda_re': -0.5 + nrm((DEPTH, 2, G, P), 0.01),
        's5_lambda_im': lam_im0 + nrm((DEPTH, 2, G, P), 0.01),
        's5_log_dt': jax.random.uniform(next(ks), (DEPTH, 2, G), f32, math.log(1e-3), math.log(1e-1)),
        's5_b_re': nrm((DEPTH, 2, G, P, H), (2.0 * H) ** -0.5),
        's5_b_im': nrm((DEPTH, 2, G, P, H), (2.0 * H) ** -0.5),
        's5_c_re': nrm((DEPTH, G, H, P), 0.5 * P ** -0.5),
        's5_c_im': nrm((DEPTH, G, H, P), 0.5 * P ** -0.5),
        's5_d': nrm((DEPTH, G, H), 1.0),
        's5_w_glu': nrm((DEPTH, S5_WIDTH, S5_WIDTH), S5_WIDTH ** -0.5),
        'w_s5_proj': nrm((DEPTH, S5_WIDTH, D_MODEL), S5_WIDTH ** -0.5),
        'hy_short_w': nrm((DEPTH, HY_SHORT, 3 * HY_WIDTH), HY_SHORT ** -0.5),
        'hy_short_b': nrm((DEPTH, 3 * HY_WIDTH), 0.02),
        'filt_w1': nrm((DEPTH, FILT_EMB, FILT_HIDDEN), FILT_EMB ** -0.5),
        'filt_b1': nrm((DEPTH, FILT_HIDDEN), 0.02),
        'filt_freq1': 1.0 + nrm((DEPTH, FILT_HIDDEN), 0.01),
        'filt_w2': nrm((DEPTH, FILT_HIDDEN, FILT_HIDDEN), FILT_HIDDEN ** -0.5),
        'filt_b2': nrm((DEPTH, FILT_HIDDEN), 0.02),
        'filt_freq2': 1.0 + nrm((DEPTH, FILT_HIDDEN), 0.01),
        'filt_w3': nrm((DEPTH, FILT_HIDDEN, FILT_CH), FILT_HIDDEN ** -0.5),
        'filt_decay': decay0 + nrm((DEPTH, FILT_CH), 0.01),
        'hy_bias': nrm((DEPTH, HY_ORDER, HY_WIDTH), 1.0),
        'w_hy_proj': nrm((DEPTH, HY_WIDTH, D_MODEL), HY_WIDTH ** -0.5),
        'w_out': nrm((DEPTH, D_MODEL, D_MODEL), BETA * D_MODEL ** -0.5),
        'ln1_g': 1.0 + nrm((DEPTH, D_MODEL), 0.02),
        'ln1_b': nrm((DEPTH, D_MODEL), 0.02),
        'w_router': nrm((DEPTH, D_MODEL, N_EXPERTS), D_MODEL ** -0.5),
        'ex_w1': nrm((DEPTH, N_EXPERTS, D_MODEL, EXPERT_FF), D_MODEL ** -0.5),
        'ex_w3': nrm((DEPTH, N_EXPERTS, D_MODEL, EXPERT_FF), D_MODEL ** -0.5),
        'ex_w2': nrm((DEPTH, N_EXPERTS, EXPERT_FF, D_MODEL), BETA * EXPERT_FF ** -0.5),
        'ln2_g': 1.0 + nrm((DEPTH, D_MODEL), 0.02),
        'ln2_b': nrm((DEPTH, D_MODEL), 0.02),
    }


def reference(x_prompt, x_sample, w_in, s5_lambda_re, s5_lambda_im, s5_log_dt, s5_b_re, s5_b_im,
              s5_c_re, s5_c_im, s5_d, s5_w_glu, w_s5_proj,
              hy_short_w, hy_short_b, filt_w1, filt_b1, filt_freq1, filt_w2, filt_b2,
              filt_freq2, filt_w3, filt_decay, hy_bias, w_hy_proj, w_out,
              ln1_g, ln1_b, w_router, ex_w1, ex_w3, ex_w2, ln2_g, ln2_b):
    layer_params = (w_in, s5_lambda_re, s5_lambda_im, s5_log_dt, s5_b_re, s5_b_im,
                    s5_c_re, s5_c_im, s5_d, s5_w_glu, w_s5_proj,
                    hy_short_w, hy_short_b, filt_w1, filt_b1, filt_freq1, filt_w2, filt_b2,
                    filt_freq2, filt_w3, filt_decay, hy_bias, w_hy_proj, w_out,
                    ln1_g, ln1_b, w_router, ex_w1, ex_w3, ex_w2, ln2_g, ln2_b)
    y_prompt = _run_trunk(x_prompt, layer_params)
    y_sample = _run_trunk(x_sample, layer_params)
    return (y_prompt, y_sample)
```

```python
import functools
import math

import jax
import jax.numpy as jnp
import numpy as np
from jax import lax
from jax.experimental import pallas as pl
from jax.experimental.pallas import tpu as pltpu

FILT_BANDS = 8
FILT_SHIFT = 0.05
EC_CAPACITY = 2
LN_EPS = 1e-5
HY_ORDER = 2

VMEM_LIMIT_BYTES = 56 * 1024 * 1024
ROW_TILE = 512
S5_TILE_ROWS = 512
S5_LANE_CHUNK = 512

bf16 = jnp.bfloat16
f32 = jnp.float32


def _cparams(*sem):
    return pltpu.CompilerParams(dimension_semantics=sem, vmem_limit_bytes=VMEM_LIMIT_BYTES)


def _const_spec(shape):
    return pl.BlockSpec(shape, lambda *_: (0,) * len(shape))


def _inproj_body(x_ref, w_ref, us_ref, uh_ref, gs_ref, gh_ref, *, o1, o2, o3):
    proj = jnp.dot(x_ref[...].astype(bf16), w_ref[...], preferred_element_type=f32)
    us_ref[...] = proj[:, :o1]
    uh_ref[...] = proj[:, o1:o2]
    gs_ref[...] = jax.nn.sigmoid(proj[:, o2:o3])
    gh_ref[...] = jax.nn.sigmoid(proj[:, o3:])


def _inproj(x2, w_in, B, L, s5w, hyw):
    n, D = x2.shape
    cols = w_in.shape[1]
    o1, o2 = s5w, s5w + 3 * hyw
    o3 = o2 + D
    tm = min(ROW_TILE, L)
    nt = L // tm
    return pl.pallas_call(
        functools.partial(_inproj_body, o1=o1, o2=o2, o3=o3),
        grid=(B, nt),
        in_specs=[pl.BlockSpec((tm, D), lambda b, i: (b * nt + i, 0)),
                  _const_spec((D, cols))],
        out_specs=[pl.BlockSpec((tm, s5w), lambda b, i: (i, b)),
                   pl.BlockSpec((tm, 3 * hyw), lambda b, i: (b * nt + i, 0)),
                   pl.BlockSpec((tm, D), lambda b, i: (b * nt + i, 0)),
                   pl.BlockSpec((tm, D), lambda b, i: (b * nt + i, 0))],
        out_shape=[jax.ShapeDtypeStruct((L, B * s5w), f32),
                   jax.ShapeDtypeStruct((n, 3 * hyw), f32),
                   jax.ShapeDtypeStruct((n, D), f32),
                   jax.ShapeDtypeStruct((n, D), f32)],
        compiler_params=_cparams("parallel", "parallel"),
        name="inproj",
    )(x2, w_in)


def _s5_body(u_ref, bmat_ref, are_ref, aim_ref, cmat_ref, d_ref, yprev_ref, y_ref,
             buf_ref, st_ref, *, R, T, NS, reverse, add_skip):
    @pl.when(pl.program_id(0) == 0)
    def _():
        st_ref[...] = jnp.zeros_like(st_ref)

    u = u_ref[...]
    buf_ref[...] = jnp.dot(u.astype(bf16), bmat_ref[...], preferred_element_type=f32)

    W = S5_LANE_CHUNK
    for k in range(NS // W):
        lo_re, lo_im = k * W, NS + k * W
        ar = jnp.broadcast_to(are_ref[:, lo_re:lo_re + W], (R, W))
        ai = jnp.broadcast_to(aim_ref[:, lo_re:lo_re + W], (R, W))

        def step(j, carry, lo_re=lo_re, lo_im=lo_im, ar=ar, ai=ai):
            sr, si = carry
            t = (T - 1 - j) if reverse else j
            row = pl.multiple_of(t * R, R)
            br = buf_ref[pl.ds(row, R), lo_re:lo_re + W]
            bi = buf_ref[pl.ds(row, R), lo_im:lo_im + W]
            nr = ar * sr - ai * si + br
            ni = ar * si + ai * sr + bi
            buf_ref[pl.ds(row, R), lo_re:lo_re + W] = nr
            buf_ref[pl.ds(row, R), lo_im:lo_im + W] = ni
            return nr, ni

        sr0 = st_ref[:, lo_re:lo_re + W]
        si0 = st_ref[:, lo_im:lo_im + W]
        sr, si = lax.fori_loop(0, T, step, (sr0, si0), unroll=4)
        st_ref[:, lo_re:lo_re + W] = sr
        st_ref[:, lo_im:lo_im + W] = si

    y = jnp.dot(buf_ref[...].astype(bf16), cmat_ref[...], preferred_element_type=f32)
    y = y + yprev_ref[...]
    if add_skip:
        y = y + d_ref[...] * u
    y_ref[...] = y


def _s5_pass(u_tb, bmat, a_re, a_im, cmat, dvec, y_prev, R, reverse, add_skip):
    rows, W5 = u_tb.shape
    NS = a_re.shape[1]
    T = S5_TILE_ROWS // R
    nt = rows // (T * R)
    imap = (lambda i: (nt - 1 - i, 0)) if reverse else (lambda i: (i, 0))
    body = functools.partial(_s5_body, R=R, T=T, NS=NS, reverse=reverse, add_skip=add_skip)
    return pl.pallas_call(
        body,
        grid=(nt,),
        in_specs=[pl.BlockSpec((T * R, W5), imap),
                  _const_spec((W5, 2 * NS)),
                  _const_spec((1, NS)),
                  _const_spec((1, NS)),
                  _const_spec((2 * NS, W5)),
                  _const_spec((1, W5)),
                  pl.BlockSpec((T * R, W5), imap)],
        out_specs=pl.BlockSpec((T * R, W5), imap),
        out_shape=jax.ShapeDtypeStruct((rows, W5), f32),
        scratch_shapes=[pltpu.VMEM((T * R, 2 * NS), f32),
                        pltpu.VMEM((R, 2 * NS), f32)],
        compiler_params=_cparams("arbitrary"),
        name="s5_bwd" if reverse else "s5_fwd",
    )(u_tb, bmat, a_re, a_im, cmat, dvec, y_prev)


def _s5_params(lam_re, lam_im, log_dt, b_re, b_im):
    lam = lax.complex(-jnp.abs(lam_re.astype(f32)), lam_im.astype(f32))
    dt = jnp.exp(log_dt.astype(f32))[:, None]
    a_bar = jnp.exp(lam * dt)
    b = lax.complex(b_re.astype(f32), b_im.astype(f32))
    b_bar = ((a_bar - 1.0) / lam)[..., None] * b
    return a_bar, b_bar


def _s5_matrices(a_bar, b_bar, c_re, c_im):
    G, P, H = b_bar.shape
    eye = jnp.eye(G, dtype=f32)
    bt_re = jnp.transpose(jnp.real(b_bar), (0, 2, 1))
    bt_im = jnp.transpose(jnp.imag(b_bar), (0, 2, 1))
    blk = lambda m: jnp.einsum('gab,gk->gakb', m, eye).reshape(G * m.shape[1], G * m.shape[2])
    bmat = jnp.concatenate([blk(bt_re), blk(bt_im)], axis=1)
    ct_re = jnp.transpose(c_re.astype(f32), (0, 2, 1))
    ct_im = jnp.transpose(c_im.astype(f32), (0, 2, 1))
    cmat = jnp.concatenate([blk(ct_re), -blk(ct_im)], axis=0)
    a_re = jnp.real(a_bar).reshape(1, G * P)
    a_im = jnp.imag(a_bar).reshape(1, G * P)
    return bmat.astype(bf16), a_re, a_im, cmat.astype(bf16)


def _shortconv_body(prev_ref, main_ref, next_ref, w_ref, b_ref, v_ref, g1_ref, g2_ref, *, nt, hyw):
    i = pl.program_id(1)
    x = main_ref[...]
    tl = x.shape[0]
    rows = lax.broadcasted_iota(jnp.int32, x.shape, 0)
    prev_row = jnp.where(i == 0, 0.0, prev_ref[7:8, :])
    next_row = jnp.where(i == nt - 1, 0.0, next_ref[0:1, :])
    xm1 = jnp.where(rows == 0, prev_row, pltpu.roll(x, 1, axis=0))
    xp1 = jnp.where(rows == tl - 1, next_row, pltpu.roll(x, tl - 1, axis=0))
    w = w_ref[...]
    hy = xm1 * w[0:1, :] + x * w[1:2, :] + xp1 * w[2:3, :] + b_ref[...]
    v_ref[...] = hy[:, :hyw]
    g1_ref[...] = hy[:, hyw:2 * hyw]
    g2_ref[...] = hy[:, 2 * hyw:]


def _shortconv(u_hy, w, b, B, L, hyw):
    n, C = u_hy.shape
    tl = min(ROW_TILE, L)
    nt = L // tl
    r8 = tl // 8
    nblk8 = n // 8
    main = lambda bb, i: (bb * nt + i, 0)
    prev = lambda bb, i: (jnp.maximum((bb * nt + i) * r8 - 1, 0), 0)
    nxt = lambda bb, i: (jnp.minimum((bb * nt + i + 1) * r8, nblk8 - 1), 0)
    out = jax.ShapeDtypeStruct((n, hyw), f32)
    return pl.pallas_call(
        functools.partial(_shortconv_body, nt=nt, hyw=hyw),
        grid=(B, nt),
        in_specs=[pl.BlockSpec((8, C), prev),
                  pl.BlockSpec((tl, C), main),
                  pl.BlockSpec((8, C), nxt),
                  _const_spec((3, C)),
                  _const_spec((1, C))],
        out_specs=[pl.BlockSpec((tl, hyw), main)] * 3,
        out_shape=[out, out, out],
        compiler_params=_cparams("parallel", "parallel"),
        name="shortconv",
    )(u_hy, u_hy, u_hy, w, b.reshape(1, C))


def _layer_norm(v, g, b):
    mu = jnp.mean(v, axis=-1, keepdims=True)
    c = v - mu
    var = jnp.mean(c * c, axis=-1, keepdims=True)
    return c * lax.rsqrt(var + LN_EPS) * g + b


def _merge_body(x_ref, ys_ref, z_ref, gs_ref, gh_ref, wglu_ref, wsp_ref, whp_ref, wout_ref,
                g_ref, b_ref, wr_ref, x1_ref, aff_ref, *, alpha):
    ys = jax.nn.gelu(ys_ref[...])
    gate = jax.nn.sigmoid(jnp.dot(ys.astype(bf16), wglu_ref[...], preferred_element_type=f32))
    branch_s = jnp.dot((ys * gate).astype(bf16), wsp_ref[...], preferred_element_type=f32)
    branch_h = jnp.dot(z_ref[...].astype(bf16), whp_ref[...], preferred_element_type=f32)
    mix = gs_ref[...] * branch_s + gh_ref[...] * branch_h
    mix = jnp.dot(mix.astype(bf16), wout_ref[...], preferred_element_type=f32)
    x1 = _layer_norm(alpha * x_ref[...] + mix, g_ref[...], b_ref[...])
    x1_ref[...] = x1
    logits = jnp.dot(x1, wr_ref[...], preferred_element_type=f32, precision=lax.Precision.HIGHEST)
    m = jnp.max(logits, axis=-1, keepdims=True)
    e = jnp.exp(logits - m)
    aff_ref[...] = e / jnp.sum(e, axis=-1, keepdims=True)


def _merge(x2, ys, z, g_s, g_h, w_glu, w_sp, w_hp, w_out, ln_g, ln_b, w_router, B, L, alpha):
    n, D = x2.shape
    s5w, hyw = ys.shape[1] // B, z.shape[1]
    E = w_router.shape[1]
    tm = min(ROW_TILE, L)
    nt = L // tm
    row = lambda b, i: (b * nt + i, 0)
    return pl.pallas_call(
        functools.partial(_merge_body, alpha=alpha),
        grid=(B, nt),
        in_specs=[pl.BlockSpec((tm, D), row),
                  pl.BlockSpec((tm, s5w), lambda b, i: (i, b)),
                  pl.BlockSpec((tm, hyw), row),
                  pl.BlockSpec((tm, D), row),
                  pl.BlockSpec((tm, D), row),
                  _const_spec(w_glu.shape), _const_spec(w_sp.shape), _const_spec(w_hp.shape),
                  _const_spec(w_out.shape), _const_spec((1, D)), _const_spec((1, D)),
                  _const_spec(w_router.shape)],
        out_specs=[pl.BlockSpec((tm, D), row), pl.BlockSpec((tm, E), row)],
        out_shape=[jax.ShapeDtypeStruct((n, D), f32), jax.ShapeDtypeStruct((n, E), f32)],
        compiler_params=_cparams("parallel", "parallel"),
        name="merge",
    )(x2, ys, z, g_s, g_h, w_glu, w_sp, w_hp, w_out, ln_g, ln_b, w_router)


def _ffn_body(xe_ref, gate_ref, w1_ref, w3_ref, w2_ref, ye_ref):
    xe = xe_ref[0].astype(bf16)
    h1 = jnp.dot(xe, w1_ref[0], preferred_element_type=f32)
    h3 = jnp.dot(xe, w3_ref[0], preferred_element_type=f32)
    h = (jax.nn.silu(h1) * h3).astype(bf16)
    ye_ref[0] = jnp.dot(h, w2_ref[0], preferred_element_type=f32) * gate_ref[0]


def _ffn(xe, gate, w1, w3, w2):
    E, cap, D = xe.shape
    F = w1.shape[2]
    tr = min(ROW_TILE, cap)
    return pl.pallas_call(
        _ffn_body,
        grid=(E, cap // tr),
        in_specs=[pl.BlockSpec((1, tr, D), lambda e, r: (e, r, 0)),
                  pl.BlockSpec((1, tr, 1), lambda e, r: (e, r, 0)),
                  pl.BlockSpec((1, D, F), lambda e, r: (e, 0, 0)),
                  pl.BlockSpec((1, D, F), lambda e, r: (e, 0, 0)),
                  pl.BlockSpec((1, F, D), lambda e, r: (e, 0, 0))],
        out_specs=pl.BlockSpec((1, tr, D), lambda e, r: (e, r, 0)),
        out_shape=jax.ShapeDtypeStruct((E, cap, D), f32),
        compiler_params=_cparams("parallel", "arbitrary"),
        name="expert_ffn",
    )(xe, gate, w1, w3, w2)


def _final_body(x1_ref, moe_ref, g_ref, b_ref, o_ref, *, alpha):
    o_ref[...] = _layer_norm(alpha * x1_ref[...] + moe_ref[...], g_ref[...], b_ref[...])


def _final(x1, moe, ln_g, ln_b, alpha):
    n, D = x1.shape
    tm = min(ROW_TILE, n)
    row = lambda i: (i, 0)
    return pl.pallas_call(
        functools.partial(_final_body, alpha=alpha),
        grid=(n // tm,),
        in_specs=[pl.BlockSpec((tm, D), row), pl.BlockSpec((tm, D), row),
                  _const_spec((1, D)), _const_spec((1, D))],
        out_specs=pl.BlockSpec((tm, D), row),
        out_shape=jax.ShapeDtypeStruct((n, D), f32),
        compiler_params=_cparams("parallel"),
        name="final_ln",
    )(x1, moe, ln_g, ln_b)


def _hyena_filters(L, w1, b1, f1, w2, b2, f2, w3, decay, hyw):
    pos = jnp.arange(L, dtype=f32)
    t = pos / (L - 1)
    bands = jnp.linspace(1e-4, FILT_BANDS - 1, FILT_BANDS, dtype=f32)
    ang = 2.0 * math.pi * (pos / L)[:, None] * bands[None, :]
    z = jnp.concatenate([t[:, None], jnp.cos(ang), -jnp.sin(ang)], axis=-1)
    hp = lax.Precision.HIGHEST
    h = jnp.sin(f1 * (jnp.dot(z, w1, precision=hp) + b1))
    h = jnp.sin(f2 * (jnp.dot(h, w2, precision=hp) + b2))
    h = jnp.dot(h, w3, precision=hp)
    window = jnp.exp(-t[:, None] * jnp.abs(decay)[None, :]) + FILT_SHIFT
    return (h * window).reshape(L, 2, HY_ORDER, hyw)


def _long_conv_xla(v, g1, g2, filt, hy_bias, B, L, hyw):
    z = v.reshape(B, L, hyw)
    gates = (g1.reshape(B, L, hyw), g2.reshape(B, L, hyw))
    for o in range(HY_ORDER):
        hf, hb = filt[:, 0, o], filt[:, 1, o]
        circ = jnp.concatenate([hf, jnp.zeros_like(hf[:1]), hb[:0:-1]], axis=0)
        h_fft = jnp.fft.rfft(circ, axis=0)
        U = jnp.fft.rfft(z, n=2 * L, axis=1)
        y = jnp.fft.irfft(U * h_fft[None], n=2 * L, axis=1)[:, :L]
        z = gates[o] * (y + z * hy_bias[o])
    return z.reshape(B * L, hyw)


def _encoder_layer(x, p, alpha):
    B, L, D = x.shape
    n = B * L
    x2 = x.reshape(n, D)
    G, P, H = p['s5_b_re'].shape[1:]
    s5w = G * H
    hyw = p['w_hy_proj'].shape[0]

    u_s5, u_hy, g_s, g_h = _inproj(x2, p['w_in'].astype(bf16), B, L, s5w, hyw)

    u_tb = u_s5.reshape(L * B, s5w)
    dvec = p['s5_d'].astype(f32).reshape(1, s5w)
    y = jnp.zeros_like(u_tb)
    for direction in range(2):
        a_bar, b_bar = _s5_params(p['s5_lambda_re'][direction], p['s5_lambda_im'][direction],
                                  p['s5_log_dt'][direction], p['s5_b_re'][direction],
                                  p['s5_b_im'][direction])
        bmat, a_re, a_im, cmat = _s5_matrices(a_bar, b_bar, p['s5_c_re'], p['s5_c_im'])
        y = _s5_pass(u_tb, bmat, a_re, a_im, cmat, dvec, y, B,
                     reverse=(direction == 1), add_skip=(direction == 0))
    y_s5 = y.reshape(L, B * s5w)

    v, g1, g2 = _shortconv(u_hy, p['hy_short_w'], p['hy_short_b'], B, L, hyw)
    filt = _hyena_filters(L, p['filt_w1'], p['filt_b1'], p['filt_freq1'], p['filt_w2'], p['filt_b2'],
                          p['filt_freq2'], p['filt_w3'], p['filt_decay'], hyw)
    z = _long_conv_xla(v, g1, g2, filt, p['hy_bias'], B, L, hyw)

    x1, aff = _merge(x2, y_s5, z, g_s, g_h, p['s5_w_glu'].astype(bf16), p['w_s5_proj'].astype(bf16),
                     p['w_hy_proj'].astype(bf16), p['w_out'].astype(bf16),
                     p['ln1_g'].reshape(1, D), p['ln1_b'].reshape(1, D), p['w_router'], B, L, alpha)

    E = aff.shape[1]
    cap = EC_CAPACITY * n // E
    gate, idx = lax.top_k(aff.T, cap)
    xe = x1[idx]
    ye = _ffn(xe, gate[..., None], p['ex_w1_bf16'], p['ex_w3_bf16'], p['ex_w2_bf16'])
    moe = jnp.zeros_like(x1).at[idx.reshape(-1)].add(ye.reshape(-1, D))

    out = _final(x1, moe, p['ln2_g'].reshape(1, D), p['ln2_b'].reshape(1, D), alpha)
    return out.reshape(B, L, D)


_PARAM_NAMES = ('w_in', 's5_lambda_re', 's5_lambda_im', 's5_log_dt', 's5_b_re', 's5_b_im',
                's5_c_re', 's5_c_im', 's5_d', 's5_w_glu', 'w_s5_proj',
                'hy_short_w', 'hy_short_b', 'filt_w1', 'filt_b1', 'filt_freq1', 'filt_w2', 'filt_b2',
                'filt_freq2', 'filt_w3', 'filt_decay', 'hy_bias', 'w_hy_proj', 'w_out',
                'ln1_g', 'ln1_b', 'w_router', 'ex_w1', 'ex_w3', 'ex_w2', 'ln2_g', 'ln2_b')


def kernel(x_prompt, x_sample, w_in, s5_lambda_re, s5_lambda_im, s5_log_dt, s5_b_re, s5_b_im, s5_c_re, s5_c_im, s5_d, s5_w_glu, w_s5_proj, hy_short_w, hy_short_b, filt_w1, filt_b1, filt_freq1, filt_w2, filt_b2, filt_freq2, filt_w3, filt_decay, hy_bias, w_hy_proj, w_out, ln1_g, ln1_b, w_router, ex_w1, ex_w3, ex_w2, ln2_g, ln2_b):
    stacked = (w_in, s5_lambda_re, s5_lambda_im, s5_log_dt, s5_b_re, s5_b_im, s5_c_re, s5_c_im, s5_d,
               s5_w_glu, w_s5_proj, hy_short_w, hy_short_b, filt_w1, filt_b1, filt_freq1, filt_w2,
               filt_b2, filt_freq2, filt_w3, filt_decay, hy_bias, w_hy_proj, w_out, ln1_g, ln1_b,
               w_router, ex_w1, ex_w3, ex_w2, ln2_g, ln2_b)
    depth = w_in.shape[0]
    alpha = (2.0 * depth) ** 0.25
    outs = []
    for x in (x_prompt, x_sample):
        for l in range(depth):
            p = {k: v[l] for k, v in zip(_PARAM_NAMES, stacked)}
            for k in ('ex_w1', 'ex_w3', 'ex_w2'):
                p[k + '_bf16'] = p[k].astype(bf16)
            x = _encoder_layer(x, p, alpha)
        outs.append(x)
    return tuple(outs)
```

```python
import functools
import math

import jax
import jax.numpy as jnp
import numpy as np
from jax import lax
from jax.experimental import pallas as pl
from jax.experimental.pallas import tpu as pltpu

FILT_BANDS = 8
FILT_SHIFT = 0.05
EC_CAPACITY = 2
LN_EPS = 1e-5
HY_ORDER = 2

VMEM_LIMIT_BYTES = 56 * 1024 * 1024
ROW_TILE = 512
S5_TILE_ROWS = 512
S5_LANE_CHUNK = 512

bf16 = jnp.bfloat16
f32 = jnp.float32


def _cparams(*sem):
    return pltpu.CompilerParams(dimension_semantics=sem, vmem_limit_bytes=VMEM_LIMIT_BYTES)


def _const_spec(shape):
    return pl.BlockSpec(shape, lambda *_: (0,) * len(shape))


def _inproj_body(x_ref, w_ref, us_ref, uh_ref, gs_ref, gh_ref, *, o1, o2, o3):
    proj = jnp.dot(x_ref[...].astype(bf16), w_ref[...], preferred_element_type=f32)
    us_ref[...] = proj[:, :o1]
    uh_ref[...] = proj[:, o1:o2]
    gs_ref[...] = jax.nn.sigmoid(proj[:, o2:o3])
    gh_ref[...] = jax.nn.sigmoid(proj[:, o3:])


def _inproj(x2, w_in, B, L, s5w, hyw):
    n, D = x2.shape
    cols = w_in.shape[1]
    o1, o2 = s5w, s5w + 3 * hyw
    o3 = o2 + D
    tm = min(ROW_TILE, L)
    nt = L // tm
    return pl.pallas_call(
        functools.partial(_inproj_body, o1=o1, o2=o2, o3=o3),
        grid=(B, nt),
        in_specs=[pl.BlockSpec((tm, D), lambda b, i: (b * nt + i, 0)),
                  _const_spec((D, cols))],
        out_specs=[pl.BlockSpec((tm, s5w), lambda b, i: (i, b)),
                   pl.BlockSpec((tm, 3 * hyw), lambda b, i: (b * nt + i, 0)),
                   pl.BlockSpec((tm, D), lambda b, i: (b * nt + i, 0)),
                   pl.BlockSpec((tm, D), lambda b, i: (b * nt + i, 0))],
        out_shape=[jax.ShapeDtypeStruct((L, B * s5w), f32),
                   jax.ShapeDtypeStruct((n, 3 * hyw), f32),
                   jax.ShapeDtypeStruct((n, D), f32),
                   jax.ShapeDtypeStruct((n, D), f32)],
        compiler_params=_cparams("parallel", "parallel"),
        name="inproj",
    )(x2, w_in)


def _s5_body(u_ref, bmat_ref, are_ref, aim_ref, cmat_ref, d_ref, yprev_ref, y_ref,
             buf_ref, st_ref, *, R, T, NS, reverse, add_skip):
    @pl.when(pl.program_id(0) == 0)
    def _():
        st_ref[...] = jnp.zeros_like(st_ref)

    u = u_ref[...]
    buf_ref[...] = jnp.dot(u.astype(bf16), bmat_ref[...], preferred_element_type=f32)

    W = S5_LANE_CHUNK
    for k in range(NS // W):
        lo_re, lo_im = k * W, NS + k * W
        ar = jnp.broadcast_to(are_ref[:, lo_re:lo_re + W], (R, W))
        ai = jnp.broadcast_to(aim_ref[:, lo_re:lo_re + W], (R, W))

        def step(j, carry, lo_re=lo_re, lo_im=lo_im, ar=ar, ai=ai):
            sr, si = carry
            t = (T - 1 - j) if reverse else j
            row = pl.multiple_of(t * R, R)
            br = buf_ref[pl.ds(row, R), lo_re:lo_re + W]
            bi = buf_ref[pl.ds(row, R), lo_im:lo_im + W]
            nr = ar * sr - ai * si + br
            ni = ar * si + ai * sr + bi
            buf_ref[pl.ds(row, R), lo_re:lo_re + W] = nr
            buf_ref[pl.ds(row, R), lo_im:lo_im + W] = ni
            return nr, ni

        sr0 = st_ref[:, lo_re:lo_re + W]
        si0 = st_ref[:, lo_im:lo_im + W]
        sr, si = lax.fori_loop(0, T, step, (sr0, si0), unroll=4)
        st_ref[:, lo_re:lo_re + W] = sr
        st_ref[:, lo_im:lo_im + W] = si

    y = jnp.dot(buf_ref[...].astype(bf16), cmat_ref[...], preferred_element_type=f32)
    y = y + yprev_ref[...]
    if add_skip:
        y = y + d_ref[...] * u
    y_ref[...] = y


def _s5_pass(u_tb, bmat, a_re, a_im, cmat, dvec, y_prev, R, reverse, add_skip):
    rows, W5 = u_tb.shape
    NS = a_re.shape[1]
    T = S5_TILE_ROWS // R
    nt = rows // (T * R)
    imap = (lambda i: (nt - 1 - i, 0)) if reverse else (lambda i: (i, 0))
    body = functools.partial(_s5_body, R=R, T=T, NS=NS, reverse=reverse, add_skip=add_skip)
    return pl.pallas_call(
        body,
        grid=(nt,),
        in_specs=[pl.BlockSpec((T * R, W5), imap),
                  _const_spec((W5, 2 * NS)),
                  _const_spec((1, NS)),
                  _const_spec((1, NS)),
                  _const_spec((2 * NS, W5)),
                  _const_spec((1, W5)),
                  pl.BlockSpec((T * R, W5), imap)],
        out_specs=pl.BlockSpec((T * R, W5), imap),
        out_shape=jax.ShapeDtypeStruct((rows, W5), f32),
        scratch_shapes=[pltpu.VMEM((T * R, 2 * NS), f32),
                        pltpu.VMEM((R, 2 * NS), f32)],
        compiler_params=_cparams("arbitrary"),
        name="s5_bwd" if reverse else "s5_fwd",
    )(u_tb, bmat, a_re, a_im, cmat, dvec, y_prev)


def _s5_params(lam_re, lam_im, log_dt, b_re, b_im):
    lam = lax.complex(-jnp.abs(lam_re.astype(f32)), lam_im.astype(f32))
    dt = jnp.exp(log_dt.astype(f32))[:, None]
    a_bar = jnp.exp(lam * dt)
    b = lax.complex(b_re.astype(f32), b_im.astype(f32))
    b_bar = ((a_bar - 1.0) / lam)[..., None] * b
    return a_bar, b_bar


def _s5_matrices(a_bar, b_bar, c_re, c_im):
    G, P, H = b_bar.shape
    eye = jnp.eye(G, dtype=f32)
    bt_re = jnp.transpose(jnp.real(b_bar), (0, 2, 1))
    bt_im = jnp.transpose(jnp.imag(b_bar), (0, 2, 1))
    blk = lambda m: jnp.einsum('gab,gk->gakb', m, eye).reshape(G * m.shape[1], G * m.shape[2])
    bmat = jnp.concatenate([blk(bt_re), blk(bt_im)], axis=1)
    ct_re = jnp.transpose(c_re.astype(f32), (0, 2, 1))
    ct_im = jnp.transpose(c_im.astype(f32), (0, 2, 1))
    cmat = jnp.concatenate([blk(ct_re), -blk(ct_im)], axis=0)
    a_re = jnp.real(a_bar).reshape(1, G * P)
    a_im = jnp.imag(a_bar).reshape(1, G * P)
    return bmat.astype(bf16), a_re, a_im, cmat.astype(bf16)


def _shortconv_body(prev_ref, main_ref, next_ref, w_ref, b_ref, v_ref, g1_ref, g2_ref, *, nt, hyw):
    i = pl.program_id(1)
    x = main_ref[...]
    tl = x.shape[0]
    rows = lax.broadcasted_iota(jnp.int32, x.shape, 0)
    prev_row = jnp.where(i == 0, 0.0, prev_ref[7:8, :])
    next_row = jnp.where(i == nt - 1, 0.0, next_ref[0:1, :])
    xm1 = jnp.where(rows == 0, prev_row, pltpu.roll(x, 1, axis=0))
    xp1 = jnp.where(rows == tl - 1, next_row, pltpu.roll(x, tl - 1, axis=0))
    w = w_ref[...]
    hy = xm1 * w[0:1, :] + x * w[1:2, :] + xp1 * w[2:3, :] + b_ref[...]
    v_ref[...] = hy[:, :hyw]
    g1_ref[...] = hy[:, hyw:2 * hyw]
    g2_ref[...] = hy[:, 2 * hyw:]


def _shortconv(u_hy, w, b, B, L, hyw):
    n, C = u_hy.shape
    tl = min(ROW_TILE, L)
    nt = L // tl
    r8 = tl // 8
    nblk8 = n // 8
    main = lambda bb, i: (bb * nt + i, 0)
    prev = lambda bb, i: (jnp.maximum((bb * nt + i) * r8 - 1, 0), 0)
    nxt = lambda bb, i: (jnp.minimum((bb * nt + i + 1) * r8, nblk8 - 1), 0)
    out = jax.ShapeDtypeStruct((n, hyw), f32)
    return pl.pallas_call(
        functools.partial(_shortconv_body, nt=nt, hyw=hyw),
        grid=(B, nt),
        in_specs=[pl.BlockSpec((8, C), prev),
                  pl.BlockSpec((tl, C), main),
                  pl.BlockSpec((8, C), nxt),
                  _const_spec((3, C)),
                  _const_spec((1, C))],
        out_specs=[pl.BlockSpec((tl, hyw), main)] * 3,
        out_shape=[out, out, out],
        compiler_params=_cparams("parallel", "parallel"),
        name="shortconv",
    )(u_hy, u_hy, u_hy, w, b.reshape(1, C))


def _layer_norm(v, g, b):
    mu = jnp.mean(v, axis=-1, keepdims=True)
    c = v - mu
    var = jnp.mean(c * c, axis=-1, keepdims=True)
    return c * lax.rsqrt(var + LN_EPS) * g + b


def _merge_body(x_ref, ys_ref, z_ref, gs_ref, gh_ref, wglu_ref, wsp_ref, whp_ref, wout_ref,
                g_ref, b_ref, wr_ref, x1_ref, aff_ref, *, alpha):
    ys = jax.nn.gelu(ys_ref[...])
    gate = jax.nn.sigmoid(jnp.dot(ys.astype(bf16), wglu_ref[...], preferred_element_type=f32))
    branch_s = jnp.dot((ys * gate).astype(bf16), wsp_ref[...], preferred_element_type=f32)
    branch_h = jnp.dot(z_ref[...].astype(bf16), whp_ref[...], preferred_element_type=f32)
    mix = gs_ref[...] * branch_s + gh_ref[...] * branch_h
    mix = jnp.dot(mix.astype(bf16), wout_ref[...], preferred_element_type=f32)
    x1 = _layer_norm(alpha * x_ref[...] + mix, g_ref[...], b_ref[...])
    x1_ref[...] = x1
    logits = jnp.dot(x1, wr_ref[...], preferred_element_type=f32, precision=lax.Precision.HIGHEST)
    m = jnp.max(logits, axis=-1, keepdims=True)
    e = jnp.exp(logits - m)
    aff_ref[...] = e / jnp.sum(e, axis=-1, keepdims=True)


def _merge(x2, ys, z, g_s, g_h, w_glu, w_sp, w_hp, w_out, ln_g, ln_b, w_router, B, L, alpha):
    n, D = x2.shape
    s5w, hyw = ys.shape[1] // B, z.shape[1]
    E = w_router.shape[1]
    tm = min(ROW_TILE, L)
    nt = L // tm
    row = lambda b, i: (b * nt + i, 0)
    return pl.pallas_call(
        functools.partial(_merge_body, alpha=alpha),
        grid=(B, nt),
        in_specs=[pl.BlockSpec((tm, D), row),
                  pl.BlockSpec((tm, s5w), lambda b, i: (i, b)),
                  pl.BlockSpec((tm, hyw), row),
                  pl.BlockSpec((tm, D), row),
                  pl.BlockSpec((tm, D), row),
                  _const_spec(w_glu.shape), _const_spec(w_sp.shape), _const_spec(w_hp.shape),
                  _const_spec(w_out.shape), _const_spec((1, D)), _const_spec((1, D)),
                  _const_spec(w_router.shape)],
        out_specs=[pl.BlockSpec((tm, D), row), pl.BlockSpec((tm, E), row)],
        out_shape=[jax.ShapeDtypeStruct((n, D), f32), jax.ShapeDtypeStruct((n, E), f32)],
        compiler_params=_cparams("parallel", "parallel"),
        name="merge",
    )(x2, ys, z, g_s, g_h, w_glu, w_sp, w_hp, w_out, ln_g, ln_b, w_router)


ROUTE_BLOCK = 256
ROUTE_SLOTS = 64
ROW_ALIGN = 8
GATE_LANES = 128


def _select_body(aff_ref, sel_ref, *, cap, idx_bits):
    bits = pltpu.bitcast(aff_ref[...], jnp.int32)
    E = bits.shape[0]
    count = lambda m: jnp.sum(jnp.where(m, 1.0, 0.0), axis=1, keepdims=True)

    def value_bit(i, prefix):
        cand = prefix | jnp.left_shift(jnp.int32(1), 30 - i)
        return jnp.where(count(bits >= cand) >= cap, cand, prefix)

    thr = lax.fori_loop(0, 31, value_bit, jnp.zeros((E, 1), jnp.int32))
    need = cap - count(bits > thr)
    idx = lax.broadcasted_iota(jnp.int32, bits.shape, 1)
    tie_idx = jnp.where(bits == thr, idx, jnp.int32(2 ** 30))

    def index_bit(i, bound):
        cand = bound | jnp.left_shift(jnp.int32(1), idx_bits - 1 - i)
        return jnp.where(count(tie_idx < cand) <= need, cand, bound)

    bound = lax.fori_loop(0, idx_bits, index_bit, jnp.zeros((E, 1), jnp.int32))
    sel_ref[...] = jnp.where(bits > thr, 1.0, jnp.where(tie_idx < bound, 1.0, 0.0))


def _select(aff_t, cap):
    E, n = aff_t.shape
    return pl.pallas_call(
        functools.partial(_select_body, cap=float(cap), idx_bits=int(n).bit_length()),
        out_shape=jax.ShapeDtypeStruct((E, n), f32),
        compiler_params=pltpu.CompilerParams(vmem_limit_bytes=VMEM_LIMIT_BYTES),
        name="expert_select",
    )(aff_t)


def _slot_onehot(sel, p_ref, first_slot, E):
    Tb = sel.shape[1]
    S = p_ref.shape[0] // E
    r = lax.broadcasted_iota(jnp.int32, (Tb, Tb), 0)
    c = lax.broadcasted_iota(jnp.int32, (Tb, Tb), 1)
    tri = jnp.where(r <= c, 1.0, 0.0).astype(bf16)
    incl = jnp.dot(sel.astype(bf16), tri, preferred_element_type=f32)
    slot = jnp.where(sel > 0.0, incl - 1.0, -1.0)
    want = (lax.broadcasted_iota(jnp.int32, (S, Tb), 0) + first_slot).astype(f32)
    for e in range(E):
        p_ref[e * S:(e + 1) * S, :] = jnp.where(slot[e:e + 1] == want, 1.0, 0.0).astype(bf16)
    return incl[:, Tb - 1:Tb]


def _gather_copies(stage_ref, xe_hbm, sem, off_ref, j, first_slot, E, S, nblk):
    copies = []
    for e in range(E):
        row = pl.multiple_of(off_ref[e * nblk + j] + first_slot, ROW_ALIGN)
        copies.append(pltpu.make_async_copy(stage_ref.at[e * S:(e + 1) * S], xe_hbm.at[e, pl.ds(row, S)], sem))
    return copies


def _gather_body(off_ref, rounds_ref, total_ref, sel_ref, x_ref, aff_ref, xe_hbm, p_ref, stage_ref, sem,
                 *, E, nblk, cap):
    j = pl.program_id(0)
    S = p_ref.shape[0] // E
    D = x_ref.shape[1]
    x = x_ref[...].astype(bf16)
    aff = aff_ref[...]
    a_hi = aff.astype(bf16)
    rem = aff - a_hi.astype(f32)
    a_mid = rem.astype(bf16)
    a_lo = (rem - a_mid.astype(f32)).astype(bf16)
    own = (lax.broadcasted_iota(jnp.int32, (E * S, E), 0) // S) == lax.broadcasted_iota(jnp.int32, (E * S, E), 1)

    def one_round(r, first_step):
        first_slot = r * S
        _slot_onehot(sel_ref[...], p_ref, first_slot, E)
        p = p_ref[...]
        rows = jnp.dot(p, x, preferred_element_type=f32)
        gates = (jnp.dot(p, a_hi, preferred_element_type=f32) + jnp.dot(p, a_mid, preferred_element_type=f32)
                 + jnp.dot(p, a_lo, preferred_element_type=f32))
        gate = jnp.sum(jnp.where(own, gates, 0.0), axis=1, keepdims=True)

        @pl.when(jnp.logical_not(first_step))
        def _():
            for cp in _gather_copies(stage_ref, xe_hbm, sem, off_ref, j, 0, E, S, nblk):
                cp.wait()

        stage_ref[:, :D] = rows
        stage_ref[:, D:] = jnp.broadcast_to(gate, (E * S, GATE_LANES))
        for cp in _gather_copies(stage_ref, xe_hbm, sem, off_ref, j, first_slot, E, S, nblk):
            cp.start()

    one_round(0, j == 0)

    def extra(r, carry):
        one_round(r, False)
        return carry

    lax.fori_loop(1, rounds_ref[j], extra, 0)

    @pl.when(j == nblk - 1)
    def _():
        for cp in _gather_copies(stage_ref, xe_hbm, sem, off_ref, j, 0, E, S, nblk):
            cp.wait()
        rows_pad = xe_hbm.shape[1]
        stage_ref[:S] = jnp.zeros((S, D + GATE_LANES), f32)
        nfill = -(-(rows_pad - cap) // S)

        def fill(e, row):
            return pltpu.make_async_copy(stage_ref.at[:S], xe_hbm.at[e, pl.ds(pl.multiple_of(row, ROW_ALIGN), S)], sem)

        def whole_chunks(e, action):
            def body(k, c):
                row = total_ref[e] + k * S

                @pl.when(row + S <= rows_pad)
                def _():
                    action(fill(e, row))
                return c
            lax.fori_loop(0, nfill, body, 0)

        for e in range(E):
            whole_chunks(e, lambda cp: cp.start())
        for e in range(E):
            whole_chunks(e, lambda cp: cp.wait())
        for e in range(E):
            fill(e, rows_pad - S).start()
        for e in range(E):
            fill(e, rows_pad - S).wait()


def _route_gather(sel, x1, aff, off, rounds, total, rows_pad):
    E, n = sel.shape
    D = x1.shape[1]
    Tb, S = ROUTE_BLOCK, ROUTE_SLOTS
    nblk = n // Tb
    return pl.pallas_call(
        functools.partial(_gather_body, E=E, nblk=nblk, cap=EC_CAPACITY * n // E),
        grid_spec=pltpu.PrefetchScalarGridSpec(
            num_scalar_prefetch=3,
            grid=(nblk,),
            in_specs=[pl.BlockSpec((E, Tb), lambda j, *_: (0, j)),
                      pl.BlockSpec((Tb, D), lambda j, *_: (j, 0)),
                      pl.BlockSpec((Tb, E), lambda j, *_: (j, 0))],
            out_specs=pl.BlockSpec(memory_space=pl.ANY),
            scratch_shapes=[pltpu.VMEM((E * S, Tb), bf16),
                            pltpu.VMEM((E * S, D + GATE_LANES), f32),
                            pltpu.SemaphoreType.DMA(())]),
        out_shape=jax.ShapeDtypeStruct((E, rows_pad, D + GATE_LANES), f32),
        compiler_params=_cparams("arbitrary"),
        name="route_gather",
    )(off, rounds, total, sel, x1, aff)


def _ffn_body(total_ref, xe_ref, w1_ref, w3_ref, w2_ref, ye_ref):
    e, r = pl.program_id(0), pl.program_id(1)
    tr = xe_ref.shape[1]
    D = ye_ref.shape[2]

    @pl.when(r * tr < total_ref[e])
    def _():
        xe = xe_ref[0, :, :D].astype(bf16)
        gate = xe_ref[0, :, D:D + 1]
        h1 = jnp.dot(xe, w1_ref[0], preferred_element_type=f32)
        h3 = jnp.dot(xe, w3_ref[0], preferred_element_type=f32)
        h = (jax.nn.silu(h1) * h3).astype(bf16)
        ye = jnp.dot(h, w2_ref[0], preferred_element_type=f32) * gate
        row = r * tr + lax.broadcasted_iota(jnp.int32, ye.shape, 0)
        ye_ref[0] = jnp.where(row < total_ref[e], ye, 0.0)

    @pl.when(r * tr >= total_ref[e])
    def _():
        ye_ref[...] = jnp.zeros_like(ye_ref)


def _ffn(total, xe, w1, w3, w2):
    E, rows_pad, Dx = xe.shape
    D, F = w1.shape[1:]
    tr = ROW_TILE
    last = lambda e, tot: (tot[e] - 1) // tr
    rowmap = lambda e, r, tot: (e, jnp.minimum(r, last(e, tot)), 0)
    wmap = lambda e, r, tot: (e, 0, 0)
    return pl.pallas_call(
        _ffn_body,
        grid_spec=pltpu.PrefetchScalarGridSpec(
            num_scalar_prefetch=1,
            grid=(E, rows_pad // tr),
            in_specs=[pl.BlockSpec((1, tr, Dx), rowmap),
                      pl.BlockSpec((1, D, F), wmap), pl.BlockSpec((1, D, F), wmap),
                      pl.BlockSpec((1, F, D), wmap)],
            out_specs=pl.BlockSpec((1, tr, D), lambda e, r, tot: (e, r, 0))),
        out_shape=jax.ShapeDtypeStruct((E, rows_pad, D), f32),
        compiler_params=_cparams("arbitrary", "arbitrary"),
        name="expert_ffn",
    )(total, xe, w1, w3, w2)


def _scatter_copies(ye_hbm, buf_ref, sem, off_ref, j, first_slot, E, S, nblk):
    copies = []
    for e in range(E):
        row = pl.multiple_of(off_ref[e * nblk + j] + first_slot, ROW_ALIGN)
        copies.append(pltpu.make_async_copy(ye_hbm.at[e, pl.ds(row, S)], buf_ref.at[e * S:(e + 1) * S], sem))
    return copies


def _scatter_body(off_ref, rounds_ref, sel_ref, x1_ref, g_ref, b_ref, ye_hbm, o_ref, p_ref, buf_ref, sem,
                  *, E, nblk, alpha):
    j = pl.program_id(0)
    S = p_ref.shape[0] // E
    slot = lax.rem(j, 2)

    def fetch(jj, first_slot, s):
        return _scatter_copies(ye_hbm, buf_ref.at[s], sem.at[s], off_ref, jj, first_slot, E, S, nblk)

    @pl.when(j == 0)
    def _():
        for cp in fetch(j, 0, 0):
            cp.start()

    @pl.when(j + 1 < nblk)
    def _():
        for cp in fetch(j + 1, 0, 1 - slot):
            cp.start()

    def one_round(r, s):
        count = _slot_onehot(sel_ref[...], p_ref, r * S, E)
        left = count - (r * S).astype(f32)
        srow = lax.broadcasted_iota(jnp.int32, (S, 1), 0).astype(f32)
        ye = buf_ref[s]
        parts = [jnp.where(srow < left[e:e + 1], ye[e * S:(e + 1) * S], 0.0) for e in range(E)]
        ye = jnp.concatenate(parts, axis=0).astype(bf16)
        return lax.dot_general(p_ref[...], ye, (((0,), (0,)), ((), ())), preferred_element_type=f32)

    for cp in fetch(j, 0, slot):
        cp.wait()
    moe = one_round(jnp.int32(0), slot)

    def extra(r, acc):
        for cp in fetch(j, r * S, 2):
            cp.start()
        for cp in fetch(j, r * S, 2):
            cp.wait()
        return acc + one_round(r, 2)

    moe = lax.fori_loop(1, rounds_ref[j], extra, moe)
    o_ref[...] = _layer_norm(alpha * x1_ref[...] + moe, g_ref[...], b_ref[...])


def _route_scatter(sel, x1, ye, off, rounds, ln_g, ln_b, alpha):
    E, n = sel.shape
    D = x1.shape[1]
    Tb, S = ROUTE_BLOCK, ROUTE_SLOTS
    nblk = n // Tb
    return pl.pallas_call(
        functools.partial(_scatter_body, E=E, nblk=nblk, alpha=alpha),
        grid_spec=pltpu.PrefetchScalarGridSpec(
            num_scalar_prefetch=2,
            grid=(nblk,),
            in_specs=[pl.BlockSpec((E, Tb), lambda j, *_: (0, j)),
                      pl.BlockSpec((Tb, D), lambda j, *_: (j, 0)),
                      pl.BlockSpec((1, D), lambda j, *_: (0, 0)),
                      pl.BlockSpec((1, D), lambda j, *_: (0, 0)),
                      pl.BlockSpec(memory_space=pl.ANY)],
            out_specs=pl.BlockSpec((Tb, D), lambda j, *_: (j, 0)),
            scratch_shapes=[pltpu.VMEM((E * S, Tb), bf16),
                            pltpu.VMEM((3, E * S, D), f32),
                            pltpu.SemaphoreType.DMA((3,))]),
        out_shape=jax.ShapeDtypeStruct((n, D), f32),
        compiler_params=_cparams("arbitrary"),
        name="route_scatter",
    )(off, rounds, sel, x1, ln_g, ln_b, ye)


def _expert_choice(x1, aff, w1, w3, w2, ln_g, ln_b, alpha):
    n, D = x1.shape
    E = aff.shape[1]
    cap = EC_CAPACITY * n // E
    Tb, S = ROUTE_BLOCK, ROUTE_SLOTS
    nblk = n // Tb
    sel = _select(aff.T, cap)
    cnt = sel.reshape(E, nblk, Tb).sum(-1).astype(jnp.int32)
    cnt_al = (cnt + ROW_ALIGN - 1) // ROW_ALIGN * ROW_ALIGN
    end = jnp.cumsum(cnt_al, axis=1)
    off = (end - cnt_al).reshape(E * nblk)
    total = end[:, -1]
    rounds = jnp.maximum((jnp.max(cnt, axis=0) + S - 1) // S, 1)
    rows_pad = -(-(cap + ROW_ALIGN * nblk + S) // ROW_TILE) * ROW_TILE
    xe = _route_gather(sel, x1, aff, off, rounds, total, rows_pad)
    ye = _ffn(total, xe, w1, w3, w2)
    return _route_scatter(sel, x1, ye, off, rounds, ln_g, ln_b, alpha)


FILT_FEAT_PAD = 32


def _lag_features(L):
    n = np.arange(2 * L, dtype=np.float64)
    pos = np.where(n < L, n, 2 * L - n)
    t = pos / (L - 1)
    bands = np.linspace(1e-4, FILT_BANDS - 1, FILT_BANDS)
    ang = 2.0 * math.pi * (pos / L)[:, None] * bands[None, :]
    z = np.concatenate([t[:, None], np.cos(ang), -np.sin(ang)], axis=-1)
    out = np.zeros((2 * L, FILT_FEAT_PAD), np.float32)
    out[:, :z.shape[1]] = z
    return out


def _filter_body(z_ref, w1_ref, b1_ref, f1_ref, w2_ref, b2_ref, f2_ref, w3_ref, dec_ref, o_ref, *, L):
    hp = lax.Precision.HIGHEST
    z = z_ref[...]
    h = jnp.sin(f1_ref[...] * (jnp.dot(z, w1_ref[...], precision=hp, preferred_element_type=f32) + b1_ref[...]))
    h = jnp.sin(f2_ref[...] * (jnp.dot(h, w2_ref[...], precision=hp, preferred_element_type=f32) + b2_ref[...]))
    h = jnp.dot(h, w3_ref[...], precision=hp, preferred_element_type=f32)
    window = jnp.exp(-z[:, 0:1] * jnp.abs(dec_ref[...])) + FILT_SHIFT
    tl = z.shape[0]
    row = pl.program_id(0) * tl + lax.broadcasted_iota(jnp.int32, h.shape, 0)
    o_ref[...] = jnp.where(row == L, 0.0, h * window)


def _circ_filters(L, w1, b1, f1, w2, b2, f2, w3, decay):
    nh = w1.shape[1]
    oc = w3.shape[1] // 2
    feats = jnp.asarray(_lag_features(L))
    w1p = jnp.zeros((FILT_FEAT_PAD, nh), f32).at[:w1.shape[0]].set(w1)
    tl = min(ROW_TILE, L)
    half = L // tl
    dirsel = lambda i: (i // half, 0, 0)
    return pl.pallas_call(
        functools.partial(_filter_body, L=L),
        grid=(2 * L // tl,),
        in_specs=[pl.BlockSpec((tl, FILT_FEAT_PAD), lambda i: (i, 0)),
                  _const_spec((FILT_FEAT_PAD, nh)), _const_spec((1, nh)), _const_spec((1, nh)),
                  _const_spec((nh, nh)), _const_spec((1, nh)), _const_spec((1, nh)),
                  pl.BlockSpec((None, nh, oc), dirsel),
                  pl.BlockSpec((None, 1, oc), dirsel)],
        out_specs=pl.BlockSpec((tl, oc), lambda i: (i, 0)),
        out_shape=jax.ShapeDtypeStruct((2 * L, oc), f32),
        compiler_params=_cparams("parallel"),
        name="hyena_filter",
    )(feats, w1p, b1.reshape(1, nh), f1.reshape(1, nh), w2, b2.reshape(1, nh), f2.reshape(1, nh),
      w3.reshape(nh, 2, oc).transpose(1, 0, 2), decay.reshape(2, 1, oc))


DFT_N2 = 128
COL_TILE = 4096


def _dft_tables(L):
    N, N2 = 2 * L, DFT_N2
    N1 = N // N2
    H1 = N1 // 2
    K1n = H1 + 1
    K1p = -(-K1n // 8) * 8
    k1 = np.arange(K1n)
    n1 = np.arange(N1)
    ang = 2.0 * math.pi * ((k1[:, None] * n1[None, :]) % N1) / N1
    f1 = np.zeros((2 * K1p, N1))
    f1[:K1n] = np.cos(ang)
    f1[K1p:K1p + K1n] = -np.sin(ang)
    wgt = np.full(K1n, 2.0)
    wgt[0] = wgt[H1] = 1.0
    ang = 2.0 * math.pi * ((np.arange(H1)[:, None] * k1[None, :]) % N1) / N1
    if1 = np.zeros((H1, 2 * K1p))
    if1[:, :K1n] = wgt * np.cos(ang) / N
    if1[:, K1p:K1p + K1n] = -wgt * np.sin(ang) / N
    k2 = np.arange(N2)
    ang = 2.0 * math.pi * ((k2[:, None] * k2[None, :]) % N2) / N2
    f2r, f2i = np.cos(ang), -np.sin(ang)
    ang = 2.0 * math.pi * (k1[:, None] * k2[None, :]) / N
    twr = np.zeros((K1p, 1, N2))
    twi = np.zeros((K1p, 1, N2))
    twr[:K1n, 0], twi[:K1n, 0] = np.cos(ang), -np.sin(ang)
    c = lambda a, dt: jnp.asarray(a.astype(np.float32)).astype(dt)
    return dict(N1=N1, H1=H1, K1p=K1p, f1=c(f1, bf16), if1=c(if1, bf16), f2r=c(f2r, f32), f2i=c(f2i, f32),
                twr=c(twr, f32), twi=c(twi, f32))


def _lmat_body(w_ref, x_ref, o_ref):
    o_ref[0] = jnp.dot(w_ref[...], x_ref[0].astype(bf16), preferred_element_type=f32).astype(o_ref.dtype)


def _dft_stage1(w, x):
    Bx, Kd, cols = x.shape
    Mo = w.shape[0]
    tc = min(COL_TILE, cols)
    return pl.pallas_call(
        _lmat_body,
        grid=(Bx, cols // tc),
        in_specs=[_const_spec((Mo, Kd)), pl.BlockSpec((1, Kd, tc), lambda b, j: (b, 0, j))],
        out_specs=pl.BlockSpec((1, Mo, tc), lambda b, j: (b, 0, j)),
        out_shape=jax.ShapeDtypeStruct((Bx, Mo, cols), bf16),
        compiler_params=_cparams("parallel", "parallel"),
        name="dft_stage1",
    )(w, x)


def _build_stage2_matrix(f2r_ref, f2i_ref, twr_ref, twi_ref, m_ref, mt_ref):
    n2 = f2r_ref.shape[0]
    twr, twi = twr_ref[...], twi_ref[...]
    re = f2r_ref[...] * twr - f2i_ref[...] * twi
    im = f2r_ref[...] * twi + f2i_ref[...] * twr
    m_ref[:n2, :n2] = re.astype(bf16)
    m_ref[:n2, n2:] = (-im).astype(bf16)
    m_ref[n2:, :n2] = im.astype(bf16)
    m_ref[n2:, n2:] = re.astype(bf16)
    if mt_ref is not None:
        ret, imt = re.T, im.T
        mt_ref[:n2, :n2] = ret.astype(bf16)
        mt_ref[:n2, n2:] = imt.astype(bf16)
        mt_ref[n2:, :n2] = (-imt).astype(bf16)
        mt_ref[n2:, n2:] = ret.astype(bf16)


def _spectrum_body(f2r_ref, f2i_ref, twr_ref, twi_ref, a_ref, x_ref, m_ref):
    _build_stage2_matrix(f2r_ref, f2i_ref, twr_ref, twi_ref, m_ref, None)
    n2, cw = a_ref.shape[1:]
    x = jnp.dot(m_ref[...], a_ref[...].reshape(2 * n2, cw), preferred_element_type=f32)
    x_ref[...] = x.reshape(2, n2, cw)


def _filter_spectrum(tab, a5):
    _, _, K1p, N2, Cw = a5.shape
    tw_spec = pl.BlockSpec((None, 1, N2), lambda k: (k, 0, 0))
    return pl.pallas_call(
        _spectrum_body,
        grid=(K1p,),
        in_specs=[_const_spec((N2, N2)), _const_spec((N2, N2)), tw_spec, tw_spec,
                  pl.BlockSpec((None, 2, None, N2, Cw), lambda k: (0, 0, k, 0, 0))],
        out_specs=pl.BlockSpec((None, 2, N2, Cw), lambda k: (k, 0, 0, 0)),
        out_shape=jax.ShapeDtypeStruct((K1p, 2, N2, Cw), f32),
        scratch_shapes=[pltpu.VMEM((2 * N2, 2 * N2), bf16)],
        compiler_params=_cparams("arbitrary"),
        name="filter_spectrum",
    )(tab['f2r'], tab['f2i'], tab['twr'], tab['twi'], a5)


def _freq_body(f2r_ref, f2i_ref, twr_ref, twi_ref, a_ref, h_ref, g_ref, m_ref, mt_ref):
    @pl.when(pl.program_id(1) == 0)
    def _():
        _build_stage2_matrix(f2r_ref, f2i_ref, twr_ref, twi_ref, m_ref, mt_ref)

    n2, c = a_ref.shape[1:]
    x = jnp.dot(m_ref[...], a_ref[...].reshape(2 * n2, c), preferred_element_type=f32)
    xr, xi = x[:n2], x[n2:]
    hr, hi = h_ref[0], h_ref[1]
    y = jnp.concatenate([xr * hr - xi * hi, xr * hi + xi * hr], axis=0).astype(bf16)
    g = jnp.dot(mt_ref[...], y, preferred_element_type=f32)
    g_ref[...] = g.reshape(2, n2, c).astype(bf16)


def _freq_stage(tab, a5, hspec, order):
    B, _, K1p, N2, C = a5.shape
    tw_spec = pl.BlockSpec((None, 1, N2), lambda k, b: (k, 0, 0))
    slab = pl.BlockSpec((None, 2, None, N2, C), lambda k, b: (b, 0, k, 0, 0))
    return pl.pallas_call(
        _freq_body,
        grid=(K1p, B),
        in_specs=[_const_spec((N2, N2)), _const_spec((N2, N2)), tw_spec, tw_spec, slab,
                  pl.BlockSpec((None, 2, N2, C), lambda k, b: (k, 0, 0, order))],
        out_specs=slab,
        out_shape=jax.ShapeDtypeStruct(a5.shape, bf16),
        scratch_shapes=[pltpu.VMEM((2 * N2, 2 * N2), bf16), pltpu.VMEM((2 * N2, 2 * N2), bf16)],
        compiler_params=_cparams("arbitrary", "arbitrary"),
        name="freq_stage",
    )(tab['f2r'], tab['f2i'], tab['twr'], tab['twi'], a5, hspec)


def _conv_out_body(w_ref, g_ref, z_ref, gate_ref, bias_ref, o_ref):
    y = jnp.dot(w_ref[...], g_ref[0], preferred_element_type=f32)
    o_ref[0] = gate_ref[0] * (y + z_ref[0] * bias_ref[...])


def _conv_out(tab, g2, z1, gate1, bias_t):
    B, H1, cols = z1.shape
    Kd = g2.shape[1]
    tc = min(COL_TILE, cols)
    blk = pl.BlockSpec((1, H1, tc), lambda b, j: (b, 0, j))
    return pl.pallas_call(
        _conv_out_body,
        grid=(B, cols // tc),
        in_specs=[_const_spec((H1, Kd)), pl.BlockSpec((1, Kd, tc), lambda b, j: (b, 0, j)), blk, blk,
                  pl.BlockSpec((1, tc), lambda b, j: (0, j))],
        out_specs=blk,
        out_shape=jax.ShapeDtypeStruct(z1.shape, f32),
        compiler_params=_cparams("parallel", "parallel"),
        name="conv_out",
    )(tab['if1'], g2, z1, gate1, bias_t)


def _long_conv(v, g1, g2, circ, hy_bias, B, L, C):
    tab = _dft_tables(L)
    N2, N1, H1, K1p = DFT_N2, tab['N1'], tab['H1'], tab['K1p']
    Cw = circ.shape[1]
    a5 = _dft_stage1(tab['f1'], circ.reshape(1, N1, N2 * Cw)).reshape(1, 2, K1p, N2, Cw)
    hspec = _filter_spectrum(tab, a5)
    z1 = v.reshape(B, H1, N2 * C)
    gates = (g1.reshape(B, H1, N2 * C), g2.reshape(B, H1, N2 * C))
    f1h = tab['f1'][:, :H1]
    for o in range(HY_ORDER):
        a5 = _dft_stage1(f1h, z1).reshape(B, 2, K1p, N2, C)
        g2d = _freq_stage(tab, a5, hspec, o).reshape(B, 2 * K1p, N2 * C)
        z1 = _conv_out(tab, g2d, z1, gates[o], jnp.tile(hy_bias[o], N2).reshape(1, N2 * C))
    return z1.reshape(B * L, C)


def _encoder_layer(x, p, alpha):
    B, L, D = x.shape
    n = B * L
    x2 = x.reshape(n, D)
    G, P, H = p['s5_b_re'].shape[1:]
    s5w = G * H
    hyw = p['w_hy_proj'].shape[0]

    u_s5, u_hy, g_s, g_h = _inproj(x2, p['w_in'].astype(bf16), B, L, s5w, hyw)

    u_tb = u_s5.reshape(L * B, s5w)
    dvec = p['s5_d'].astype(f32).reshape(1, s5w)
    y = jnp.zeros_like(u_tb)
    for direction in range(2):
        a_bar, b_bar = _s5_params(p['s5_lambda_re'][direction], p['s5_lambda_im'][direction],
                                  p['s5_log_dt'][direction], p['s5_b_re'][direction],
                                  p['s5_b_im'][direction])
        bmat, a_re, a_im, cmat = _s5_matrices(a_bar, b_bar, p['s5_c_re'], p['s5_c_im'])
        y = _s5_pass(u_tb, bmat, a_re, a_im, cmat, dvec, y, B,
                     reverse=(direction == 1), add_skip=(direction == 0))
    y_s5 = y.reshape(L, B * s5w)

    v, g1, g2 = _shortconv(u_hy, p['hy_short_w'], p['hy_short_b'], B, L, hyw)
    circ = _circ_filters(L, p['filt_w1'], p['filt_b1'], p['filt_freq1'], p['filt_w2'], p['filt_b2'],
                         p['filt_freq2'], p['filt_w3'], p['filt_decay'])
    z = _long_conv(v, g1, g2, circ, p['hy_bias'], B, L, hyw)

    x1, aff = _merge(x2, y_s5, z, g_s, g_h, p['s5_w_glu'].astype(bf16), p['w_s5_proj'].astype(bf16),
                     p['w_hy_proj'].astype(bf16), p['w_out'].astype(bf16),
                     p['ln1_g'].reshape(1, D), p['ln1_b'].reshape(1, D), p['w_router'], B, L, alpha)

    out = _expert_choice(x1, aff, p['ex_w1_bf16'], p['ex_w3_bf16'], p['ex_w2_bf16'],
                         p['ln2_g'].reshape(1, D), p['ln2_b'].reshape(1, D), alpha)
    return out.reshape(B, L, D)


_PARAM_NAMES = ('w_in', 's5_lambda_re', 's5_lambda_im', 's5_log_dt', 's5_b_re', 's5_b_im',
                's5_c_re', 's5_c_im', 's5_d', 's5_w_glu', 'w_s5_proj',
                'hy_short_w', 'hy_short_b', 'filt_w1', 'filt_b1', 'filt_freq1', 'filt_w2', 'filt_b2',
                'filt_freq2', 'filt_w3', 'filt_decay', 'hy_bias', 'w_hy_proj', 'w_out',
                'ln1_g', 'ln1_b', 'w_router', 'ex_w1', 'ex_w3', 'ex_w2', 'ln2_g', 'ln2_b')


def kernel(x_prompt, x_sample, w_in, s5_lambda_re, s5_lambda_im, s5_log_dt, s5_b_re, s5_b_im, s5_c_re, s5_c_im, s5_d, s5_w_glu, w_s5_proj, hy_short_w, hy_short_b, filt_w1, filt_b1, filt_freq1, filt_w2, filt_b2, filt_freq2, filt_w3, filt_decay, hy_bias, w_hy_proj, w_out, ln1_g, ln1_b, w_router, ex_w1, ex_w3, ex_w2, ln2_g, ln2_b):
    stacked = (w_in, s5_lambda_re, s5_lambda_im, s5_log_dt, s5_b_re, s5_b_im, s5_c_re, s5_c_im, s5_d,
               s5_w_glu, w_s5_proj, hy_short_w, hy_short_b, filt_w1, filt_b1, filt_freq1, filt_w2,
               filt_b2, filt_freq2, filt_w3, filt_decay, hy_bias, w_hy_proj, w_out, ln1_g, ln1_b,
               w_router, ex_w1, ex_w3, ex_w2, ln2_g, ln2_b)
    depth = w_in.shape[0]
    alpha = (2.0 * depth) ** 0.25
    outs = []
    for x in (x_prompt, x_sample):
        for l in range(depth):
            p = {k: v[l] for k, v in zip(_PARAM_NAMES, stacked)}
            for k in ('ex_w1', 'ex_w3', 'ex_w2'):
                p[k + '_bf16'] = p[k].astype(bf16)
            x = _encoder_layer(x, p, alpha)
        outs.append(x)
    return tuple(outs)
```

```python
import functools
import math

import jax
import jax.numpy as jnp
import numpy as np
from jax import lax
from jax.experimental import pallas as pl
from jax.experimental.pallas import tpu as pltpu

FILT_BANDS = 8
FILT_SHIFT = 0.05
EC_CAPACITY = 2
LN_EPS = 1e-5
HY_ORDER = 2

VMEM_LIMIT_BYTES = 56 * 1024 * 1024
ROW_TILE = 512

bf16 = jnp.bfloat16
f32 = jnp.float32


def _cparams(*sem):
    return pltpu.CompilerParams(dimension_semantics=sem, vmem_limit_bytes=VMEM_LIMIT_BYTES)


def _const_spec(shape):
    return pl.BlockSpec(shape, lambda *_: (0,) * len(shape))


def _inproj_body(x_ref, w_ref, us_ref, uh_ref, gs_ref, gh_ref, *, o1, o2, o3):
    proj = jnp.dot(x_ref[...].astype(bf16), w_ref[...], preferred_element_type=f32)
    us_ref[...] = proj[:, :o1].astype(us_ref.dtype)
    uh_ref[...] = proj[:, o1:o2]
    gs_ref[...] = jax.nn.sigmoid(proj[:, o2:o3])
    gh_ref[...] = jax.nn.sigmoid(proj[:, o3:])


def _inproj(x2, w_in, B, L, s5w, hyw):
    n, D = x2.shape
    cols = w_in.shape[1]
    o1, o2 = s5w, s5w + 3 * hyw
    o3 = o2 + D
    tm = min(ROW_TILE, L)
    nt = L // tm
    return pl.pallas_call(
        functools.partial(_inproj_body, o1=o1, o2=o2, o3=o3),
        grid=(B, nt),
        in_specs=[pl.BlockSpec((tm, D), lambda b, i: (b * nt + i, 0)),
                  _const_spec((D, cols))],
        out_specs=[pl.BlockSpec((tm, s5w), lambda b, i: (b * nt + i, 0)),
                   pl.BlockSpec((tm, 3 * hyw), lambda b, i: (b * nt + i, 0)),
                   pl.BlockSpec((tm, D), lambda b, i: (b * nt + i, 0)),
                   pl.BlockSpec((tm, D), lambda b, i: (b * nt + i, 0))],
        out_shape=[jax.ShapeDtypeStruct((n, s5w), bf16),
                   jax.ShapeDtypeStruct((n, 3 * hyw), f32),
                   jax.ShapeDtypeStruct((n, D), f32),
                   jax.ShapeDtypeStruct((n, D), f32)],
        compiler_params=_cparams("parallel", "parallel"),
        name="inproj",
    )(x2, w_in)


S5_CHUNK = 16
S5_ROWS = 8


def _s5_body(u_ref, kmat_ref, winf_ref, winb_ref, woutf_ref, woutb_ref, af_ref, ab_ref, y_ref,
             vf_ref, vb_ref, sf_ref, sb_ref, *, R, nc, chained):
    u = u_ref[0]
    vf_ref[...] = jnp.dot(u, winf_ref[0], preferred_element_type=f32)
    vb_ref[...] = jnp.dot(u, winb_ref[0], preferred_element_type=f32)
    hw = sf_ref.shape[1]
    bc = lambda ref, row, lo: jnp.broadcast_to(ref[0, row:row + 1, lo:lo + hw], (R, hw))
    decay = lambda ref, row: ((bc(ref, row, 0), bc(ref, row, hw)), (bc(ref, row + 1, 0), bc(ref, row + 1, hw)))
    f_a1, f_a2 = decay(af_ref, 0)
    b_a1, b_a2 = decay(ab_ref, 0)

    def mul_add(s0, s1, a1, a2, v0, v1):
        return a1[0] * s0 + a2[0] * s1 + v0, a1[1] * s1 + a2[1] * s0 + v1

    def scan(init, store):
        def step(i, carry):
            f0, f1, b0, b1 = carry
            rf = pl.multiple_of(i * R, R)
            rb = pl.multiple_of((nc - 1 - i) * R, R)
            if store:
                sf_ref[pl.ds(rf, R), :] = f0
                sb_ref[pl.ds(rb, R), :] = b0
            f0, f1 = mul_add(f0, f1, f_a1, f_a2, vf_ref[pl.ds(rf, R), :hw], vf_ref[pl.ds(rf, R), hw:])
            b0, b1 = mul_add(b0, b1, b_a1, b_a2, vb_ref[pl.ds(rb, R), :hw], vb_ref[pl.ds(rb, R), hw:])
            return f0, f1, b0, b1
        return lax.fori_loop(0, nc, step, init, unroll=4)

    zero = jnp.zeros((R, hw), f32)
    init = (zero, zero, zero, zero)
    if chained:
        ef0, ef1, eb0, eb1 = scan(init, store=False)
        fn_a1, fn_a2 = decay(af_ref, 2)
        bn_a1, bn_a2 = decay(ab_ref, 2)
        row = lax.broadcasted_iota(jnp.int32, (R, hw), 0)
        down = lambda v: jnp.where(row == 0, 0.0, pltpu.roll(v, 1, axis=0))
        up = lambda v: jnp.where(row == R - 1, 0.0, pltpu.roll(v, R - 1, axis=0))
        f0, f1, b0, b1 = init
        for _ in range(R - 1):
            t0, t1 = mul_add(f0, f1, fn_a1, fn_a2, ef0, ef1)
            f0, f1 = down(t0), down(t1)
            t0, t1 = mul_add(b0, b1, bn_a1, bn_a2, eb0, eb1)
            b0, b1 = up(t0), up(t1)
        init = (f0, f1, b0, b1)
    scan(init, store=True)
    y = jnp.dot(u, kmat_ref[0], preferred_element_type=f32)
    y = y + jnp.dot(sf_ref[...].astype(bf16), woutf_ref[0], preferred_element_type=f32)
    y = y + jnp.dot(sb_ref[...].astype(bf16), woutb_ref[0], preferred_element_type=f32)
    y_ref[0] = y


def _s5(u_g, mats, R, chained):
    G, M, W = u_g.shape
    kmat, winf, winb, woutf, woutb, af, ab = mats
    hw = woutf.shape[1]
    grp = lambda shape: pl.BlockSpec((1,) + shape, lambda g: (g, 0, 0))
    return pl.pallas_call(
        functools.partial(_s5_body, R=R, nc=M // R, chained=chained),
        grid=(G,),
        in_specs=[grp((M, W)), grp((W, W)), grp((W, 2 * hw)), grp((W, 2 * hw)), grp((hw, W)), grp((hw, W)),
                  grp((4, 2 * hw)), grp((4, 2 * hw))],
        out_specs=grp((M, W)),
        out_shape=jax.ShapeDtypeStruct((G, M, W), f32),
        scratch_shapes=[pltpu.VMEM((M, 2 * hw), f32), pltpu.VMEM((M, 2 * hw), f32),
                        pltpu.VMEM((M, hw), f32), pltpu.VMEM((M, hw), f32)],
        compiler_params=_cparams("parallel"),
        name="s5_chunked",
    )(u_g, kmat, winf, winb, woutf, woutb, af, ab)


def _s5_discretise(lam_re, lam_im, log_dt, b_re, b_im, powers):
    lam = lax.complex(-jnp.abs(lam_re.astype(f32)), lam_im.astype(f32))
    dt = jnp.exp(log_dt.astype(f32))[:, None]
    a_bar = jnp.exp(lam * dt)
    k = jnp.asarray(powers, f32)[None, :, None]
    apow = jnp.exp((lam * dt)[:, None, :] * k)
    b_bar = ((a_bar - 1.0) / lam)[..., None] * lax.complex(b_re.astype(f32), b_im.astype(f32))
    return apow, b_bar


def _s5_matrices(p, seg_steps):
    Tc = S5_CHUNK
    powers = list(range(Tc + 1)) + [seg_steps]
    apf, bbf = _s5_discretise(p['s5_lambda_re'][0], p['s5_lambda_im'][0], p['s5_log_dt'][0],
                              p['s5_b_re'][0], p['s5_b_im'][0], powers)
    apb, bbb = _s5_discretise(p['s5_lambda_re'][1], p['s5_lambda_im'][1], p['s5_log_dt'][1],
                              p['s5_b_re'][1], p['s5_b_im'][1], powers)
    c = lax.complex(p['s5_c_re'].astype(f32), p['s5_c_im'].astype(f32))
    G, H, P = c.shape
    W = Tc * H
    kf = jnp.real(jnp.einsum('ghp,gtp,gpk->gthk', c, apf[:, :Tc], bbf))
    kb = jnp.real(jnp.einsum('ghp,gtp,gpk->gthk', c, apb[:, :Tc], bbb))
    tau = jnp.arange(Tc)[None, :] - jnp.arange(Tc)[:, None]
    blk = (jnp.where((tau >= 0)[None, :, :, None, None], kf[:, jnp.abs(tau)], 0.0)
           + jnp.where((tau <= 0)[None, :, :, None, None], kb[:, jnp.abs(tau)], 0.0))
    kmat = jnp.transpose(blk, (0, 1, 4, 2, 3)).reshape(G, W, W)
    kmat = kmat + jnp.eye(W, dtype=f32)[None] * jnp.tile(p['s5_d'].astype(f32), (1, Tc))[:, None, :]
    lanes = lambda z: jnp.concatenate([jnp.real(z), jnp.imag(z), jnp.imag(z), jnp.real(z)], axis=-1)
    win = lambda ap, bb: lanes(jnp.einsum('gsp,gpk->gskp', ap, bb)).reshape(G, W, 4 * P)
    winf = win(apf[:, Tc - 1::-1][:, :Tc], bbf)
    winb = win(apb[:, :Tc], bbb)
    def wout(ap):
        z = jnp.einsum('ghp,gtp->gpth', c, ap).reshape(G, P, W)
        return jnp.concatenate([jnp.real(z), -jnp.imag(z)], axis=1)
    woutf = wout(apf[:, 1:Tc + 1])
    woutb = wout(apb[:, Tc:0:-1])
    def chunk_decay(ap):
        rows = []
        for k in (Tc, Tc + 1):
            ar, ai = jnp.real(ap[:, k]), jnp.imag(ap[:, k])
            rows += [jnp.concatenate([ar, ar, ar, ar], -1), jnp.concatenate([-ai, ai, ai, -ai], -1)]
        return jnp.stack(rows, axis=1)
    cast = lambda m: m.astype(bf16)
    return (cast(kmat), cast(winf), cast(winb), cast(woutf), cast(woutb), chunk_decay(apf), chunk_decay(apb))


def _shortconv_body(prev_ref, main_ref, next_ref, w_ref, b_ref, v_ref, g1_ref, g2_ref, *, nt, hyw):
    i = pl.program_id(1)
    x = main_ref[...]
    tl = x.shape[0]
    rows = lax.broadcasted_iota(jnp.int32, x.shape, 0)
    prev_row = jnp.where(i == 0, 0.0, prev_ref[7:8, :])
    next_row = jnp.where(i == nt - 1, 0.0, next_ref[0:1, :])
    xm1 = jnp.where(rows == 0, prev_row, pltpu.roll(x, 1, axis=0))
    xp1 = jnp.where(rows == tl - 1, next_row, pltpu.roll(x, tl - 1, axis=0))
    w = w_ref[...]
    hy = xm1 * w[0:1, :] + x * w[1:2, :] + xp1 * w[2:3, :] + b_ref[...]
    v_ref[...] = hy[:, :hyw]
    g1_ref[...] = hy[:, hyw:2 * hyw]
    g2_ref[...] = hy[:, 2 * hyw:]


def _shortconv(u_hy, w, b, B, L, hyw):
    n, C = u_hy.shape
    tl = min(ROW_TILE, L)
    nt = L // tl
    r8 = tl // 8
    nblk8 = n // 8
    main = lambda bb, i: (bb * nt + i, 0)
    prev = lambda bb, i: (jnp.maximum((bb * nt + i) * r8 - 1, 0), 0)
    nxt = lambda bb, i: (jnp.minimum((bb * nt + i + 1) * r8, nblk8 - 1), 0)
    out = jax.ShapeDtypeStruct((n, hyw), f32)
    return pl.pallas_call(
        functools.partial(_shortconv_body, nt=nt, hyw=hyw),
        grid=(B, nt),
        in_specs=[pl.BlockSpec((8, C), prev),
                  pl.BlockSpec((tl, C), main),
                  pl.BlockSpec((8, C), nxt),
                  _const_spec((3, C)),
                  _const_spec((1, C))],
        out_specs=[pl.BlockSpec((tl, hyw), main)] * 3,
        out_shape=[out, out, out],
        compiler_params=_cparams("parallel", "parallel"),
        name="shortconv",
    )(u_hy, u_hy, u_hy, w, b.reshape(1, C))


def _layer_norm(v, g, b):
    mu = jnp.mean(v, axis=-1, keepdims=True)
    c = v - mu
    var = jnp.mean(c * c, axis=-1, keepdims=True)
    return c * lax.rsqrt(var + LN_EPS) * g + b


def _merge_body(x_ref, ys_ref, z_ref, gs_ref, gh_ref, wglu_ref, wsp_ref, whp_ref, wout_ref,
                g_ref, b_ref, wrh_ref, wrl_ref, x1_ref, aff_ref, *, alpha):
    ys = jax.nn.gelu(ys_ref[...])
    gate = jax.nn.sigmoid(jnp.dot(ys.astype(bf16), wglu_ref[...], preferred_element_type=f32))
    branch_s = jnp.dot((ys * gate).astype(bf16), wsp_ref[...], preferred_element_type=f32)
    branch_h = jnp.dot(z_ref[...].astype(bf16), whp_ref[...], preferred_element_type=f32)
    mix = gs_ref[...] * branch_s + gh_ref[...] * branch_h
    mix = jnp.dot(mix.astype(bf16), wout_ref[...], preferred_element_type=f32)
    x1 = _layer_norm(alpha * x_ref[...] + mix, g_ref[...], b_ref[...])
    x1_ref[...] = x1
    x1_hi = x1.astype(bf16)
    x1_lo = (x1 - x1_hi.astype(f32)).astype(bf16)
    logits = (jnp.dot(x1_hi, wrh_ref[...], preferred_element_type=f32)
              + jnp.dot(x1_lo, wrh_ref[...], preferred_element_type=f32)
              + jnp.dot(x1_hi, wrl_ref[...], preferred_element_type=f32))
    m = jnp.max(logits, axis=-1, keepdims=True)
    e = jnp.exp(logits - m)
    aff_ref[...] = e / jnp.sum(e, axis=-1, keepdims=True)


def _merge(x2, ys, z, g_s, g_h, w_glu, w_sp, w_hp, w_out, ln_g, ln_b, w_router, B, L, alpha):
    n, D = x2.shape
    s5w, hyw = ys.shape[1], z.shape[1]
    E = w_router.shape[1]
    tm = min(ROW_TILE, L)
    nt = L // tm
    row = lambda b, i: (b * nt + i, 0)
    wr_hi = w_router.astype(bf16)
    wr_lo = (w_router - wr_hi.astype(f32)).astype(bf16)
    return pl.pallas_call(
        functools.partial(_merge_body, alpha=alpha),
        grid=(B, nt),
        in_specs=[pl.BlockSpec((tm, D), row),
                  pl.BlockSpec((tm, s5w), row),
                  pl.BlockSpec((tm, hyw), row),
                  pl.BlockSpec((tm, D), row),
                  pl.BlockSpec((tm, D), row),
                  _const_spec(w_glu.shape), _const_spec(w_sp.shape), _const_spec(w_hp.shape),
                  _const_spec(w_out.shape), _const_spec((1, D)), _const_spec((1, D)),
                  _const_spec(w_router.shape), _const_spec(w_router.shape)],
        out_specs=[pl.BlockSpec((tm, D), row), pl.BlockSpec((tm, E), row)],
        out_shape=[jax.ShapeDtypeStruct((n, D), f32), jax.ShapeDtypeStruct((n, E), f32)],
        compiler_params=_cparams("parallel", "parallel"),
        name="merge",
    )(x2, ys, z, g_s, g_h, w_glu, w_sp, w_hp, w_out, ln_g, ln_b, wr_hi, wr_lo)


ROUTE_BLOCK = 256
ROUTE_SLOTS = 48
ROW_ALIGN = 8
GATE_LANES = 128


def _select_body(aff_ref, sel_ref, *, cap, idx_bits):
    bits = pltpu.bitcast(aff_ref[...], jnp.int32)
    E = bits.shape[0]
    count = lambda m: jnp.sum(jnp.where(m, 1.0, 0.0), axis=1, keepdims=True)

    def value_bit(i, prefix):
        cand = prefix | jnp.left_shift(jnp.int32(1), 30 - i)
        return jnp.where(count(bits >= cand) >= cap, cand, prefix)

    thr = lax.fori_loop(0, 31, value_bit, jnp.zeros((E, 1), jnp.int32))
    need = cap - count(bits > thr)
    idx = lax.broadcasted_iota(jnp.int32, bits.shape, 1)
    tie_idx = jnp.where(bits == thr, idx, jnp.int32(2 ** 30))

    def index_bit(i, bound):
        cand = bound | jnp.left_shift(jnp.int32(1), idx_bits - 1 - i)
        return jnp.where(count(tie_idx < cand) <= need, cand, bound)

    bound = lax.fori_loop(0, idx_bits, index_bit, jnp.zeros((E, 1), jnp.int32))
    sel_ref[...] = jnp.where(bits > thr, 1.0, jnp.where(tie_idx < bound, 1.0, 0.0))


def _select(aff_t, cap):
    E, n = aff_t.shape
    return pl.pallas_call(
        functools.partial(_select_body, cap=float(cap), idx_bits=int(n).bit_length()),
        out_shape=jax.ShapeDtypeStruct((E, n), f32),
        compiler_params=pltpu.CompilerParams(vmem_limit_bytes=VMEM_LIMIT_BYTES),
        name="expert_select",
    )(aff_t)


def _slot_onehot(sel, p_ref, first_slot, E):
    Tb = sel.shape[1]
    S = p_ref.shape[0] // E
    r = lax.broadcasted_iota(jnp.int32, (Tb, Tb), 0)
    c = lax.broadcasted_iota(jnp.int32, (Tb, Tb), 1)
    tri = jnp.where(r <= c, 1.0, 0.0).astype(bf16)
    incl = jnp.dot(sel.astype(bf16), tri, preferred_element_type=f32)
    slot = jnp.where(sel > 0.0, incl - 1.0, -1.0)
    want = (lax.broadcasted_iota(jnp.int32, (S, Tb), 0) + first_slot).astype(f32)
    for e in range(E):
        p_ref[e * S:(e + 1) * S, :] = jnp.where(slot[e:e + 1] == want, 1.0, 0.0).astype(bf16)
    return incl[:, Tb - 1:Tb]


def _gather_copies(stage_ref, xe_hbm, sem, off_ref, j, first_slot, E, S, nblk):
    copies = []
    for e in range(E):
        row = pl.multiple_of(off_ref[e * nblk + j] + first_slot, ROW_ALIGN)
        copies.append(pltpu.make_async_copy(stage_ref.at[e * S:(e + 1) * S], xe_hbm.at[e, pl.ds(row, S)], sem))
    return copies


def _gather_body(off_ref, rounds_ref, total_ref, sel_ref, x_ref, aff_ref, xe_hbm, p_ref, stage_ref, sem,
                 *, E, nblk, cap):
    j = pl.program_id(0)
    S = p_ref.shape[0] // E
    D = x_ref.shape[1]
    x = x_ref[...].astype(bf16)
    aff = aff_ref[...]
    a_hi = aff.astype(bf16)
    rem = aff - a_hi.astype(f32)
    a_mid = rem.astype(bf16)
    a_lo = (rem - a_mid.astype(f32)).astype(bf16)
    aff3 = jnp.concatenate([a_hi, a_mid, a_lo], axis=1)
    own = (lax.broadcasted_iota(jnp.int32, (E * S, 3 * E), 0) // S) == (
        lax.broadcasted_iota(jnp.int32, (E * S, 3 * E), 1) % E)

    def one_round(r, first_step):
        first_slot = r * S
        _slot_onehot(sel_ref[...], p_ref, first_slot, E)
        p = p_ref[...]
        rows = jnp.dot(p, x, preferred_element_type=f32)
        gates = jnp.dot(p, aff3, preferred_element_type=f32)
        gate = jnp.sum(jnp.where(own, gates, 0.0), axis=1, keepdims=True)

        @pl.when(jnp.logical_not(first_step))
        def _():
            for cp in _gather_copies(stage_ref, xe_hbm, sem, off_ref, j, 0, E, S, nblk):
                cp.wait()

        stage_ref[:, :D] = rows
        stage_ref[:, D:] = jnp.broadcast_to(gate, (E * S, GATE_LANES))
        for cp in _gather_copies(stage_ref, xe_hbm, sem, off_ref, j, first_slot, E, S, nblk):
            cp.start()

    one_round(0, j == 0)

    def extra(r, carry):
        one_round(r, False)
        return carry

    lax.fori_loop(1, rounds_ref[j], extra, 0)

    @pl.when(j == nblk - 1)
    def _():
        for cp in _gather_copies(stage_ref, xe_hbm, sem, off_ref, j, 0, E, S, nblk):
            cp.wait()
        rows_pad = xe_hbm.shape[1]
        stage_ref[:S] = jnp.zeros((S, D + GATE_LANES), f32)
        nfill = -(-(rows_pad - cap) // S)

        def fill(e, row):
            return pltpu.make_async_copy(stage_ref.at[:S], xe_hbm.at[e, pl.ds(pl.multiple_of(row, ROW_ALIGN), S)], sem)

        def whole_chunks(e, action):
            def body(k, c):
                row = total_ref[e] + k * S

                @pl.when(row + S <= rows_pad)
                def _():
                    action(fill(e, row))
                return c
            lax.fori_loop(0, nfill, body, 0)

        for e in range(E):
            whole_chunks(e, lambda cp: cp.start())
        for e in range(E):
            whole_chunks(e, lambda cp: cp.wait())
        for e in range(E):
            fill(e, rows_pad - S).start()
        for e in range(E):
            fill(e, rows_pad - S).wait()


def _route_gather(sel, x1, aff, off, rounds, total, rows_pad):
    E, n = sel.shape
    D = x1.shape[1]
    Tb, S = ROUTE_BLOCK, ROUTE_SLOTS
    nblk = n // Tb
    return pl.pallas_call(
        functools.partial(_gather_body, E=E, nblk=nblk, cap=EC_CAPACITY * n // E),
        grid_spec=pltpu.PrefetchScalarGridSpec(
            num_scalar_prefetch=3,
            grid=(nblk,),
            in_specs=[pl.BlockSpec((E, Tb), lambda j, *_: (0, j)),
                      pl.BlockSpec((Tb, D), lambda j, *_: (j, 0)),
                      pl.BlockSpec((Tb, E), lambda j, *_: (j, 0))],
            out_specs=pl.BlockSpec(memory_space=pl.ANY),
            scratch_shapes=[pltpu.VMEM((E * S, Tb), bf16),
                            pltpu.VMEM((E * S, D + GATE_LANES), f32),
                            pltpu.SemaphoreType.DMA(())]),
        out_shape=jax.ShapeDtypeStruct((E, rows_pad, D + GATE_LANES), f32),
        compiler_params=_cparams("arbitrary"),
        name="route_gather",
    )(off, rounds, total, sel, x1, aff)


def _ffn_body(total_ref, xe_ref, w1_ref, w3_ref, w2_ref, ye_ref):
    e, r = pl.program_id(0), pl.program_id(1)
    tr = xe_ref.shape[1]
    D = ye_ref.shape[2]

    @pl.when(r * tr < total_ref[e])
    def _():
        xe = xe_ref[0, :, :D].astype(bf16)
        gate = xe_ref[0, :, D:D + 1]
        h1 = jnp.dot(xe, w1_ref[0], preferred_element_type=f32)
        h3 = jnp.dot(xe, w3_ref[0], preferred_element_type=f32)
        h = (jax.nn.silu(h1) * h3).astype(bf16)
        ye = jnp.dot(h, w2_ref[0], preferred_element_type=f32) * gate
        row = r * tr + lax.broadcasted_iota(jnp.int32, ye.shape, 0)
        ye_ref[0] = jnp.where(row < total_ref[e], ye, 0.0)

    @pl.when(r * tr >= total_ref[e])
    def _():
        ye_ref[...] = jnp.zeros_like(ye_ref)


def _ffn(total, xe, w1, w3, w2):
    E, rows_pad, Dx = xe.shape
    D, F = w1.shape[1:]
    tr = ROW_TILE
    last = lambda e, tot: (tot[e] - 1) // tr
    rowmap = lambda e, r, tot: (e, jnp.minimum(r, last(e, tot)), 0)
    wmap = lambda e, r, tot: (e, 0, 0)
    return pl.pallas_call(
        _ffn_body,
        grid_spec=pltpu.PrefetchScalarGridSpec(
            num_scalar_prefetch=1,
            grid=(E, rows_pad // tr),
            in_specs=[pl.BlockSpec((1, tr, Dx), rowmap),
                      pl.BlockSpec((1, D, F), wmap), pl.BlockSpec((1, D, F), wmap),
                      pl.BlockSpec((1, F, D), wmap)],
            out_specs=pl.BlockSpec((1, tr, D), lambda e, r, tot: (e, r, 0))),
        out_shape=jax.ShapeDtypeStruct((E, rows_pad, D), f32),
        compiler_params=_cparams("arbitrary", "arbitrary"),
        name="expert_ffn",
    )(total, xe, w1, w3, w2)


def _scatter_copies(ye_hbm, buf_ref, sem, off_ref, j, first_slot, E, S, nblk):
    copies = []
    for e in range(E):
        row = pl.multiple_of(off_ref[e * nblk + j] + first_slot, ROW_ALIGN)
        copies.append(pltpu.make_async_copy(ye_hbm.at[e, pl.ds(row, S)], buf_ref.at[e * S:(e + 1) * S], sem))
    return copies


def _scatter_body(off_ref, rounds_ref, sel_ref, x1_ref, g_ref, b_ref, ye_hbm, o_ref, p_ref, buf_ref, sem,
                  *, E, nblk, alpha):
    j = pl.program_id(0)
    S = p_ref.shape[0] // E
    slot = lax.rem(j, 2)

    def fetch(jj, first_slot, s):
        return _scatter_copies(ye_hbm, buf_ref.at[s], sem.at[s], off_ref, jj, first_slot, E, S, nblk)

    @pl.when(j == 0)
    def _():
        for cp in fetch(j, 0, 0):
            cp.start()

    @pl.when(j + 1 < nblk)
    def _():
        for cp in fetch(j + 1, 0, 1 - slot):
            cp.start()

    def one_round(r, s):
        count = _slot_onehot(sel_ref[...], p_ref, r * S, E)
        left = count - (r * S).astype(f32)
        srow = lax.broadcasted_iota(jnp.int32, (S, 1), 0).astype(f32)
        ye = buf_ref[s]
        parts = [jnp.where(srow < left[e:e + 1], ye[e * S:(e + 1) * S], 0.0) for e in range(E)]
        ye = jnp.concatenate(parts, axis=0).astype(bf16)
        return lax.dot_general(p_ref[...], ye, (((0,), (0,)), ((), ())), preferred_element_type=f32)

    for cp in fetch(j, 0, slot):
        cp.wait()
    moe = one_round(jnp.int32(0), slot)

    def extra(r, acc):
        for cp in fetch(j, r * S, 2):
            cp.start()
        for cp in fetch(j, r * S, 2):
            cp.wait()
        return acc + one_round(r, 2)

    moe = lax.fori_loop(1, rounds_ref[j], extra, moe)
    o_ref[...] = _layer_norm(alpha * x1_ref[...] + moe, g_ref[...], b_ref[...])


def _route_scatter(sel, x1, ye, off, rounds, ln_g, ln_b, alpha):
    E, n = sel.shape
    D = x1.shape[1]
    Tb, S = ROUTE_BLOCK, ROUTE_SLOTS
    nblk = n // Tb
    return pl.pallas_call(
        functools.partial(_scatter_body, E=E, nblk=nblk, alpha=alpha),
        grid_spec=pltpu.PrefetchScalarGridSpec(
            num_scalar_prefetch=2,
            grid=(nblk,),
            in_specs=[pl.BlockSpec((E, Tb), lambda j, *_: (0, j)),
                      pl.BlockSpec((Tb, D), lambda j, *_: (j, 0)),
                      pl.BlockSpec((1, D), lambda j, *_: (0, 0)),
                      pl.BlockSpec((1, D), lambda j, *_: (0, 0)),
                      pl.BlockSpec(memory_space=pl.ANY)],
            out_specs=pl.BlockSpec((Tb, D), lambda j, *_: (j, 0)),
            scratch_shapes=[pltpu.VMEM((E * S, Tb), bf16),
                            pltpu.VMEM((3, E * S, D), f32),
                            pltpu.SemaphoreType.DMA((3,))]),
        out_shape=jax.ShapeDtypeStruct((n, D), f32),
        compiler_params=_cparams("arbitrary"),
        name="route_scatter",
    )(off, rounds, sel, x1, ln_g, ln_b, ye)


def _expert_choice(x1, aff, w1, w3, w2, ln_g, ln_b, alpha):
    n, D = x1.shape
    E = aff.shape[1]
    cap = EC_CAPACITY * n // E
    Tb, S = ROUTE_BLOCK, ROUTE_SLOTS
    nblk = n // Tb
    sel = _select(aff.T, cap)
    cnt = sel.reshape(E, nblk, Tb).sum(-1).astype(jnp.int32)
    cnt_al = (cnt + ROW_ALIGN - 1) // ROW_ALIGN * ROW_ALIGN
    end = jnp.cumsum(cnt_al, axis=1)
    off = (end - cnt_al).reshape(E * nblk)
    total = end[:, -1]
    rounds = jnp.maximum((jnp.max(cnt, axis=0) + S - 1) // S, 1)
    rows_pad = -(-(cap + ROW_ALIGN * nblk + S) // ROW_TILE) * ROW_TILE
    xe = _route_gather(sel, x1, aff, off, rounds, total, rows_pad)
    ye = _ffn(total, xe, w1, w3, w2)
    return _route_scatter(sel, x1, ye, off, rounds, ln_g, ln_b, alpha)


FILT_FEAT_PAD = 32


def _lag_features(L):
    n = np.arange(2 * L, dtype=np.float64)
    pos = np.where(n < L, n, 2 * L - n)
    t = pos / (L - 1)
    bands = np.linspace(1e-4, FILT_BANDS - 1, FILT_BANDS)
    ang = 2.0 * math.pi * (pos / L)[:, None] * bands[None, :]
    z = np.concatenate([t[:, None], np.cos(ang), -np.sin(ang)], axis=-1)
    out = np.zeros((2 * L, FILT_FEAT_PAD), np.float32)
    out[:, :z.shape[1]] = z
    return out


def _filter_body(z_ref, w1_ref, b1_ref, f1_ref, w2_ref, b2_ref, f2_ref, w3_ref, dec_ref, o_ref, *, L):
    hp = lax.Precision.HIGHEST
    z = z_ref[...]
    h = jnp.sin(f1_ref[...] * (jnp.dot(z, w1_ref[...], precision=hp, preferred_element_type=f32) + b1_ref[...]))
    h = jnp.sin(f2_ref[...] * (jnp.dot(h, w2_ref[...], precision=hp, preferred_element_type=f32) + b2_ref[...]))
    h = jnp.dot(h, w3_ref[...], precision=hp, preferred_element_type=f32)
    window = jnp.exp(-z[:, 0:1] * jnp.abs(dec_ref[...])) + FILT_SHIFT
    tl = z.shape[0]
    row = pl.program_id(0) * tl + lax.broadcasted_iota(jnp.int32, h.shape, 0)
    o_ref[...] = jnp.where(row == L, 0.0, h * window)


def _circ_filters(L, w1, b1, f1, w2, b2, f2, w3, decay):
    nh = w1.shape[1]
    oc = w3.shape[1] // 2
    feats = jnp.asarray(_lag_features(L))
    w1p = jnp.zeros((FILT_FEAT_PAD, nh), f32).at[:w1.shape[0]].set(w1)
    tl = min(ROW_TILE, L)
    half = L // tl
    dirsel = lambda i: (i // half, 0, 0)
    return pl.pallas_call(
        functools.partial(_filter_body, L=L),
        grid=(2 * L // tl,),
        in_specs=[pl.BlockSpec((tl, FILT_FEAT_PAD), lambda i: (i, 0)),
                  _const_spec((FILT_FEAT_PAD, nh)), _const_spec((1, nh)), _const_spec((1, nh)),
                  _const_spec((nh, nh)), _const_spec((1, nh)), _const_spec((1, nh)),
                  pl.BlockSpec((None, nh, oc), dirsel),
                  pl.BlockSpec((None, 1, oc), dirsel)],
        out_specs=pl.BlockSpec((tl, oc), lambda i: (i, 0)),
        out_shape=jax.ShapeDtypeStruct((2 * L, oc), f32),
        compiler_params=_cparams("parallel"),
        name="hyena_filter",
    )(feats, w1p, b1.reshape(1, nh), f1.reshape(1, nh), w2, b2.reshape(1, nh), f2.reshape(1, nh),
      w3.reshape(nh, 2, oc).transpose(1, 0, 2), decay.reshape(2, 1, oc))


DFT_N2 = 128
FREQ_CHAINS = 8
COL_TILE = 4096


def _dft_tables(L):
    N, N2 = 2 * L, DFT_N2
    N1 = N // N2
    H1 = N1 // 2
    K1n = H1 + 1
    K1p = -(-K1n // 8) * 8
    k1 = np.arange(K1n)
    n1 = np.arange(N1)
    ang = 2.0 * math.pi * ((k1[:, None] * n1[None, :]) % N1) / N1
    f1 = np.zeros((2 * K1p, N1))
    f1[:K1n] = np.cos(ang)
    f1[K1p:K1p + K1n] = -np.sin(ang)
    wgt = np.full(K1n, 2.0)
    wgt[0] = wgt[H1] = 1.0
    ang = 2.0 * math.pi * ((np.arange(H1)[:, None] * k1[None, :]) % N1) / N1
    if1 = np.zeros((H1, 2 * K1p))
    if1[:, :K1n] = wgt * np.cos(ang) / N
    if1[:, K1p:K1p + K1n] = -wgt * np.sin(ang) / N
    k2 = np.arange(N2)
    ang = 2.0 * math.pi * ((k2[:, None] * k2[None, :]) % N2) / N2
    f2r, f2i = np.cos(ang), -np.sin(ang)
    ang = 2.0 * math.pi * (k1[:, None] * k2[None, :]) / N
    twr = np.zeros((K1p, 1, N2))
    twi = np.zeros((K1p, 1, N2))
    twr[:K1n, 0], twi[:K1n, 0] = np.cos(ang), -np.sin(ang)
    c = lambda a, dt: jnp.asarray(a.astype(np.float32)).astype(dt)
    return dict(N1=N1, H1=H1, K1p=K1p, f1=c(f1, bf16), if1=c(if1, bf16), f2r=c(f2r, f32), f2i=c(f2i, f32),
                twr=c(twr, f32), twi=c(twi, f32))


def _lmat_body(w_ref, x_ref, o_ref):
    o_ref[0] = jnp.dot(w_ref[...], x_ref[0].astype(bf16), preferred_element_type=f32).astype(o_ref.dtype)


def _dft_stage1(w, x):
    Bx, Kd, cols = x.shape
    Mo = w.shape[0]
    tc = min(COL_TILE, cols)
    return pl.pallas_call(
        _lmat_body,
        grid=(Bx, cols // tc),
        in_specs=[_const_spec((Mo, Kd)), pl.BlockSpec((1, Kd, tc), lambda b, j: (b, 0, j))],
        out_specs=pl.BlockSpec((1, Mo, tc), lambda b, j: (b, 0, j)),
        out_shape=jax.ShapeDtypeStruct((Bx, Mo, cols), bf16),
        compiler_params=_cparams("parallel", "parallel"),
        name="dft_stage1",
    )(w, x)


def _build_stage2_matrix(f2r_ref, f2i_ref, twr_ref, twi_ref, m_ref, mt_ref):
    n2 = f2r_ref.shape[0]
    twr, twi = twr_ref[...], twi_ref[...]
    re = f2r_ref[...] * twr - f2i_ref[...] * twi
    im = f2r_ref[...] * twi + f2i_ref[...] * twr
    m_ref[:n2, :n2] = re.astype(bf16)
    m_ref[:n2, n2:] = (-im).astype(bf16)
    m_ref[n2:, :n2] = im.astype(bf16)
    m_ref[n2:, n2:] = re.astype(bf16)
    if mt_ref is not None:
        ret, imt = re.T, im.T
        mt_ref[:n2, :n2] = ret.astype(bf16)
        mt_ref[:n2, n2:] = imt.astype(bf16)
        mt_ref[n2:, :n2] = (-imt).astype(bf16)
        mt_ref[n2:, n2:] = ret.astype(bf16)


def _spectrum_body(f2r_ref, f2i_ref, twr_ref, twi_ref, a_ref, x_ref, m_ref):
    _build_stage2_matrix(f2r_ref, f2i_ref, twr_ref, twi_ref, m_ref, None)
    n2, cw = a_ref.shape[1:]
    x = jnp.dot(m_ref[...], a_ref[...].reshape(2 * n2, cw), preferred_element_type=f32)
    x_ref[...] = x.reshape(2, n2, cw)


def _filter_spectrum(tab, a5):
    _, _, K1p, N2, Cw = a5.shape
    tw_spec = pl.BlockSpec((None, 1, N2), lambda k: (k, 0, 0))
    return pl.pallas_call(
        _spectrum_body,
        grid=(K1p,),
        in_specs=[_const_spec((N2, N2)), _const_spec((N2, N2)), tw_spec, tw_spec,
                  pl.BlockSpec((None, 2, None, N2, Cw), lambda k: (0, 0, k, 0, 0))],
        out_specs=pl.BlockSpec((None, 2, N2, Cw), lambda k: (k, 0, 0, 0)),
        out_shape=jax.ShapeDtypeStruct((K1p, 2, N2, Cw), f32),
        scratch_shapes=[pltpu.VMEM((2 * N2, 2 * N2), bf16)],
        compiler_params=_cparams("arbitrary"),
        name="filter_spectrum",
    )(tab['f2r'], tab['f2i'], tab['twr'], tab['twi'], a5)


def _freq_body(f2r_ref, f2i_ref, twr_ref, twi_ref, a_ref, h_ref, g_ref, m_ref, mt_ref):
    nb, _, nk, n2, c = a_ref.shape
    for kk in range(nk):
        _build_stage2_matrix(f2r_ref, f2i_ref, twr_ref.at[kk], twi_ref.at[kk], m_ref.at[kk], mt_ref.at[kk])
        hr, hi = h_ref[kk, 0], h_ref[kk, 1]
        for b in range(nb):
            x = jnp.dot(m_ref[kk], a_ref[b, :, kk].reshape(2 * n2, c), preferred_element_type=f32)
            xr, xi = x[:n2], x[n2:]
            y = jnp.concatenate([xr * hr - xi * hi, xr * hi + xi * hr], axis=0).astype(bf16)
            g = jnp.dot(mt_ref[kk], y, preferred_element_type=f32)
            g_ref[b, :, kk] = g.reshape(2, n2, c).astype(bf16)


def _freq_stage(tab, a5, hspec, order):
    B, _, K1p, N2, C = a5.shape
    nb = min(B, FREQ_CHAINS)
    nk = FREQ_CHAINS // nb
    tw_spec = pl.BlockSpec((nk, 1, N2), lambda k, b: (k, 0, 0))
    slab = pl.BlockSpec((nb, 2, nk, N2, C), lambda k, b: (b, 0, k, 0, 0))
    return pl.pallas_call(
        _freq_body,
        grid=(K1p // nk, B // nb),
        in_specs=[_const_spec((N2, N2)), _const_spec((N2, N2)), tw_spec, tw_spec, slab,
                  pl.BlockSpec((nk, 2, N2, C), lambda k, b: (k, 0, 0, order))],
        out_specs=slab,
        out_shape=jax.ShapeDtypeStruct(a5.shape, bf16),
        scratch_shapes=[pltpu.VMEM((nk, 2 * N2, 2 * N2), bf16), pltpu.VMEM((nk, 2 * N2, 2 * N2), bf16)],
        compiler_params=_cparams("parallel", "parallel"),
        name="freq_stage",
    )(tab['f2r'], tab['f2i'], tab['twr'], tab['twi'], a5, hspec)


def _conv_out_body(w_ref, g_ref, z_ref, gate_ref, bias_ref, o_ref):
    y = jnp.dot(w_ref[...], g_ref[0], preferred_element_type=f32)
    o_ref[0] = gate_ref[0] * (y + z_ref[0] * bias_ref[...])


def _conv_out(tab, g2, z1, gate1, bias_t):
    B, H1, cols = z1.shape
    Kd = g2.shape[1]
    tc = min(COL_TILE, cols)
    blk = pl.BlockSpec((1, H1, tc), lambda b, j: (b, 0, j))
    return pl.pallas_call(
        _conv_out_body,
        grid=(B, cols // tc),
        in_specs=[_const_spec((H1, Kd)), pl.BlockSpec((1, Kd, tc), lambda b, j: (b, 0, j)), blk, blk,
                  pl.BlockSpec((1, tc), lambda b, j: (0, j))],
        out_specs=blk,
        out_shape=jax.ShapeDtypeStruct(z1.shape, f32),
        compiler_params=_cparams("parallel", "parallel"),
        name="conv_out",
    )(tab['if1'], g2, z1, gate1, bias_t)


def _long_conv(v, g1, g2, circ, hy_bias, B, L, C):
    tab = _dft_tables(L)
    N2, N1, H1, K1p = DFT_N2, tab['N1'], tab['H1'], tab['K1p']
    Cw = circ.shape[1]
    a5 = _dft_stage1(tab['f1'], circ.reshape(1, N1, N2 * Cw)).reshape(1, 2, K1p, N2, Cw)
    hspec = _filter_spectrum(tab, a5)
    z1 = v.reshape(B, H1, N2 * C)
    gates = (g1.reshape(B, H1, N2 * C), g2.reshape(B, H1, N2 * C))
    f1h = tab['f1'][:, :H1]
    for o in range(HY_ORDER):
        a5 = _dft_stage1(f1h, z1).reshape(B, 2, K1p, N2, C)
        g2d = _freq_stage(tab, a5, hspec, o).reshape(B, 2 * K1p, N2 * C)
        z1 = _conv_out(tab, g2d, z1, gates[o], jnp.tile(hy_bias[o], N2).reshape(1, N2 * C))
    return z1.reshape(B * L, C)


def _encoder_layer(x, p, alpha):
    B, L, D = x.shape
    n = B * L
    x2 = x.reshape(n, D)
    G, P, H = p['s5_b_re'].shape[1:]
    s5w = G * H
    hyw = p['w_hy_proj'].shape[0]

    u_s5, u_hy, g_s, g_h = _inproj(x2, p['w_in'].astype(bf16), B, L, s5w, hyw)

    Tc, R = S5_CHUNK, S5_ROWS
    chained = B == 1
    Bs, Ls = (R, L // R) if chained else (B, L)
    nc = Ls // Tc
    u_g = u_s5.reshape(Bs, nc, Tc, G, H).transpose(3, 1, 0, 2, 4).reshape(G, nc * Bs, Tc * H)
    y_g = _s5(u_g, _s5_matrices(p, Ls), Bs, chained)
    y_s5 = y_g.reshape(G, nc, Bs, Tc, H).transpose(2, 1, 3, 0, 4).reshape(n, s5w)

    v, g1, g2 = _shortconv(u_hy, p['hy_short_w'], p['hy_short_b'], B, L, hyw)
    circ = _circ_filters(L, p['filt_w1'], p['filt_b1'], p['filt_freq1'], p['filt_w2'], p['filt_b2'],
                         p['filt_freq2'], p['filt_w3'], p['filt_decay'])
    z = _long_conv(v, g1, g2, circ, p['hy_bias'], B, L, hyw)

    x1, aff = _merge(x2, y_s5, z, g_s, g_h, p['s5_w_glu'].astype(bf16), p['w_s5_proj'].astype(bf16),
                     p['w_hy_proj'].astype(bf16), p['w_out'].astype(bf16),
                     p['ln1_g'].reshape(1, D), p['ln1_b'].reshape(1, D), p['w_router'], B, L, alpha)

    out = _expert_choice(x1, aff, p['ex_w1_bf16'], p['ex_w3_bf16'], p['ex_w2_bf16'],
                         p['ln2_g'].reshape(1, D), p['ln2_b'].reshape(1, D), alpha)
    return out.reshape(B, L, D)


_PARAM_NAMES = ('w_in', 's5_lambda_re', 's5_lambda_im', 's5_log_dt', 's5_b_re', 's5_b_im',
                's5_c_re', 's5_c_im', 's5_d', 's5_w_glu', 'w_s5_proj',
                'hy_short_w', 'hy_short_b', 'filt_w1', 'filt_b1', 'filt_freq1', 'filt_w2', 'filt_b2',
                'filt_freq2', 'filt_w3', 'filt_decay', 'hy_bias', 'w_hy_proj', 'w_out',
                'ln1_g', 'ln1_b', 'w_router', 'ex_w1', 'ex_w3', 'ex_w2', 'ln2_g', 'ln2_b')


def kernel(x_prompt, x_sample, w_in, s5_lambda_re, s5_lambda_im, s5_log_dt, s5_b_re, s5_b_im, s5_c_re, s5_c_im, s5_d, s5_w_glu, w_s5_proj, hy_short_w, hy_short_b, filt_w1, filt_b1, filt_freq1, filt_w2, filt_b2, filt_freq2, filt_w3, filt_decay, hy_bias, w_hy_proj, w_out, ln1_g, ln1_b, w_router, ex_w1, ex_w3, ex_w2, ln2_g, ln2_b):
    stacked = (w_in, s5_lambda_re, s5_lambda_im, s5_log_dt, s5_b_re, s5_b_im, s5_c_re, s5_c_im, s5_d,
               s5_w_glu, w_s5_proj, hy_short_w, hy_short_b, filt_w1, filt_b1, filt_freq1, filt_w2,
               filt_b2, filt_freq2, filt_w3, filt_decay, hy_bias, w_hy_proj, w_out, ln1_g, ln1_b,
               w_router, ex_w1, ex_w3, ex_w2, ln2_g, ln2_b)
    depth = w_in.shape[0]
    alpha = (2.0 * depth) ** 0.25
    outs = []
    for x in (x_prompt, x_sample):
        for l in range(depth):
            p = {k: v[l] for k, v in zip(_PARAM_NAMES, stacked)}
            for k in ('ex_w1', 'ex_w3', 'ex_w2'):
                p[k + '_bf16'] = p[k].astype(bf16)
            x = _encoder_layer(x, p, alpha)
        outs.append(x)
    return tuple(outs)
```

```python
import functools
import math

import jax
import jax.numpy as jnp
import numpy as np
from jax import lax
from jax.experimental import pallas as pl
from jax.experimental.pallas import tpu as pltpu

FILT_BANDS = 8
FILT_SHIFT = 0.05
EC_CAPACITY = 2
LN_EPS = 1e-5
HY_ORDER = 2

VMEM_LIMIT_BYTES = 56 * 1024 * 1024
ROW_TILE = 512

bf16 = jnp.bfloat16
f32 = jnp.float32


def _cparams(*sem):
    return pltpu.CompilerParams(dimension_semantics=sem, vmem_limit_bytes=VMEM_LIMIT_BYTES)


def _const_spec(shape):
    return pl.BlockSpec(shape, lambda *_: (0,) * len(shape))


def _inproj_body(x_ref, w_ref, us_ref, uh_ref, gs_ref, gh_ref, *, o1, o2, o3):
    proj = jnp.dot(x_ref[...].astype(bf16), w_ref[...], preferred_element_type=f32)
    us_ref[...] = proj[:, :o1].astype(us_ref.dtype)
    uh_ref[...] = proj[:, o1:o2]
    gs_ref[...] = jax.nn.sigmoid(proj[:, o2:o3])
    gh_ref[...] = jax.nn.sigmoid(proj[:, o3:])


def _inproj(x2, w_in, B, L, s5w, hyw):
    n, D = x2.shape
    cols = w_in.shape[1]
    o1, o2 = s5w, s5w + 3 * hyw
    o3 = o2 + D
    tm = min(ROW_TILE, L)
    nt = L // tm
    return pl.pallas_call(
        functools.partial(_inproj_body, o1=o1, o2=o2, o3=o3),
        grid=(B, nt),
        in_specs=[pl.BlockSpec((tm, D), lambda b, i: (b * nt + i, 0)),
                  _const_spec((D, cols))],
        out_specs=[pl.BlockSpec((tm, s5w), lambda b, i: (b * nt + i, 0)),
                   pl.BlockSpec((tm, 3 * hyw), lambda b, i: (b * nt + i, 0)),
                   pl.BlockSpec((tm, D), lambda b, i: (b * nt + i, 0)),
                   pl.BlockSpec((tm, D), lambda b, i: (b * nt + i, 0))],
        out_shape=[jax.ShapeDtypeStruct((n, s5w), f32),
                   jax.ShapeDtypeStruct((n, 3 * hyw), f32),
                   jax.ShapeDtypeStruct((n, D), f32),
                   jax.ShapeDtypeStruct((n, D), f32)],
        compiler_params=_cparams("parallel", "parallel"),
        name="inproj",
    )(x2, w_in)


S5_CHUNK = 16
S5_ROWS = 8
RELAYOUT_ROWS = 2048


def _s5_body(u_ref, kmat_ref, winf_ref, winb_ref, woutf_ref, woutb_ref, af_ref, ab_ref, y_ref,
             vf_ref, vb_ref, sf_ref, sb_ref, *, R, nc, chained):
    u = u_ref[0]
    hw = sf_ref.shape[1]
    for v_ref, w_ref in ((vf_ref, winf_ref), (vb_ref, winb_ref)):
        v = jnp.dot(u, w_ref[0], preferred_element_type=f32)
        v_ref[0] = v[:, :hw]
        v_ref[1] = v[:, hw:]
    bc = lambda ref, row, lo: jnp.broadcast_to(ref[0, row:row + 1, lo:lo + hw], (R, hw))
    decay = lambda ref, row: ((bc(ref, row, 0), bc(ref, row, hw)), (bc(ref, row + 1, 0), bc(ref, row + 1, hw)))
    f_a1, f_a2 = decay(af_ref, 0)
    b_a1, b_a2 = decay(ab_ref, 0)

    def mul_add(s0, s1, a1, a2, v0, v1):
        return a1[0] * s0 + a2[0] * s1 + v0, a1[1] * s1 + a2[1] * s0 + v1

    def scan(init, store):
        def step(i, carry):
            f0, f1, b0, b1 = carry
            rf = pl.ds(i, R, stride=nc)
            rb = pl.ds(nc - 1 - i, R, stride=nc)
            if store:
                sf_ref[rf, :] = f0
                sb_ref[rb, :] = b0
            f0, f1 = mul_add(f0, f1, f_a1, f_a2, vf_ref[0, rf, :], vf_ref[1, rf, :])
            b0, b1 = mul_add(b0, b1, b_a1, b_a2, vb_ref[0, rb, :], vb_ref[1, rb, :])
            return f0, f1, b0, b1
        return lax.fori_loop(0, nc, step, init, unroll=4)

    zero = jnp.zeros((R, hw), f32)
    init = (zero, zero, zero, zero)
    if chained:
        ef0, ef1, eb0, eb1 = scan(init, store=False)
        fn_a1, fn_a2 = decay(af_ref, 2)
        bn_a1, bn_a2 = decay(ab_ref, 2)
        row = lax.broadcasted_iota(jnp.int32, (R, hw), 0)
        down = lambda v: jnp.where(row == 0, 0.0, pltpu.roll(v, 1, axis=0))
        up = lambda v: jnp.where(row == R - 1, 0.0, pltpu.roll(v, R - 1, axis=0))
        f0, f1, b0, b1 = init
        for _ in range(R - 1):
            t0, t1 = mul_add(f0, f1, fn_a1, fn_a2, ef0, ef1)
            f0, f1 = down(t0), down(t1)
            t0, t1 = mul_add(b0, b1, bn_a1, bn_a2, eb0, eb1)
            b0, b1 = up(t0), up(t1)
        init = (f0, f1, b0, b1)
    scan(init, store=True)
    y = jnp.dot(u, kmat_ref[0], preferred_element_type=f32)
    y = y + jnp.dot(sf_ref[...].astype(bf16), woutf_ref[0], preferred_element_type=f32)
    y = y + jnp.dot(sb_ref[...].astype(bf16), woutb_ref[0], preferred_element_type=f32)
    y_ref[0] = y


def _s5(u_g, mats, R, chained):
    G, M, W = u_g.shape
    kmat, winf, winb, woutf, woutb, af, ab = mats
    hw = woutf.shape[1]
    grp = lambda shape: pl.BlockSpec((1,) + shape, lambda g: (g, 0, 0))
    return pl.pallas_call(
        functools.partial(_s5_body, R=R, nc=M // R, chained=chained),
        grid=(G,),
        in_specs=[grp((M, W)), grp((W, W)), grp((W, 2 * hw)), grp((W, 2 * hw)), grp((hw, W)), grp((hw, W)),
                  grp((4, 2 * hw)), grp((4, 2 * hw))],
        out_specs=grp((M, W)),
        out_shape=jax.ShapeDtypeStruct((G, M, W), f32),
        scratch_shapes=[pltpu.VMEM((2, M, hw), f32), pltpu.VMEM((2, M, hw), f32),
                        pltpu.VMEM((M, hw), f32), pltpu.VMEM((M, hw), f32)],
        compiler_params=_cparams("parallel"),
        name="s5_chunked",
    )(u_g, kmat, winf, winb, woutf, woutb, af, ab)


def _chunk_rows_body(x_ref, o_ref, *, Tc, H):
    nr = x_ref.shape[0] // Tc
    per_tile = x_ref.shape[1] // H
    rows_t = [x_ref[pl.ds(t, nr, stride=Tc), :] for t in range(Tc)]
    for g in range(o_ref.shape[0]):
        for j in range(Tc // per_tile):
            tile = jnp.concatenate([rows_t[per_tile * j + k][:, H * g:H * (g + 1)] for k in range(per_tile)], axis=1)
            o_ref[g, :, j * x_ref.shape[1]:(j + 1) * x_ref.shape[1]] = tile.astype(o_ref.dtype)


def _to_chunk_rows(u, G, H):
    n = u.shape[0]
    Tc, LT = S5_CHUNK, 128
    tm = min(RELAYOUT_ROWS, n)
    gpt = LT // H
    return pl.pallas_call(
        functools.partial(_chunk_rows_body, Tc=Tc, H=H),
        grid=(n // tm, G // gpt),
        in_specs=[pl.BlockSpec((tm, LT), lambda i, j: (i, j))],
        out_specs=pl.BlockSpec((gpt, tm // Tc, Tc * H), lambda i, j: (j, i, 0)),
        out_shape=jax.ShapeDtypeStruct((G, n // Tc, Tc * H), bf16),
        compiler_params=_cparams("parallel", "parallel"),
        name="s5_to_chunks",
    )(u)


def _token_rows_body(y_ref, o_ref, *, Tc, H):
    nr = y_ref.shape[1]
    for t in range(Tc):
        tile = jnp.concatenate([y_ref[g, :, H * t:H * (t + 1)] for g in range(y_ref.shape[0])], axis=1)
        o_ref[pl.ds(t, nr, stride=Tc), :] = tile


def _to_token_rows(y_g, H):
    G, M, W = y_g.shape
    Tc, LT = S5_CHUNK, 128
    n = M * Tc
    tm = min(RELAYOUT_ROWS, n)
    gpt = LT // H
    return pl.pallas_call(
        functools.partial(_token_rows_body, Tc=Tc, H=H),
        grid=(n // tm, G // gpt),
        in_specs=[pl.BlockSpec((gpt, tm // Tc, W), lambda i, j: (j, i, 0))],
        out_specs=pl.BlockSpec((tm, LT), lambda i, j: (i, j)),
        out_shape=jax.ShapeDtypeStruct((n, G * H), f32),
        compiler_params=_cparams("parallel", "parallel"),
        name="s5_to_tokens",
    )(y_g)


def _s5_discretise(lam_re, lam_im, log_dt, b_re, b_im, powers):
    lam = lax.complex(-jnp.abs(lam_re.astype(f32)), lam_im.astype(f32))
    dt = jnp.exp(log_dt.astype(f32))[:, None]
    a_bar = jnp.exp(lam * dt)
    k = jnp.asarray(powers, f32)[None, :, None]
    apow = jnp.exp((lam * dt)[:, None, :] * k)
    b_bar = ((a_bar - 1.0) / lam)[..., None] * lax.complex(b_re.astype(f32), b_im.astype(f32))
    return apow, b_bar


def _s5_matrices(p, seg_steps):
    Tc = S5_CHUNK
    powers = list(range(Tc + 1)) + [seg_steps]
    apf, bbf = _s5_discretise(p['s5_lambda_re'][0], p['s5_lambda_im'][0], p['s5_log_dt'][0],
                              p['s5_b_re'][0], p['s5_b_im'][0], powers)
    apb, bbb = _s5_discretise(p['s5_lambda_re'][1], p['s5_lambda_im'][1], p['s5_log_dt'][1],
                              p['s5_b_re'][1], p['s5_b_im'][1], powers)
    c = lax.complex(p['s5_c_re'].astype(f32), p['s5_c_im'].astype(f32))
    G, H, P = c.shape
    W = Tc * H
    kf = jnp.real(jnp.einsum('ghp,gtp,gpk->gthk', c, apf[:, :Tc], bbf))
    kb = jnp.real(jnp.einsum('ghp,gtp,gpk->gthk', c, apb[:, :Tc], bbb))
    tau = jnp.arange(Tc)[None, :] - jnp.arange(Tc)[:, None]
    blk = (jnp.where((tau >= 0)[None, :, :, None, None], kf[:, jnp.abs(tau)], 0.0)
           + jnp.where((tau <= 0)[None, :, :, None, None], kb[:, jnp.abs(tau)], 0.0))
    kmat = jnp.transpose(blk, (0, 1, 4, 2, 3)).reshape(G, W, W)
    kmat = kmat + jnp.eye(W, dtype=f32)[None] * jnp.tile(p['s5_d'].astype(f32), (1, Tc))[:, None, :]
    lanes = lambda z: jnp.concatenate([jnp.real(z), jnp.imag(z), jnp.imag(z), jnp.real(z)], axis=-1)
    win = lambda ap, bb: lanes(jnp.einsum('gsp,gpk->gskp', ap, bb)).reshape(G, W, 4 * P)
    winf = win(apf[:, Tc - 1::-1][:, :Tc], bbf)
    winb = win(apb[:, :Tc], bbb)
    def wout(ap):
        z = jnp.einsum('ghp,gtp->gpth', c, ap).reshape(G, P, W)
        return jnp.concatenate([jnp.real(z), -jnp.imag(z)], axis=1)
    woutf = wout(apf[:, 1:Tc + 1])
    woutb = wout(apb[:, Tc:0:-1])
    def chunk_decay(ap):
        rows = []
        for k in (Tc, Tc + 1):
            ar, ai = jnp.real(ap[:, k]), jnp.imag(ap[:, k])
            rows += [jnp.concatenate([ar, ar, ar, ar], -1), jnp.concatenate([-ai, ai, ai, -ai], -1)]
        return jnp.stack(rows, axis=1)
    cast = lambda m: m.astype(bf16)
    return (cast(kmat), cast(winf), cast(winb), cast(woutf), cast(woutb), chunk_decay(apf), chunk_decay(apb))


def _shortconv_body(prev_ref, main_ref, next_ref, w_ref, b_ref, v_ref, g1_ref, g2_ref, *, nt, hyw):
    i = pl.program_id(1)
    x = main_ref[...]
    tl = x.shape[0]
    rows = lax.broadcasted_iota(jnp.int32, x.shape, 0)
    prev_row = jnp.where(i == 0, 0.0, prev_ref[7:8, :])
    next_row = jnp.where(i == nt - 1, 0.0, next_ref[0:1, :])
    xm1 = jnp.where(rows == 0, prev_row, pltpu.roll(x, 1, axis=0))
    xp1 = jnp.where(rows == tl - 1, next_row, pltpu.roll(x, tl - 1, axis=0))
    w = w_ref[...]
    hy = xm1 * w[0:1, :] + x * w[1:2, :] + xp1 * w[2:3, :] + b_ref[...]
    v_ref[...] = hy[:, :hyw]
    g1_ref[...] = hy[:, hyw:2 * hyw]
    g2_ref[...] = hy[:, 2 * hyw:]


def _shortconv(u_hy, w, b, B, L, hyw):
    n, C = u_hy.shape
    tl = min(ROW_TILE, L)
    nt = L // tl
    r8 = tl // 8
    nblk8 = n // 8
    main = lambda bb, i: (bb * nt + i, 0)
    prev = lambda bb, i: (jnp.maximum((bb * nt + i) * r8 - 1, 0), 0)
    nxt = lambda bb, i: (jnp.minimum((bb * nt + i + 1) * r8, nblk8 - 1), 0)
    out = jax.ShapeDtypeStruct((n, hyw), f32)
    return pl.pallas_call(
        functools.partial(_shortconv_body, nt=nt, hyw=hyw),
        grid=(B, nt),
        in_specs=[pl.BlockSpec((8, C), prev),
                  pl.BlockSpec((tl, C), main),
                  pl.BlockSpec((8, C), nxt),
                  _const_spec((3, C)),
                  _const_spec((1, C))],
        out_specs=[pl.BlockSpec((tl, hyw), main)] * 3,
        out_shape=[out, out, out],
        compiler_params=_cparams("parallel", "parallel"),
        name="shortconv",
    )(u_hy, u_hy, u_hy, w, b.reshape(1, C))


def _layer_norm(v, g, b):
    mu = jnp.mean(v, axis=-1, keepdims=True)
    c = v - mu
    var = jnp.mean(c * c, axis=-1, keepdims=True)
    return c * lax.rsqrt(var + LN_EPS) * g + b


def _merge_body(x_ref, ys_ref, z_ref, gs_ref, gh_ref, wglu_ref, wsp_ref, whp_ref, wout_ref,
                g_ref, b_ref, wrh_ref, wrl_ref, x1_ref, aff_ref, *, alpha):
    ys = jax.nn.gelu(ys_ref[...])
    gate = jax.nn.sigmoid(jnp.dot(ys.astype(bf16), wglu_ref[...], preferred_element_type=f32))
    branch_s = jnp.dot((ys * gate).astype(bf16), wsp_ref[...], preferred_element_type=f32)
    branch_h = jnp.dot(z_ref[...].astype(bf16), whp_ref[...], preferred_element_type=f32)
    mix = gs_ref[...] * branch_s + gh_ref[...] * branch_h
    mix = jnp.dot(mix.astype(bf16), wout_ref[...], preferred_element_type=f32)
    x1 = _layer_norm(alpha * x_ref[...] + mix, g_ref[...], b_ref[...])
    x1_ref[...] = x1
    x1_hi = x1.astype(bf16)
    x1_lo = (x1 - x1_hi.astype(f32)).astype(bf16)
    logits = (jnp.dot(x1_hi, wrh_ref[...], preferred_element_type=f32)
              + jnp.dot(x1_lo, wrh_ref[...], preferred_element_type=f32)
              + jnp.dot(x1_hi, wrl_ref[...], preferred_element_type=f32))
    m = jnp.max(logits, axis=-1, keepdims=True)
    e = jnp.exp(logits - m)
    aff_ref[...] = e / jnp.sum(e, axis=-1, keepdims=True)


def _merge(x2, ys, z, g_s, g_h, w_glu, w_sp, w_hp, w_out, ln_g, ln_b, w_router, B, L, alpha):
    n, D = x2.shape
    s5w, hyw = ys.shape[1], z.shape[1]
    E = w_router.shape[1]
    tm = min(ROW_TILE, L)
    nt = L // tm
    row = lambda b, i: (b * nt + i, 0)
    wr_hi = w_router.astype(bf16)
    wr_lo = (w_router - wr_hi.astype(f32)).astype(bf16)
    return pl.pallas_call(
        functools.partial(_merge_body, alpha=alpha),
        grid=(B, nt),
        in_specs=[pl.BlockSpec((tm, D), row),
                  pl.BlockSpec((tm, s5w), row),
                  pl.BlockSpec((tm, hyw), row),
                  pl.BlockSpec((tm, D), row),
                  pl.BlockSpec((tm, D), row),
                  _const_spec(w_glu.shape), _const_spec(w_sp.shape), _const_spec(w_hp.shape),
                  _const_spec(w_out.shape), _const_spec((1, D)), _const_spec((1, D)),
                  _const_spec(w_router.shape), _const_spec(w_router.shape)],
        out_specs=[pl.BlockSpec((tm, D), row), pl.BlockSpec((tm, E), row)],
        out_shape=[jax.ShapeDtypeStruct((n, D), f32), jax.ShapeDtypeStruct((n, E), f32)],
        compiler_params=_cparams("parallel", "parallel"),
        name="merge",
    )(x2, ys, z, g_s, g_h, w_glu, w_sp, w_hp, w_out, ln_g, ln_b, wr_hi, wr_lo)


ROUTE_BLOCK = 256
ROUTE_SLOTS = 64
ROW_ALIGN = 8
GATE_LANES = 128


def _select_body(aff_ref, sel_ref, *, cap, idx_bits):
    bits = pltpu.bitcast(aff_ref[...], jnp.int32)
    E = bits.shape[0]
    count = lambda m: jnp.sum(jnp.where(m, 1.0, 0.0), axis=1, keepdims=True)

    def value_bit(i, prefix):
        cand = prefix | jnp.left_shift(jnp.int32(1), 30 - i)
        return jnp.where(count(bits >= cand) >= cap, cand, prefix)

    thr = lax.fori_loop(0, 31, value_bit, jnp.zeros((E, 1), jnp.int32))
    need = cap - count(bits > thr)
    idx = lax.broadcasted_iota(jnp.int32, bits.shape, 1)
    tie_idx = jnp.where(bits == thr, idx, jnp.int32(2 ** 30))

    def index_bit(i, bound):
        cand = bound | jnp.left_shift(jnp.int32(1), idx_bits - 1 - i)
        return jnp.where(count(tie_idx < cand) <= need, cand, bound)

    bound = lax.fori_loop(0, idx_bits, index_bit, jnp.zeros((E, 1), jnp.int32))
    sel_ref[...] = jnp.where(bits > thr, 1.0, jnp.where(tie_idx < bound, 1.0, 0.0))


def _select(aff_t, cap):
    E, n = aff_t.shape
    return pl.pallas_call(
        functools.partial(_select_body, cap=float(cap), idx_bits=int(n).bit_length()),
        out_shape=jax.ShapeDtypeStruct((E, n), f32),
        compiler_params=pltpu.CompilerParams(vmem_limit_bytes=VMEM_LIMIT_BYTES),
        name="expert_select",
    )(aff_t)


def _slot_onehot(sel, p_ref, first_slot, E):
    Tb = sel.shape[1]
    S = p_ref.shape[0] // E
    r = lax.broadcasted_iota(jnp.int32, (Tb, Tb), 0)
    c = lax.broadcasted_iota(jnp.int32, (Tb, Tb), 1)
    tri = jnp.where(r <= c, 1.0, 0.0).astype(bf16)
    incl = jnp.dot(sel.astype(bf16), tri, preferred_element_type=f32)
    slot = jnp.where(sel > 0.0, incl - 1.0, -1.0)
    want = (lax.broadcasted_iota(jnp.int32, (S, Tb), 0) + first_slot).astype(f32)
    for e in range(E):
        p_ref[e * S:(e + 1) * S, :] = jnp.where(slot[e:e + 1] == want, 1.0, 0.0).astype(bf16)
    return incl[:, Tb - 1:Tb]


def _gather_copies(stage_ref, xe_hbm, sem, off_ref, j, first_slot, E, S, nblk):
    copies = []
    for e in range(E):
        row = pl.multiple_of(off_ref[e * nblk + j] + first_slot, ROW_ALIGN)
        copies.append(pltpu.make_async_copy(stage_ref.at[e * S:(e + 1) * S], xe_hbm.at[e, pl.ds(row, S)], sem))
    return copies


def _gather_body(off_ref, rounds_ref, total_ref, sel_ref, x_ref, aff_ref, xe_hbm, p_ref, stage_ref, sem,
                 *, E, nblk, cap):
    j = pl.program_id(0)
    S = p_ref.shape[0] // E
    D = x_ref.shape[1]
    x = x_ref[...].astype(bf16)
    aff = aff_ref[...]
    a_hi = aff.astype(bf16)
    rem = aff - a_hi.astype(f32)
    a_mid = rem.astype(bf16)
    a_lo = (rem - a_mid.astype(f32)).astype(bf16)
    aff3 = jnp.concatenate([a_hi, a_mid, a_lo], axis=1)
    own = (lax.broadcasted_iota(jnp.int32, (E * S, 3 * E), 0) // S) == (
        lax.broadcasted_iota(jnp.int32, (E * S, 3 * E), 1) % E)

    def one_round(r, first_step):
        first_slot = r * S
        _slot_onehot(sel_ref[...], p_ref, first_slot, E)
        p = p_ref[...]
        rows = jnp.dot(p, x, preferred_element_type=f32)
        gates = jnp.dot(p, aff3, preferred_element_type=f32)
        gate = jnp.sum(jnp.where(own, gates, 0.0), axis=1, keepdims=True)

        @pl.when(jnp.logical_not(first_step))
        def _():
            for cp in _gather_copies(stage_ref, xe_hbm, sem, off_ref, j, 0, E, S, nblk):
                cp.wait()

        stage_ref[:, :D] = rows
        stage_ref[:, D:] = jnp.broadcast_to(gate, (E * S, GATE_LANES))
        for cp in _gather_copies(stage_ref, xe_hbm, sem, off_ref, j, first_slot, E, S, nblk):
            cp.start()

    one_round(0, j == 0)

    def extra(r, carry):
        one_round(r, False)
        return carry

    lax.fori_loop(1, rounds_ref[j], extra, 0)

    @pl.when(j == nblk - 1)
    def _():
        for cp in _gather_copies(stage_ref, xe_hbm, sem, off_ref, j, 0, E, S, nblk):
            cp.wait()
        rows_pad = xe_hbm.shape[1]
        stage_ref[:S] = jnp.zeros((S, D + GATE_LANES), f32)
        nfill = -(-(rows_pad - cap) // S)

        def fill(e, row):
            return pltpu.make_async_copy(stage_ref.at[:S], xe_hbm.at[e, pl.ds(pl.multiple_of(row, ROW_ALIGN), S)], sem)

        def whole_chunks(e, action):
            def body(k, c):
                row = total_ref[e] + k * S

                @pl.when(row + S <= rows_pad)
                def _():
                    action(fill(e, row))
                return c
            lax.fori_loop(0, nfill, body, 0)

        for e in range(E):
            whole_chunks(e, lambda cp: cp.start())
        for e in range(E):
            whole_chunks(e, lambda cp: cp.wait())
        for e in range(E):
            fill(e, rows_pad - S).start()
        for e in range(E):
            fill(e, rows_pad - S).wait()


def _route_gather(sel, x1, aff, off, rounds, total, rows_pad):
    E, n = sel.shape
    D = x1.shape[1]
    Tb, S = ROUTE_BLOCK, ROUTE_SLOTS
    nblk = n // Tb
    return pl.pallas_call(
        functools.partial(_gather_body, E=E, nblk=nblk, cap=EC_CAPACITY * n // E),
        grid_spec=pltpu.PrefetchScalarGridSpec(
            num_scalar_prefetch=3,
            grid=(nblk,),
            in_specs=[pl.BlockSpec((E, Tb), lambda j, *_: (0, j)),
                      pl.BlockSpec((Tb, D), lambda j, *_: (j, 0)),
                      pl.BlockSpec((Tb, E), lambda j, *_: (j, 0))],
            out_specs=pl.BlockSpec(memory_space=pl.ANY),
            scratch_shapes=[pltpu.VMEM((E * S, Tb), bf16),
                            pltpu.VMEM((E * S, D + GATE_LANES), f32),
                            pltpu.SemaphoreType.DMA(())]),
        out_shape=jax.ShapeDtypeStruct((E, rows_pad, D + GATE_LANES), f32),
        compiler_params=_cparams("arbitrary"),
        name="route_gather",
    )(off, rounds, total, sel, x1, aff)


def _ffn_body(total_ref, xe_ref, w1_ref, w3_ref, w2_ref, ye_ref):
    e, r = pl.program_id(0), pl.program_id(1)
    tr = xe_ref.shape[1]
    D = ye_ref.shape[2]

    @pl.when(r * tr < total_ref[e])
    def _():
        xe = xe_ref[0, :, :D].astype(bf16)
        gate = xe_ref[0, :, D:D + 1]
        h1 = jnp.dot(xe, w1_ref[0], preferred_element_type=f32)
        h3 = jnp.dot(xe, w3_ref[0], preferred_element_type=f32)
        h = (jax.nn.silu(h1) * h3).astype(bf16)
        ye = jnp.dot(h, w2_ref[0], preferred_element_type=f32) * gate
        row = r * tr + lax.broadcasted_iota(jnp.int32, ye.shape, 0)
        ye_ref[0] = jnp.where(row < total_ref[e], ye, 0.0)

    @pl.when(r * tr >= total_ref[e])
    def _():
        ye_ref[...] = jnp.zeros_like(ye_ref)


def _ffn(total, xe, w1, w3, w2):
    E, rows_pad, Dx = xe.shape
    D, F = w1.shape[1:]
    tr = ROW_TILE
    last = lambda e, tot: (tot[e] - 1) // tr
    rowmap = lambda e, r, tot: (e, jnp.minimum(r, last(e, tot)), 0)
    wmap = lambda e, r, tot: (e, 0, 0)
    return pl.pallas_call(
        _ffn_body,
        grid_spec=pltpu.PrefetchScalarGridSpec(
            num_scalar_prefetch=1,
            grid=(E, rows_pad // tr),
            in_specs=[pl.BlockSpec((1, tr, Dx), rowmap),
                      pl.BlockSpec((1, D, F), wmap), pl.BlockSpec((1, D, F), wmap),
                      pl.BlockSpec((1, F, D), wmap)],
            out_specs=pl.BlockSpec((1, tr, D), lambda e, r, tot: (e, r, 0))),
        out_shape=jax.ShapeDtypeStruct((E, rows_pad, D), f32),
        compiler_params=_cparams("arbitrary", "arbitrary"),
        name="expert_ffn",
    )(total, xe, w1, w3, w2)


def _scatter_copies(ye_hbm, buf_ref, sem, off_ref, j, first_slot, E, S, nblk):
    copies = []
    for e in range(E):
        row = pl.multiple_of(off_ref[e * nblk + j] + first_slot, ROW_ALIGN)
        copies.append(pltpu.make_async_copy(ye_hbm.at[e, pl.ds(row, S)], buf_ref.at[e * S:(e + 1) * S], sem))
    return copies


def _scatter_body(off_ref, rounds_ref, sel_ref, x1_ref, g_ref, b_ref, ye_hbm, o_ref, p_ref, buf_ref, sem,
                  *, E, nblk, alpha):
    j = pl.program_id(0)
    S = p_ref.shape[0] // E
    slot = lax.rem(j, 2)

    def fetch(jj, first_slot, s):
        return _scatter_copies(ye_hbm, buf_ref.at[s], sem.at[s], off_ref, jj, first_slot, E, S, nblk)

    @pl.when(j == 0)
    def _():
        for cp in fetch(j, 0, 0):
            cp.start()

    @pl.when(j + 1 < nblk)
    def _():
        for cp in fetch(j + 1, 0, 1 - slot):
            cp.start()

    def one_round(r, s):
        count = _slot_onehot(sel_ref[...], p_ref, r * S, E)
        left = count - (r * S).astype(f32)
        srow = lax.broadcasted_iota(jnp.int32, (S, 1), 0).astype(f32)
        ye = buf_ref[s]
        parts = [jnp.where(srow < left[e:e + 1], ye[e * S:(e + 1) * S], 0.0) for e in range(E)]
        ye = jnp.concatenate(parts, axis=0).astype(bf16)
        return lax.dot_general(p_ref[...], ye, (((0,), (0,)), ((), ())), preferred_element_type=f32)

    for cp in fetch(j, 0, slot):
        cp.wait()
    moe = one_round(jnp.int32(0), slot)

    def extra(r, acc):
        for cp in fetch(j, r * S, 2):
            cp.start()
        for cp in fetch(j, r * S, 2):
            cp.wait()
        return acc + one_round(r, 2)

    moe = lax.fori_loop(1, rounds_ref[j], extra, moe)
    o_ref[...] = _layer_norm(alpha * x1_ref[...] + moe, g_ref[...], b_ref[...])


def _route_scatter(sel, x1, ye, off, rounds, ln_g, ln_b, alpha):
    E, n = sel.shape
    D = x1.shape[1]
    Tb, S = ROUTE_BLOCK, ROUTE_SLOTS
    nblk = n // Tb
    return pl.pallas_call(
        functools.partial(_scatter_body, E=E, nblk=nblk, alpha=alpha),
        grid_spec=pltpu.PrefetchScalarGridSpec(
            num_scalar_prefetch=2,
            grid=(nblk,),
            in_specs=[pl.BlockSpec((E, Tb), lambda j, *_: (0, j)),
                      pl.BlockSpec((Tb, D), lambda j, *_: (j, 0)),
                      pl.BlockSpec((1, D), lambda j, *_: (0, 0)),
                      pl.BlockSpec((1, D), lambda j, *_: (0, 0)),
                      pl.BlockSpec(memory_space=pl.ANY)],
            out_specs=pl.BlockSpec((Tb, D), lambda j, *_: (j, 0)),
            scratch_shapes=[pltpu.VMEM((E * S, Tb), bf16),
                            pltpu.VMEM((3, E * S, D), f32),
                            pltpu.SemaphoreType.DMA((3,))]),
        out_shape=jax.ShapeDtypeStruct((n, D), f32),
        compiler_params=_cparams("arbitrary"),
        name="route_scatter",
    )(off, rounds, sel, x1, ln_g, ln_b, ye)


def _expert_choice(x1, aff, w1, w3, w2, ln_g, ln_b, alpha):
    n, D = x1.shape
    E = aff.shape[1]
    cap = EC_CAPACITY * n // E
    Tb, S = ROUTE_BLOCK, ROUTE_SLOTS
    nblk = n // Tb
    sel = _select(aff.T, cap)
    cnt = sel.reshape(E, nblk, Tb).sum(-1).astype(jnp.int32)
    cnt_al = (cnt + ROW_ALIGN - 1) // ROW_ALIGN * ROW_ALIGN
    end = jnp.cumsum(cnt_al, axis=1)
    off = (end - cnt_al).reshape(E * nblk)
    total = end[:, -1]
    rounds = jnp.maximum((jnp.max(cnt, axis=0) + S - 1) // S, 1)
    rows_pad = -(-(cap + ROW_ALIGN * nblk + S) // ROW_TILE) * ROW_TILE
    xe = _route_gather(sel, x1, aff, off, rounds, total, rows_pad)
    ye = _ffn(total, xe, w1, w3, w2)
    return _route_scatter(sel, x1, ye, off, rounds, ln_g, ln_b, alpha)


FILT_FEAT_PAD = 32


def _lag_features(L):
    n = np.arange(2 * L, dtype=np.float64)
    pos = np.where(n < L, n, 2 * L - n)
    t = pos / (L - 1)
    bands = np.linspace(1e-4, FILT_BANDS - 1, FILT_BANDS)
    ang = 2.0 * math.pi * (pos / L)[:, None] * bands[None, :]
    z = np.concatenate([t[:, None], np.cos(ang), -np.sin(ang)], axis=-1)
    out = np.zeros((2 * L, FILT_FEAT_PAD), np.float32)
    out[:, :z.shape[1]] = z
    return out


def _filter_body(z_ref, w1_ref, b1_ref, f1_ref, w2_ref, b2_ref, f2_ref, w3_ref, dec_ref, o_ref, *, L):
    hp = lax.Precision.HIGHEST
    z = z_ref[...]
    h = jnp.sin(f1_ref[...] * (jnp.dot(z, w1_ref[...], precision=hp, preferred_element_type=f32) + b1_ref[...]))
    h = jnp.sin(f2_ref[...] * (jnp.dot(h, w2_ref[...], precision=hp, preferred_element_type=f32) + b2_ref[...]))
    h = jnp.dot(h, w3_ref[...], precision=hp, preferred_element_type=f32)
    window = jnp.exp(-z[:, 0:1] * jnp.abs(dec_ref[...])) + FILT_SHIFT
    tl = z.shape[0]
    row = pl.program_id(0) * tl + lax.broadcasted_iota(jnp.int32, h.shape, 0)
    o_ref[...] = jnp.where(row == L, 0.0, h * window)


def _circ_filters(L, w1, b1, f1, w2, b2, f2, w3, decay):
    nh = w1.shape[1]
    oc = w3.shape[1] // 2
    feats = jnp.asarray(_lag_features(L))
    w1p = jnp.zeros((FILT_FEAT_PAD, nh), f32).at[:w1.shape[0]].set(w1)
    tl = min(ROW_TILE, L)
    half = L // tl
    dirsel = lambda i: (i // half, 0, 0)
    return pl.pallas_call(
        functools.partial(_filter_body, L=L),
        grid=(2 * L // tl,),
        in_specs=[pl.BlockSpec((tl, FILT_FEAT_PAD), lambda i: (i, 0)),
                  _const_spec((FILT_FEAT_PAD, nh)), _const_spec((1, nh)), _const_spec((1, nh)),
                  _const_spec((nh, nh)), _const_spec((1, nh)), _const_spec((1, nh)),
                  pl.BlockSpec((None, nh, oc), dirsel),
                  pl.BlockSpec((None, 1, oc), dirsel)],
        out_specs=pl.BlockSpec((tl, oc), lambda i: (i, 0)),
        out_shape=jax.ShapeDtypeStruct((2 * L, oc), f32),
        compiler_params=_cparams("parallel"),
        name="hyena_filter",
    )(feats, w1p, b1.reshape(1, nh), f1.reshape(1, nh), w2, b2.reshape(1, nh), f2.reshape(1, nh),
      w3.reshape(nh, 2, oc).transpose(1, 0, 2), decay.reshape(2, 1, oc))


DFT_N2 = 128
FREQ_CHAINS = 8
COL_TILE = 4096


def _dft_tables(L):
    N, N2 = 2 * L, DFT_N2
    N1 = N // N2
    H1 = N1 // 2
    K1n = H1 + 1
    K1p = -(-K1n // 8) * 8
    k1 = np.arange(K1n)
    n1 = np.arange(N1)
    ang = 2.0 * math.pi * ((k1[:, None] * n1[None, :]) % N1) / N1
    f1 = np.zeros((2 * K1p, N1))
    f1[:K1n] = np.cos(ang)
    f1[K1p:K1p + K1n] = -np.sin(ang)
    wgt = np.full(K1n, 2.0)
    wgt[0] = wgt[H1] = 1.0
    ang = 2.0 * math.pi * ((np.arange(H1)[:, None] * k1[None, :]) % N1) / N1
    if1 = np.zeros((H1, 2 * K1p))
    if1[:, :K1n] = wgt * np.cos(ang) / N
    if1[:, K1p:K1p + K1n] = -wgt * np.sin(ang) / N
    k2 = np.arange(N2)
    ang = 2.0 * math.pi * ((k2[:, None] * k2[None, :]) % N2) / N2
    f2r, f2i = np.cos(ang), -np.sin(ang)
    ang = 2.0 * math.pi * (k1[:, None] * k2[None, :]) / N
    twr = np.zeros((K1p, 1, N2))
    twi = np.zeros((K1p, 1, N2))
    twr[:K1n, 0], twi[:K1n, 0] = np.cos(ang), -np.sin(ang)
    c = lambda a, dt: jnp.asarray(a.astype(np.float32)).astype(dt)
    return dict(N1=N1, H1=H1, K1p=K1p, f1=c(f1, bf16), if1=c(if1, bf16), f2r=c(f2r, f32), f2i=c(f2i, f32),
                twr=c(twr, f32), twi=c(twi, f32))


def _lmat_body(w_ref, x_ref, o_ref):
    o_ref[0] = jnp.dot(w_ref[...], x_ref[0].astype(bf16), preferred_element_type=f32).astype(o_ref.dtype)


def _dft_stage1(w, x):
    Bx, Kd, cols = x.shape
    Mo = w.shape[0]
    tc = min(COL_TILE, cols)
    return pl.pallas_call(
        _lmat_body,
        grid=(Bx, cols // tc),
        in_specs=[_const_spec((Mo, Kd)), pl.BlockSpec((1, Kd, tc), lambda b, j: (b, 0, j))],
        out_specs=pl.BlockSpec((1, Mo, tc), lambda b, j: (b, 0, j)),
        out_shape=jax.ShapeDtypeStruct((Bx, Mo, cols), bf16),
        compiler_params=_cparams("parallel", "parallel"),
        name="dft_stage1",
    )(w, x)


def _build_stage2_matrix(f2r_ref, f2i_ref, twr_ref, twi_ref, m_ref, mt_ref):
    n2 = f2r_ref.shape[0]
    twr, twi = twr_ref[...], twi_ref[...]
    re = f2r_ref[...] * twr - f2i_ref[...] * twi
    im = f2r_ref[...] * twi + f2i_ref[...] * twr
    m_ref[:n2, :n2] = re.astype(bf16)
    m_ref[:n2, n2:] = (-im).astype(bf16)
    m_ref[n2:, :n2] = im.astype(bf16)
    m_ref[n2:, n2:] = re.astype(bf16)
    if mt_ref is not None:
        ret, imt = re.T, im.T
        mt_ref[:n2, :n2] = ret.astype(bf16)
        mt_ref[:n2, n2:] = imt.astype(bf16)
        mt_ref[n2:, :n2] = (-imt).astype(bf16)
        mt_ref[n2:, n2:] = ret.astype(bf16)


def _spectrum_body(f2r_ref, f2i_ref, twr_ref, twi_ref, a_ref, x_ref, m_ref):
    _build_stage2_matrix(f2r_ref, f2i_ref, twr_ref, twi_ref, m_ref, None)
    n2, cw = a_ref.shape[1:]
    x = jnp.dot(m_ref[...], a_ref[...].reshape(2 * n2, cw), preferred_element_type=f32)
    x_ref[...] = x.reshape(2, n2, cw)


def _filter_spectrum(tab, a5):
    _, _, K1p, N2, Cw = a5.shape
    tw_spec = pl.BlockSpec((None, 1, N2), lambda k: (k, 0, 0))
    return pl.pallas_call(
        _spectrum_body,
        grid=(K1p,),
        in_specs=[_const_spec((N2, N2)), _const_spec((N2, N2)), tw_spec, tw_spec,
                  pl.BlockSpec((None, 2, None, N2, Cw), lambda k: (0, 0, k, 0, 0))],
        out_specs=pl.BlockSpec((None, 2, N2, Cw), lambda k: (k, 0, 0, 0)),
        out_shape=jax.ShapeDtypeStruct((K1p, 2, N2, Cw), f32),
        scratch_shapes=[pltpu.VMEM((2 * N2, 2 * N2), bf16)],
        compiler_params=_cparams("arbitrary"),
        name="filter_spectrum",
    )(tab['f2r'], tab['f2i'], tab['twr'], tab['twi'], a5)


def _freq_body(f2r_ref, f2i_ref, twr_ref, twi_ref, a_ref, h_ref, g_ref, m_ref, mt_ref):
    nb, _, nk, n2, c = a_ref.shape
    for kk in range(nk):
        _build_stage2_matrix(f2r_ref, f2i_ref, twr_ref.at[kk], twi_ref.at[kk], m_ref.at[kk], mt_ref.at[kk])
        hr, hi = h_ref[kk, 0], h_ref[kk, 1]
        for b in range(nb):
            x = jnp.dot(m_ref[kk], a_ref[b, :, kk].reshape(2 * n2, c), preferred_element_type=f32)
            xr, xi = x[:n2], x[n2:]
            y = jnp.concatenate([xr * hr - xi * hi, xr * hi + xi * hr], axis=0).astype(bf16)
            g = jnp.dot(mt_ref[kk], y, preferred_element_type=f32)
            g_ref[b, :, kk] = g.reshape(2, n2, c).astype(bf16)


def _freq_stage(tab, a5, hspec, order):
    B, _, K1p, N2, C = a5.shape
    nb = min(B, FREQ_CHAINS)
    nk = FREQ_CHAINS // nb
    tw_spec = pl.BlockSpec((nk, 1, N2), lambda k, b: (k, 0, 0))
    slab = pl.BlockSpec((nb, 2, nk, N2, C), lambda k, b: (b, 0, k, 0, 0))
    return pl.pallas_call(
        _freq_body,
        grid=(K1p // nk, B // nb),
        in_specs=[_const_spec((N2, N2)), _const_spec((N2, N2)), tw_spec, tw_spec, slab,
                  pl.BlockSpec((nk, 2, N2, C), lambda k, b: (k, 0, 0, order))],
        out_specs=slab,
        out_shape=jax.ShapeDtypeStruct(a5.shape, bf16),
        scratch_shapes=[pltpu.VMEM((nk, 2 * N2, 2 * N2), bf16), pltpu.VMEM((nk, 2 * N2, 2 * N2), bf16)],
        compiler_params=_cparams("parallel", "parallel"),
        name="freq_stage",
    )(tab['f2r'], tab['f2i'], tab['twr'], tab['twi'], a5, hspec)


def _conv_out_body(w_ref, g_ref, z_ref, gate_ref, bias_ref, o_ref):
    y = jnp.dot(w_ref[...], g_ref[0], preferred_element_type=f32)
    o_ref[0] = gate_ref[0] * (y + z_ref[0] * bias_ref[...])


def _conv_out(tab, g2, z1, gate1, bias_t):
    B, H1, cols = z1.shape
    Kd = g2.shape[1]
    tc = min(COL_TILE, cols)
    blk = pl.BlockSpec((1, H1, tc), lambda b, j: (b, 0, j))
    return pl.pallas_call(
        _conv_out_body,
        grid=(B, cols // tc),
        in_specs=[_const_spec((H1, Kd)), pl.BlockSpec((1, Kd, tc), lambda b, j: (b, 0, j)), blk, blk,
                  pl.BlockSpec((1, tc), lambda b, j: (0, j))],
        out_specs=blk,
        out_shape=jax.ShapeDtypeStruct(z1.shape, f32),
        compiler_params=_cparams("parallel", "parallel"),
        name="conv_out",
    )(tab['if1'], g2, z1, gate1, bias_t)


def _long_conv(v, g1, g2, circ, hy_bias, B, L, C):
    tab = _dft_tables(L)
    N2, N1, H1, K1p = DFT_N2, tab['N1'], tab['H1'], tab['K1p']
    Cw = circ.shape[1]
    a5 = _dft_stage1(tab['f1'], circ.reshape(1, N1, N2 * Cw)).reshape(1, 2, K1p, N2, Cw)
    hspec = _filter_spectrum(tab, a5)
    z1 = v.reshape(B, H1, N2 * C)
    gates = (g1.reshape(B, H1, N2 * C), g2.reshape(B, H1, N2 * C))
    f1h = tab['f1'][:, :H1]
    for o in range(HY_ORDER):
        a5 = _dft_stage1(f1h, z1).reshape(B, 2, K1p, N2, C)
        g2d = _freq_stage(tab, a5, hspec, o).reshape(B, 2 * K1p, N2 * C)
        z1 = _conv_out(tab, g2d, z1, gates[o], jnp.tile(hy_bias[o], N2).reshape(1, N2 * C))
    return z1.reshape(B * L, C)


def _encoder_layer(x, p, alpha):
    B, L, D = x.shape
    n = B * L
    x2 = x.reshape(n, D)
    G, P, H = p['s5_b_re'].shape[1:]
    s5w = G * H
    hyw = p['w_hy_proj'].shape[0]

    u_s5, u_hy, g_s, g_h = _inproj(x2, p['w_in'].astype(bf16), B, L, s5w, hyw)

    chained = B == 1
    Bs, Ls = (S5_ROWS, L // S5_ROWS) if chained else (B, L)
    y_g = _s5(_to_chunk_rows(u_s5, G, H), _s5_matrices(p, Ls), Bs, chained)
    y_s5 = _to_token_rows(y_g, H)

    v, g1, g2 = _shortconv(u_hy, p['hy_short_w'], p['hy_short_b'], B, L, hyw)
    circ = _circ_filters(L, p['filt_w1'], p['filt_b1'], p['filt_freq1'], p['filt_w2'], p['filt_b2'],
                         p['filt_freq2'], p['filt_w3'], p['filt_decay'])
    z = _long_conv(v, g1, g2, circ, p['hy_bias'], B, L, hyw)

    x1, aff = _merge(x2, y_s5, z, g_s, g_h, p['s5_w_glu'].astype(bf16), p['w_s5_proj'].astype(bf16),
                     p['w_hy_proj'].astype(bf16), p['w_out'].astype(bf16),
                     p['ln1_g'].reshape(1, D), p['ln1_b'].reshape(1, D), p['w_router'], B, L, alpha)

    out = _expert_choice(x1, aff, p['ex_w1_bf16'], p['ex_w3_bf16'], p['ex_w2_bf16'],
                         p['ln2_g'].reshape(1, D), p['ln2_b'].reshape(1, D), alpha)
    return out.reshape(B, L, D)


_PARAM_NAMES = ('w_in', 's5_lambda_re', 's5_lambda_im', 's5_log_dt', 's5_b_re', 's5_b_im',
                's5_c_re', 's5_c_im', 's5_d', 's5_w_glu', 'w_s5_proj',
                'hy_short_w', 'hy_short_b', 'filt_w1', 'filt_b1', 'filt_freq1', 'filt_w2', 'filt_b2',
                'filt_freq2', 'filt_w3', 'filt_decay', 'hy_bias', 'w_hy_proj', 'w_out',
                'ln1_g', 'ln1_b', 'w_router', 'ex_w1', 'ex_w3', 'ex_w2', 'ln2_g', 'ln2_b')


def kernel(x_prompt, x_sample, w_in, s5_lambda_re, s5_lambda_im, s5_log_dt, s5_b_re, s5_b_im, s5_c_re, s5_c_im, s5_d, s5_w_glu, w_s5_proj, hy_short_w, hy_short_b, filt_w1, filt_b1, filt_freq1, filt_w2, filt_b2, filt_freq2, filt_w3, filt_decay, hy_bias, w_hy_proj, w_out, ln1_g, ln1_b, w_router, ex_w1, ex_w3, ex_w2, ln2_g, ln2_b):
    stacked = (w_in, s5_lambda_re, s5_lambda_im, s5_log_dt, s5_b_re, s5_b_im, s5_c_re, s5_c_im, s5_d,
               s5_w_glu, w_s5_proj, hy_short_w, hy_short_b, filt_w1, filt_b1, filt_freq1, filt_w2,
               filt_b2, filt_freq2, filt_w3, filt_decay, hy_bias, w_hy_proj, w_out, ln1_g, ln1_b,
               w_router, ex_w1, ex_w3, ex_w2, ln2_g, ln2_b)
    depth = w_in.shape[0]
    alpha = (2.0 * depth) ** 0.25
    outs = []
    for x in (x_prompt, x_sample):
        for l in range(depth):
            p = {k: v[l] for k, v in zip(_PARAM_NAMES, stacked)}
            for k in ('ex_w1', 'ex_w3', 'ex_w2'):
                p[k + '_bf16'] = p[k].astype(bf16)
            x = _encoder_layer(x, p, alpha)
        outs.append(x)
    return tuple(outs)
```

```python
import functools
import math

import jax
import jax.numpy as jnp
import numpy as np
from jax import lax
from jax.experimental import pallas as pl
from jax.experimental.pallas import tpu as pltpu

FILT_BANDS = 8
FILT_SHIFT = 0.05
EC_CAPACITY = 2
LN_EPS = 1e-5
HY_ORDER = 2

VMEM_LIMIT_BYTES = 56 * 1024 * 1024
ROW_TILE = 512

bf16 = jnp.bfloat16
f32 = jnp.float32


def _cparams(*sem):
    return pltpu.CompilerParams(dimension_semantics=sem, vmem_limit_bytes=VMEM_LIMIT_BYTES)


def _const_spec(shape):
    return pl.BlockSpec(shape, lambda *_: (0,) * len(shape))


def _inproj_body(x_ref, w_ref, us_ref, uh_ref, gs_ref, gh_ref, *, o1, o2, o3):
    proj = jnp.dot(x_ref[...].astype(bf16), w_ref[...], preferred_element_type=f32)
    us_ref[...] = proj[:, :o1].astype(us_ref.dtype)
    uh_ref[...] = proj[:, o1:o2]
    gs_ref[...] = jax.nn.sigmoid(proj[:, o2:o3])
    gh_ref[...] = jax.nn.sigmoid(proj[:, o3:])


def _inproj(x2, w_in, B, L, s5w, hyw):
    n, D = x2.shape
    cols = w_in.shape[1]
    o1, o2 = s5w, s5w + 3 * hyw
    o3 = o2 + D
    tm = min(ROW_TILE, L)
    nt = L // tm
    return pl.pallas_call(
        functools.partial(_inproj_body, o1=o1, o2=o2, o3=o3),
        grid=(B, nt),
        in_specs=[pl.BlockSpec((tm, D), lambda b, i: (b * nt + i, 0)),
                  _const_spec((D, cols))],
        out_specs=[pl.BlockSpec((tm, s5w), lambda b, i: (b * nt + i, 0)),
                   pl.BlockSpec((tm, 3 * hyw), lambda b, i: (b * nt + i, 0)),
                   pl.BlockSpec((tm, D), lambda b, i: (b * nt + i, 0)),
                   pl.BlockSpec((tm, D), lambda b, i: (b * nt + i, 0))],
        out_shape=[jax.ShapeDtypeStruct((n, s5w), f32),
                   jax.ShapeDtypeStruct((n, 3 * hyw), f32),
                   jax.ShapeDtypeStruct((n, D), f32),
                   jax.ShapeDtypeStruct((n, D), f32)],
        compiler_params=_cparams("parallel", "parallel"),
        name="inproj",
    )(x2, w_in)


S5_CHUNK = 16
S5_ROWS = 8
RELAYOUT_ROWS = 2048


def _s5_body(u_ref, kmat_ref, winf_ref, winb_ref, woutf_ref, woutb_ref, af_ref, ab_ref, y_ref,
             vf_ref, vb_ref, sf_ref, sb_ref, *, R, nc, chained):
    u = u_ref[0]
    hw = sf_ref.shape[1]
    seq_rows = lambda b: pl.ds(b, nc, stride=R)
    for v_ref, w_ref in ((vf_ref, winf_ref), (vb_ref, winb_ref)):
        v = jnp.dot(u, w_ref[0], preferred_element_type=f32)
        for b in range(R):
            v_ref[0, seq_rows(b), :] = v[b * nc:(b + 1) * nc, :hw]
            v_ref[1, seq_rows(b), :] = v[b * nc:(b + 1) * nc, hw:]
    bc = lambda ref, row, lo: jnp.broadcast_to(ref[0, row:row + 1, lo:lo + hw], (R, hw))
    decay = lambda ref, row: ((bc(ref, row, 0), bc(ref, row, hw)), (bc(ref, row + 1, 0), bc(ref, row + 1, hw)))
    f_a1, f_a2 = decay(af_ref, 0)
    b_a1, b_a2 = decay(ab_ref, 0)

    def mul_add(s0, s1, a1, a2, v0, v1):
        return a1[0] * s0 + a2[0] * s1 + v0, a1[1] * s1 + a2[1] * s0 + v1

    def scan(init, store):
        def step(i, carry):
            f0, f1, b0, b1 = carry
            rf = pl.ds(pl.multiple_of(i * R, R), R)
            rb = pl.ds(pl.multiple_of((nc - 1 - i) * R, R), R)
            if store:
                sf_ref[rf, :] = f0
                sb_ref[rb, :] = b0
            f0, f1 = mul_add(f0, f1, f_a1, f_a2, vf_ref[0, rf, :], vf_ref[1, rf, :])
            b0, b1 = mul_add(b0, b1, b_a1, b_a2, vb_ref[0, rb, :], vb_ref[1, rb, :])
            return f0, f1, b0, b1
        return lax.fori_loop(0, nc, step, init, unroll=4)

    zero = jnp.zeros((R, hw), f32)
    init = (zero, zero, zero, zero)
    if chained:
        ef0, ef1, eb0, eb1 = scan(init, store=False)
        fn_a1, fn_a2 = decay(af_ref, 2)
        bn_a1, bn_a2 = decay(ab_ref, 2)
        row = lax.broadcasted_iota(jnp.int32, (R, hw), 0)
        down = lambda v: jnp.where(row == 0, 0.0, pltpu.roll(v, 1, axis=0))
        up = lambda v: jnp.where(row == R - 1, 0.0, pltpu.roll(v, R - 1, axis=0))
        f0, f1, b0, b1 = init
        for _ in range(R - 1):
            t0, t1 = mul_add(f0, f1, fn_a1, fn_a2, ef0, ef1)
            f0, f1 = down(t0), down(t1)
            t0, t1 = mul_add(b0, b1, bn_a1, bn_a2, eb0, eb1)
            b0, b1 = up(t0), up(t1)
        init = (f0, f1, b0, b1)
    scan(init, store=True)
    y = jnp.dot(u, kmat_ref[0], preferred_element_type=f32)
    for s_ref, w_ref in ((sf_ref, woutf_ref), (sb_ref, woutb_ref)):
        states = jnp.concatenate([s_ref[seq_rows(b), :] for b in range(R)], axis=0)
        y = y + jnp.dot(states.astype(bf16), w_ref[0], preferred_element_type=f32)
    y_ref[0] = y


def _s5(u_g, mats, R, chained):
    G, M, W = u_g.shape
    kmat, winf, winb, woutf, woutb, af, ab = mats
    hw = woutf.shape[1]
    grp = lambda shape: pl.BlockSpec((1,) + shape, lambda g: (g, 0, 0))
    return pl.pallas_call(
        functools.partial(_s5_body, R=R, nc=M // R, chained=chained),
        grid=(G,),
        in_specs=[grp((M, W)), grp((W, W)), grp((W, 2 * hw)), grp((W, 2 * hw)), grp((hw, W)), grp((hw, W)),
                  grp((4, 2 * hw)), grp((4, 2 * hw))],
        out_specs=grp((M, W)),
        out_shape=jax.ShapeDtypeStruct((G, M, W), f32),
        scratch_shapes=[pltpu.VMEM((2, M, hw), f32), pltpu.VMEM((2, M, hw), f32),
                        pltpu.VMEM((M, hw), f32), pltpu.VMEM((M, hw), f32)],
        compiler_params=_cparams("parallel"),
        name="s5_chunked",
    )(u_g, kmat, winf, winb, woutf, woutb, af, ab)


def _chunk_rows_body(x_ref, o_ref, *, Tc, H):
    nr = x_ref.shape[0] // Tc
    per_tile = x_ref.shape[1] // H
    rows_t = [x_ref[pl.ds(t, nr, stride=Tc), :] for t in range(Tc)]
    for g in range(o_ref.shape[0]):
        for j in range(Tc // per_tile):
            tile = jnp.concatenate([rows_t[per_tile * j + k][:, H * g:H * (g + 1)] for k in range(per_tile)], axis=1)
            o_ref[g, :, j * x_ref.shape[1]:(j + 1) * x_ref.shape[1]] = tile.astype(o_ref.dtype)


def _to_chunk_rows(u, G, H):
    n = u.shape[0]
    Tc, LT = S5_CHUNK, 128
    tm = min(RELAYOUT_ROWS, n)
    gpt = LT // H
    return pl.pallas_call(
        functools.partial(_chunk_rows_body, Tc=Tc, H=H),
        grid=(n // tm, G // gpt),
        in_specs=[pl.BlockSpec((tm, LT), lambda i, j: (i, j))],
        out_specs=pl.BlockSpec((gpt, tm // Tc, Tc * H), lambda i, j: (j, i, 0)),
        out_shape=jax.ShapeDtypeStruct((G, n // Tc, Tc * H), bf16),
        compiler_params=_cparams("parallel", "parallel"),
        name="s5_to_chunks",
    )(u)


def _token_rows_body(y_ref, o_ref, *, Tc, H):
    nr = y_ref.shape[1]
    for t in range(Tc):
        tile = jnp.concatenate([y_ref[g, :, H * t:H * (t + 1)] for g in range(y_ref.shape[0])], axis=1)
        o_ref[pl.ds(t, nr, stride=Tc), :] = tile


def _to_token_rows(y_g, H):
    G, M, W = y_g.shape
    Tc, LT = S5_CHUNK, 128
    n = M * Tc
    tm = min(RELAYOUT_ROWS, n)
    gpt = LT // H
    return pl.pallas_call(
        functools.partial(_token_rows_body, Tc=Tc, H=H),
        grid=(n // tm, G // gpt),
        in_specs=[pl.BlockSpec((gpt, tm // Tc, W), lambda i, j: (j, i, 0))],
        out_specs=pl.BlockSpec((tm, LT), lambda i, j: (i, j)),
        out_shape=jax.ShapeDtypeStruct((n, G * H), f32),
        compiler_params=_cparams("parallel", "parallel"),
        name="s5_to_tokens",
    )(y_g)


def _s5_discretise(lam_re, lam_im, log_dt, b_re, b_im, powers):
    lam = lax.complex(-jnp.abs(lam_re.astype(f32)), lam_im.astype(f32))
    dt = jnp.exp(log_dt.astype(f32))[:, None]
    a_bar = jnp.exp(lam * dt)
    k = jnp.asarray(powers, f32)[None, :, None]
    apow = jnp.exp((lam * dt)[:, None, :] * k)
    b_bar = ((a_bar - 1.0) / lam)[..., None] * lax.complex(b_re.astype(f32), b_im.astype(f32))
    return apow, b_bar


def _s5_matrices(p, seg_steps):
    Tc = S5_CHUNK
    powers = list(range(Tc + 1)) + [seg_steps]
    apf, bbf = _s5_discretise(p['s5_lambda_re'][0], p['s5_lambda_im'][0], p['s5_log_dt'][0],
                              p['s5_b_re'][0], p['s5_b_im'][0], powers)
    apb, bbb = _s5_discretise(p['s5_lambda_re'][1], p['s5_lambda_im'][1], p['s5_log_dt'][1],
                              p['s5_b_re'][1], p['s5_b_im'][1], powers)
    c = lax.complex(p['s5_c_re'].astype(f32), p['s5_c_im'].astype(f32))
    G, H, P = c.shape
    W = Tc * H
    kf = jnp.real(jnp.einsum('ghp,gtp,gpk->gthk', c, apf[:, :Tc], bbf))
    kb = jnp.real(jnp.einsum('ghp,gtp,gpk->gthk', c, apb[:, :Tc], bbb))
    tau = jnp.arange(Tc)[None, :] - jnp.arange(Tc)[:, None]
    blk = (jnp.where((tau >= 0)[None, :, :, None, None], kf[:, jnp.abs(tau)], 0.0)
           + jnp.where((tau <= 0)[None, :, :, None, None], kb[:, jnp.abs(tau)], 0.0))
    kmat = jnp.transpose(blk, (0, 1, 4, 2, 3)).reshape(G, W, W)
    kmat = kmat + jnp.eye(W, dtype=f32)[None] * jnp.tile(p['s5_d'].astype(f32), (1, Tc))[:, None, :]
    lanes = lambda z: jnp.concatenate([jnp.real(z), jnp.imag(z), jnp.imag(z), jnp.real(z)], axis=-1)
    win = lambda ap, bb: lanes(jnp.einsum('gsp,gpk->gskp', ap, bb)).reshape(G, W, 4 * P)
    winf = win(apf[:, Tc - 1::-1][:, :Tc], bbf)
    winb = win(apb[:, :Tc], bbb)
    def wout(ap):
        z = jnp.einsum('ghp,gtp->gpth', c, ap).reshape(G, P, W)
        return jnp.concatenate([jnp.real(z), -jnp.imag(z)], axis=1)
    woutf = wout(apf[:, 1:Tc + 1])
    woutb = wout(apb[:, Tc:0:-1])
    def chunk_decay(ap):
        rows = []
        for k in (Tc, Tc + 1):
            ar, ai = jnp.real(ap[:, k]), jnp.imag(ap[:, k])
            rows += [jnp.concatenate([ar, ar, ar, ar], -1), jnp.concatenate([-ai, ai, ai, -ai], -1)]
        return jnp.stack(rows, axis=1)
    cast = lambda m: m.astype(bf16)
    return (cast(kmat), cast(winf), cast(winb), cast(woutf), cast(woutb), chunk_decay(apf), chunk_decay(apb))


def _shortconv_body(prev_ref, main_ref, next_ref, w_ref, b_ref, v_ref, g1_ref, g2_ref, *, nt, hyw):
    i = pl.program_id(1)
    x = main_ref[...]
    tl = x.shape[0]
    rows = lax.broadcasted_iota(jnp.int32, x.shape, 0)
    prev_row = jnp.where(i == 0, 0.0, prev_ref[7:8, :])
    next_row = jnp.where(i == nt - 1, 0.0, next_ref[0:1, :])
    xm1 = jnp.where(rows == 0, prev_row, pltpu.roll(x, 1, axis=0))
    xp1 = jnp.where(rows == tl - 1, next_row, pltpu.roll(x, tl - 1, axis=0))
    w = w_ref[...]
    hy = xm1 * w[0:1, :] + x * w[1:2, :] + xp1 * w[2:3, :] + b_ref[...]
    v_ref[...] = hy[:, :hyw]
    g1_ref[...] = hy[:, hyw:2 * hyw]
    g2_ref[...] = hy[:, 2 * hyw:]


def _shortconv(u_hy, w, b, B, L, hyw):
    n, C = u_hy.shape
    tl = min(ROW_TILE, L)
    nt = L // tl
    r8 = tl // 8
    nblk8 = n // 8
    main = lambda bb, i: (bb * nt + i, 0)
    prev = lambda bb, i: (jnp.maximum((bb * nt + i) * r8 - 1, 0), 0)
    nxt = lambda bb, i: (jnp.minimum((bb * nt + i + 1) * r8, nblk8 - 1), 0)
    out = jax.ShapeDtypeStruct((n, hyw), f32)
    return pl.pallas_call(
        functools.partial(_shortconv_body, nt=nt, hyw=hyw),
        grid=(B, nt),
        in_specs=[pl.BlockSpec((8, C), prev),
                  pl.BlockSpec((tl, C), main),
                  pl.BlockSpec((8, C), nxt),
                  _const_spec((3, C)),
                  _const_spec((1, C))],
        out_specs=[pl.BlockSpec((tl, hyw), main)] * 3,
        out_shape=[out, out, out],
        compiler_params=_cparams("parallel", "parallel"),
        name="shortconv",
    )(u_hy, u_hy, u_hy, w, b.reshape(1, C))


def _layer_norm(v, g, b):
    mu = jnp.mean(v, axis=-1, keepdims=True)
    c = v - mu
    var = jnp.mean(c * c, axis=-1, keepdims=True)
    return c * lax.rsqrt(var + LN_EPS) * g + b


def _merge_body(x_ref, ys_ref, z_ref, gs_ref, gh_ref, wglu_ref, wsp_ref, whp_ref, wout_ref,
                g_ref, b_ref, wrh_ref, wrl_ref, x1_ref, aff_ref, *, alpha):
    ys = jax.nn.gelu(ys_ref[...])
    gate = jax.nn.sigmoid(jnp.dot(ys.astype(bf16), wglu_ref[...], preferred_element_type=f32))
    branch_s = jnp.dot((ys * gate).astype(bf16), wsp_ref[...], preferred_element_type=f32)
    branch_h = jnp.dot(z_ref[...].astype(bf16), whp_ref[...], preferred_element_type=f32)
    mix = gs_ref[...] * branch_s + gh_ref[...] * branch_h
    mix = jnp.dot(mix.astype(bf16), wout_ref[...], preferred_element_type=f32)
    x1 = _layer_norm(alpha * x_ref[...] + mix, g_ref[...], b_ref[...])
    x1_ref[...] = x1
    x1_hi = x1.astype(bf16)
    x1_lo = (x1 - x1_hi.astype(f32)).astype(bf16)
    logits = (jnp.dot(x1_hi, wrh_ref[...], preferred_element_type=f32)
              + jnp.dot(x1_lo, wrh_ref[...], preferred_element_type=f32)
              + jnp.dot(x1_hi, wrl_ref[...], preferred_element_type=f32))
    m = jnp.max(logits, axis=-1, keepdims=True)
    e = jnp.exp(logits - m)
    aff_ref[...] = e / jnp.sum(e, axis=-1, keepdims=True)


def _merge(x2, ys, z, g_s, g_h, w_glu, w_sp, w_hp, w_out, ln_g, ln_b, w_router, B, L, alpha):
    n, D = x2.shape
    s5w, hyw = ys.shape[1], z.shape[1]
    E = w_router.shape[1]
    tm = min(ROW_TILE, L)
    nt = L // tm
    row = lambda b, i: (b * nt + i, 0)
    wr_hi = w_router.astype(bf16)
    wr_lo = (w_router - wr_hi.astype(f32)).astype(bf16)
    return pl.pallas_call(
        functools.partial(_merge_body, alpha=alpha),
        grid=(B, nt),
        in_specs=[pl.BlockSpec((tm, D), row),
                  pl.BlockSpec((tm, s5w), row),
                  pl.BlockSpec((tm, hyw), row),
                  pl.BlockSpec((tm, D), row),
                  pl.BlockSpec((tm, D), row),
                  _const_spec(w_glu.shape), _const_spec(w_sp.shape), _const_spec(w_hp.shape),
                  _const_spec(w_out.shape), _const_spec((1, D)), _const_spec((1, D)),
                  _const_spec(w_router.shape), _const_spec(w_router.shape)],
        out_specs=[pl.BlockSpec((tm, D), row), pl.BlockSpec((tm, E), row)],
        out_shape=[jax.ShapeDtypeStruct((n, D), f32), jax.ShapeDtypeStruct((n, E), f32)],
        compiler_params=_cparams("parallel", "parallel"),
        name="merge",
    )(x2, ys, z, g_s, g_h, w_glu, w_sp, w_hp, w_out, ln_g, ln_b, wr_hi, wr_lo)


ROUTE_BLOCK = 256
ROUTE_SLOTS = 64
ROW_ALIGN = 8
GATE_LANES = 128


def _select_body(aff_ref, sel_ref, *, cap, idx_bits):
    bits = pltpu.bitcast(aff_ref[...], jnp.int32)
    E = bits.shape[0]
    count = lambda m: jnp.sum(jnp.where(m, 1.0, 0.0), axis=1, keepdims=True)

    def value_bit(i, prefix):
        cand = prefix | jnp.left_shift(jnp.int32(1), 30 - i)
        return jnp.where(count(bits >= cand) >= cap, cand, prefix)

    thr = lax.fori_loop(0, 31, value_bit, jnp.zeros((E, 1), jnp.int32))
    need = cap - count(bits > thr)
    idx = lax.broadcasted_iota(jnp.int32, bits.shape, 1)
    tie_idx = jnp.where(bits == thr, idx, jnp.int32(2 ** 30))

    def index_bit(i, bound):
        cand = bound | jnp.left_shift(jnp.int32(1), idx_bits - 1 - i)
        return jnp.where(count(tie_idx < cand) <= need, cand, bound)

    bound = lax.fori_loop(0, idx_bits, index_bit, jnp.zeros((E, 1), jnp.int32))
    sel_ref[...] = jnp.where(bits > thr, 1.0, jnp.where(tie_idx < bound, 1.0, 0.0))


def _select(aff_t, cap):
    E, n = aff_t.shape
    return pl.pallas_call(
        functools.partial(_select_body, cap=float(cap), idx_bits=int(n).bit_length()),
        out_shape=jax.ShapeDtypeStruct((E, n), f32),
        compiler_params=pltpu.CompilerParams(vmem_limit_bytes=VMEM_LIMIT_BYTES),
        name="expert_select",
    )(aff_t)


def _slot_onehot(sel, p_ref, first_slot, E):
    Tb = sel.shape[1]
    S = p_ref.shape[0] // E
    r = lax.broadcasted_iota(jnp.int32, (Tb, Tb), 0)
    c = lax.broadcasted_iota(jnp.int32, (Tb, Tb), 1)
    tri = jnp.where(r <= c, 1.0, 0.0).astype(bf16)
    incl = jnp.dot(sel.astype(bf16), tri, preferred_element_type=f32)
    slot = jnp.where(sel > 0.0, incl - 1.0, -1.0)
    want = (lax.broadcasted_iota(jnp.int32, (S, Tb), 0) + first_slot).astype(f32)
    for e in range(E):
        p_ref[e * S:(e + 1) * S, :] = jnp.where(slot[e:e + 1] == want, 1.0, 0.0).astype(bf16)
    return incl[:, Tb - 1:Tb]


def _gather_copies(stage_ref, xe_hbm, sem, off_ref, j, first_slot, E, S, nblk):
    copies = []
    for e in range(E):
        row = pl.multiple_of(off_ref[e * nblk + j] + first_slot, ROW_ALIGN)
        copies.append(pltpu.make_async_copy(stage_ref.at[e * S:(e + 1) * S], xe_hbm.at[e, pl.ds(row, S)], sem))
    return copies


def _gather_body(off_ref, rounds_ref, total_ref, sel_ref, x_ref, aff_ref, xe_hbm, p_ref, stage_ref, sem,
                 *, E, nblk, cap):
    j = pl.program_id(0)
    S = p_ref.shape[0] // E
    D = x_ref.shape[1]
    x = x_ref[...].astype(bf16)
    aff = aff_ref[...]
    a_hi = aff.astype(bf16)
    rem = aff - a_hi.astype(f32)
    a_mid = rem.astype(bf16)
    a_lo = (rem - a_mid.astype(f32)).astype(bf16)
    aff3 = jnp.concatenate([a_hi, a_mid, a_lo], axis=1)
    own = (lax.broadcasted_iota(jnp.int32, (E * S, 3 * E), 0) // S) == (
        lax.broadcasted_iota(jnp.int32, (E * S, 3 * E), 1) % E)

    def one_round(r, first_step):
        first_slot = r * S
        _slot_onehot(sel_ref[...], p_ref, first_slot, E)
        p = p_ref[...]
        rows = jnp.dot(p, x, preferred_element_type=f32)
        gates = jnp.dot(p, aff3, preferred_element_type=f32)
        gate = jnp.sum(jnp.where(own, gates, 0.0), axis=1, keepdims=True)

        @pl.when(jnp.logical_not(first_step))
        def _():
            for cp in _gather_copies(stage_ref, xe_hbm, sem, off_ref, j, 0, E, S, nblk):
                cp.wait()

        stage_ref[:, :D] = rows
        stage_ref[:, D:] = jnp.broadcast_to(gate, (E * S, GATE_LANES))
        for cp in _gather_copies(stage_ref, xe_hbm, sem, off_ref, j, first_slot, E, S, nblk):
            cp.start()

    one_round(0, j == 0)

    def extra(r, carry):
        one_round(r, False)
        return carry

    lax.fori_loop(1, rounds_ref[j], extra, 0)

    @pl.when(j == nblk - 1)
    def _():
        for cp in _gather_copies(stage_ref, xe_hbm, sem, off_ref, j, 0, E, S, nblk):
            cp.wait()
        rows_pad = xe_hbm.shape[1]
        stage_ref[:S] = jnp.zeros((S, D + GATE_LANES), f32)
        nfill = -(-(rows_pad - cap) // S)

        def fill(e, row):
            return pltpu.make_async_copy(stage_ref.at[:S], xe_hbm.at[e, pl.ds(pl.multiple_of(row, ROW_ALIGN), S)], sem)

        def whole_chunks(e, action):
            def body(k, c):
                row = total_ref[e] + k * S

                @pl.when(row + S <= rows_pad)
                def _():
                    action(fill(e, row))
                return c
            lax.fori_loop(0, nfill, body, 0)

        for e in range(E):
            whole_chunks(e, lambda cp: cp.start())
        for e in range(E):
            whole_chunks(e, lambda cp: cp.wait())
        for e in range(E):
            fill(e, rows_pad - S).start()
        for e in range(E):
            fill(e, rows_pad - S).wait()


def _route_gather(sel, x1, aff, off, rounds, total, rows_pad):
    E, n = sel.shape
    D = x1.shape[1]
    Tb, S = ROUTE_BLOCK, ROUTE_SLOTS
    nblk = n // Tb
    return pl.pallas_call(
        functools.partial(_gather_body, E=E, nblk=nblk, cap=EC_CAPACITY * n // E),
        grid_spec=pltpu.PrefetchScalarGridSpec(
            num_scalar_prefetch=3,
            grid=(nblk,),
            in_specs=[pl.BlockSpec((E, Tb), lambda j, *_: (0, j)),
                      pl.BlockSpec((Tb, D), lambda j, *_: (j, 0)),
                      pl.BlockSpec((Tb, E), lambda j, *_: (j, 0))],
            out_specs=pl.BlockSpec(memory_space=pl.ANY),
            scratch_shapes=[pltpu.VMEM((E * S, Tb), bf16),
                            pltpu.VMEM((E * S, D + GATE_LANES), f32),
                            pltpu.SemaphoreType.DMA(())]),
        out_shape=jax.ShapeDtypeStruct((E, rows_pad, D + GATE_LANES), f32),
        compiler_params=_cparams("arbitrary"),
        name="route_gather",
    )(off, rounds, total, sel, x1, aff)


def _ffn_body(total_ref, xe_ref, w1_ref, w3_ref, w2_ref, ye_ref):
    e, r = pl.program_id(0), pl.program_id(1)
    tr = xe_ref.shape[1]
    D = ye_ref.shape[2]

    @pl.when(r * tr < total_ref[e])
    def _():
        xe = xe_ref[0, :, :D].astype(bf16)
        gate = xe_ref[0, :, D:D + 1]
        h1 = jnp.dot(xe, w1_ref[0], preferred_element_type=f32)
        h3 = jnp.dot(xe, w3_ref[0], preferred_element_type=f32)
        h = (jax.nn.silu(h1) * h3).astype(bf16)
        ye = jnp.dot(h, w2_ref[0], preferred_element_type=f32) * gate
        row = r * tr + lax.broadcasted_iota(jnp.int32, ye.shape, 0)
        ye_ref[0] = jnp.where(row < total_ref[e], ye, 0.0)

    @pl.when(r * tr >= total_ref[e])
    def _():
        ye_ref[...] = jnp.zeros_like(ye_ref)


def _ffn(total, xe, w1, w3, w2):
    E, rows_pad, Dx = xe.shape
    D, F = w1.shape[1:]
    tr = ROW_TILE
    last = lambda e, tot: (tot[e] - 1) // tr
    rowmap = lambda e, r, tot: (e, jnp.minimum(r, last(e, tot)), 0)
    wmap = lambda e, r, tot: (e, 0, 0)
    return pl.pallas_call(
        _ffn_body,
        grid_spec=pltpu.PrefetchScalarGridSpec(
            num_scalar_prefetch=1,
            grid=(E, rows_pad // tr),
            in_specs=[pl.BlockSpec((1, tr, Dx), rowmap),
                      pl.BlockSpec((1, D, F), wmap), pl.BlockSpec((1, D, F), wmap),
                      pl.BlockSpec((1, F, D), wmap)],
            out_specs=pl.BlockSpec((1, tr, D), lambda e, r, tot: (e, r, 0))),
        out_shape=jax.ShapeDtypeStruct((E, rows_pad, D), f32),
        compiler_params=_cparams("arbitrary", "arbitrary"),
        name="expert_ffn",
    )(total, xe, w1, w3, w2)


def _scatter_copies(ye_hbm, buf_ref, sem, off_ref, j, first_slot, E, S, nblk):
    copies = []
    for e in range(E):
        row = pl.multiple_of(off_ref[e * nblk + j] + first_slot, ROW_ALIGN)
        copies.append(pltpu.make_async_copy(ye_hbm.at[e, pl.ds(row, S)], buf_ref.at[e * S:(e + 1) * S], sem))
    return copies


def _scatter_body(off_ref, rounds_ref, sel_ref, x1_ref, g_ref, b_ref, ye_hbm, o_ref, p_ref, buf_ref, sem,
                  *, E, nblk, alpha):
    j = pl.program_id(0)
    S = p_ref.shape[0] // E
    slot = lax.rem(j, 2)

    def fetch(jj, first_slot, s):
        return _scatter_copies(ye_hbm, buf_ref.at[s], sem.at[s], off_ref, jj, first_slot, E, S, nblk)

    @pl.when(j == 0)
    def _():
        for cp in fetch(j, 0, 0):
            cp.start()

    @pl.when(j + 1 < nblk)
    def _():
        for cp in fetch(j + 1, 0, 1 - slot):
            cp.start()

    def one_round(r, s):
        count = _slot_onehot(sel_ref[...], p_ref, r * S, E)
        left = count - (r * S).astype(f32)
        srow = lax.broadcasted_iota(jnp.int32, (S, 1), 0).astype(f32)
        ye = buf_ref[s]
        parts = [jnp.where(srow < left[e:e + 1], ye[e * S:(e + 1) * S], 0.0) for e in range(E)]
        ye = jnp.concatenate(parts, axis=0).astype(bf16)
        return lax.dot_general(p_ref[...], ye, (((0,), (0,)), ((), ())), preferred_element_type=f32)

    for cp in fetch(j, 0, slot):
        cp.wait()
    moe = one_round(jnp.int32(0), slot)

    def extra(r, acc):
        for cp in fetch(j, r * S, 2):
            cp.start()
        for cp in fetch(j, r * S, 2):
            cp.wait()
        return acc + one_round(r, 2)

    moe = lax.fori_loop(1, rounds_ref[j], extra, moe)
    o_ref[...] = _layer_norm(alpha * x1_ref[...] + moe, g_ref[...], b_ref[...])


def _route_scatter(sel, x1, ye, off, rounds, ln_g, ln_b, alpha):
    E, n = sel.shape
    D = x1.shape[1]
    Tb, S = ROUTE_BLOCK, ROUTE_SLOTS
    nblk = n // Tb
    return pl.pallas_call(
        functools.partial(_scatter_body, E=E, nblk=nblk, alpha=alpha),
        grid_spec=pltpu.PrefetchScalarGridSpec(
            num_scalar_prefetch=2,
            grid=(nblk,),
            in_specs=[pl.BlockSpec((E, Tb), lambda j, *_: (0, j)),
                      pl.BlockSpec((Tb, D), lambda j, *_: (j, 0)),
                      pl.BlockSpec((1, D), lambda j, *_: (0, 0)),
                      pl.BlockSpec((1, D), lambda j, *_: (0, 0)),
                      pl.BlockSpec(memory_space=pl.ANY)],
            out_specs=pl.BlockSpec((Tb, D), lambda j, *_: (j, 0)),
            scratch_shapes=[pltpu.VMEM((E * S, Tb), bf16),
                            pltpu.VMEM((3, E * S, D), f32),
                            pltpu.SemaphoreType.DMA((3,))]),
        out_shape=jax.ShapeDtypeStruct((n, D), f32),
        compiler_params=_cparams("arbitrary"),
        name="route_scatter",
    )(off, rounds, sel, x1, ln_g, ln_b, ye)


def _expert_choice(x1, aff, w1, w3, w2, ln_g, ln_b, alpha):
    n, D = x1.shape
    E = aff.shape[1]
    cap = EC_CAPACITY * n // E
    Tb, S = ROUTE_BLOCK, ROUTE_SLOTS
    nblk = n // Tb
    sel = _select(aff.T, cap)
    cnt = sel.reshape(E, nblk, Tb).sum(-1).astype(jnp.int32)
    cnt_al = (cnt + ROW_ALIGN - 1) // ROW_ALIGN * ROW_ALIGN
    end = jnp.cumsum(cnt_al, axis=1)
    off = (end - cnt_al).reshape(E * nblk)
    total = end[:, -1]
    rounds = jnp.maximum((jnp.max(cnt, axis=0) + S - 1) // S, 1)
    rows_pad = -(-(cap + ROW_ALIGN * nblk + S) // ROW_TILE) * ROW_TILE
    xe = _route_gather(sel, x1, aff, off, rounds, total, rows_pad)
    ye = _ffn(total, xe, w1, w3, w2)
    return _route_scatter(sel, x1, ye, off, rounds, ln_g, ln_b, alpha)


FILT_FEAT_PAD = 32


def _lag_features(L):
    n = np.arange(2 * L, dtype=np.float64)
    pos = np.where(n < L, n, 2 * L - n)
    t = pos / (L - 1)
    bands = np.linspace(1e-4, FILT_BANDS - 1, FILT_BANDS)
    ang = 2.0 * math.pi * (pos / L)[:, None] * bands[None, :]
    z = np.concatenate([t[:, None], np.cos(ang), -np.sin(ang)], axis=-1)
    out = np.zeros((2 * L, FILT_FEAT_PAD), np.float32)
    out[:, :z.shape[1]] = z
    return out


def _filter_body(z_ref, w1_ref, b1_ref, f1_ref, w2_ref, b2_ref, f2_ref, w3_ref, dec_ref, o_ref, *, L):
    hp = lax.Precision.HIGHEST
    z = z_ref[...]
    h = jnp.sin(f1_ref[...] * (jnp.dot(z, w1_ref[...], precision=hp, preferred_element_type=f32) + b1_ref[...]))
    h = jnp.sin(f2_ref[...] * (jnp.dot(h, w2_ref[...], precision=hp, preferred_element_type=f32) + b2_ref[...]))
    h = jnp.dot(h, w3_ref[...], precision=hp, preferred_element_type=f32)
    window = jnp.exp(-z[:, 0:1] * jnp.abs(dec_ref[...])) + FILT_SHIFT
    tl = z.shape[0]
    row = pl.program_id(0) * tl + lax.broadcasted_iota(jnp.int32, h.shape, 0)
    o_ref[...] = jnp.where(row == L, 0.0, h * window)


def _circ_filters(L, w1, b1, f1, w2, b2, f2, w3, decay):
    nh = w1.shape[1]
    oc = w3.shape[1] // 2
    feats = jnp.asarray(_lag_features(L))
    w1p = jnp.zeros((FILT_FEAT_PAD, nh), f32).at[:w1.shape[0]].set(w1)
    tl = min(ROW_TILE, L)
    half = L // tl
    dirsel = lambda i: (i // half, 0, 0)
    return pl.pallas_call(
        functools.partial(_filter_body, L=L),
        grid=(2 * L // tl,),
        in_specs=[pl.BlockSpec((tl, FILT_FEAT_PAD), lambda i: (i, 0)),
                  _const_spec((FILT_FEAT_PAD, nh)), _const_spec((1, nh)), _const_spec((1, nh)),
                  _const_spec((nh, nh)), _const_spec((1, nh)), _const_spec((1, nh)),
                  pl.BlockSpec((None, nh, oc), dirsel),
                  pl.BlockSpec((None, 1, oc), dirsel)],
        out_specs=pl.BlockSpec((tl, oc), lambda i: (i, 0)),
        out_shape=jax.ShapeDtypeStruct((2 * L, oc), f32),
        compiler_params=_cparams("parallel"),
        name="hyena_filter",
    )(feats, w1p, b1.reshape(1, nh), f1.reshape(1, nh), w2, b2.reshape(1, nh), f2.reshape(1, nh),
      w3.reshape(nh, 2, oc).transpose(1, 0, 2), decay.reshape(2, 1, oc))


DFT_N2 = 128
FREQ_CHAINS = 8
COL_TILE = 4096


def _dft_tables(L):
    N, N2 = 2 * L, DFT_N2
    N1 = N // N2
    H1 = N1 // 2
    K1n = H1 + 1
    K1p = -(-K1n // 8) * 8
    k1 = np.arange(K1n)
    n1 = np.arange(N1)
    ang = 2.0 * math.pi * ((k1[:, None] * n1[None, :]) % N1) / N1
    f1 = np.zeros((2 * K1p, N1))
    f1[:K1n] = np.cos(ang)
    f1[K1p:K1p + K1n] = -np.sin(ang)
    wgt = np.full(K1n, 2.0)
    wgt[0] = wgt[H1] = 1.0
    ang = 2.0 * math.pi * ((np.arange(H1)[:, None] * k1[None, :]) % N1) / N1
    if1 = np.zeros((H1, 2 * K1p))
    if1[:, :K1n] = wgt * np.cos(ang) / N
    if1[:, K1p:K1p + K1n] = -wgt * np.sin(ang) / N
    k2 = np.arange(N2)
    ang = 2.0 * math.pi * ((k2[:, None] * k2[None, :]) % N2) / N2
    f2r, f2i = np.cos(ang), -np.sin(ang)
    ang = 2.0 * math.pi * (k1[:, None] * k2[None, :]) / N
    twr = np.zeros((K1p, 1, N2))
    twi = np.zeros((K1p, 1, N2))
    twr[:K1n, 0], twi[:K1n, 0] = np.cos(ang), -np.sin(ang)
    c = lambda a, dt: jnp.asarray(a.astype(np.float32)).astype(dt)
    return dict(N1=N1, H1=H1, K1p=K1p, f1=c(f1, bf16), if1=c(if1, bf16), f2r=c(f2r, f32), f2i=c(f2i, f32),
                twr=c(twr, f32), twi=c(twi, f32))


def _lmat_body(w_ref, x_ref, o_ref):
    o_ref[0] = jnp.dot(w_ref[...], x_ref[0].astype(bf16), preferred_element_type=f32).astype(o_ref.dtype)


def _dft_stage1(w, x):
    Bx, Kd, cols = x.shape
    Mo = w.shape[0]
    tc = min(COL_TILE, cols)
    return pl.pallas_call(
        _lmat_body,
        grid=(Bx, cols // tc),
        in_specs=[_const_spec((Mo, Kd)), pl.BlockSpec((1, Kd, tc), lambda b, j: (b, 0, j))],
        out_specs=pl.BlockSpec((1, Mo, tc), lambda b, j: (b, 0, j)),
        out_shape=jax.ShapeDtypeStruct((Bx, Mo, cols), bf16),
        compiler_params=_cparams("parallel", "parallel"),
        name="dft_stage1",
    )(w, x)


def _build_stage2_matrix(f2r_ref, f2i_ref, twr_ref, twi_ref, m_ref, mt_ref):
    n2 = f2r_ref.shape[0]
    twr, twi = twr_ref[...], twi_ref[...]
    re = f2r_ref[...] * twr - f2i_ref[...] * twi
    im = f2r_ref[...] * twi + f2i_ref[...] * twr
    m_ref[:n2, :n2] = re.astype(bf16)
    m_ref[:n2, n2:] = (-im).astype(bf16)
    m_ref[n2:, :n2] = im.astype(bf16)
    m_ref[n2:, n2:] = re.astype(bf16)
    if mt_ref is not None:
        ret, imt = re.T, im.T
        mt_ref[:n2, :n2] = ret.astype(bf16)
        mt_ref[:n2, n2:] = imt.astype(bf16)
        mt_ref[n2:, :n2] = (-imt).astype(bf16)
        mt_ref[n2:, n2:] = ret.astype(bf16)


def _spectrum_body(f2r_ref, f2i_ref, twr_ref, twi_ref, a_ref, x_ref, m_ref):
    _build_stage2_matrix(f2r_ref, f2i_ref, twr_ref, twi_ref, m_ref, None)
    n2, cw = a_ref.shape[1:]
    x = jnp.dot(m_ref[...], a_ref[...].reshape(2 * n2, cw), preferred_element_type=f32)
    x_ref[...] = x.reshape(2, n2, cw)


def _filter_spectrum(tab, a5):
    _, _, K1p, N2, Cw = a5.shape
    tw_spec = pl.BlockSpec((None, 1, N2), lambda k: (k, 0, 0))
    return pl.pallas_call(
        _spectrum_body,
        grid=(K1p,),
        in_specs=[_const_spec((N2, N2)), _const_spec((N2, N2)), tw_spec, tw_spec,
                  pl.BlockSpec((None, 2, None, N2, Cw), lambda k: (0, 0, k, 0, 0))],
        out_specs=pl.BlockSpec((None, 2, N2, Cw), lambda k: (k, 0, 0, 0)),
        out_shape=jax.ShapeDtypeStruct((K1p, 2, N2, Cw), f32),
        scratch_shapes=[pltpu.VMEM((2 * N2, 2 * N2), bf16)],
        compiler_params=_cparams("arbitrary"),
        name="filter_spectrum",
    )(tab['f2r'], tab['f2i'], tab['twr'], tab['twi'], a5)


def _freq_body(f2r_ref, f2i_ref, twr_ref, twi_ref, a_ref, h_ref, g_ref, m_ref, mt_ref):
    nb, _, nk, n2, c = a_ref.shape
    for kk in range(nk):
        _build_stage2_matrix(f2r_ref, f2i_ref, twr_ref.at[kk], twi_ref.at[kk], m_ref.at[kk], mt_ref.at[kk])
        hr, hi = h_ref[kk, 0], h_ref[kk, 1]
        for b in range(nb):
            x = jnp.dot(m_ref[kk], a_ref[b, :, kk].reshape(2 * n2, c), preferred_element_type=f32)
            xr, xi = x[:n2], x[n2:]
            y = jnp.concatenate([xr * hr - xi * hi, xr * hi + xi * hr], axis=0).astype(bf16)
            g = jnp.dot(mt_ref[kk], y, preferred_element_type=f32)
            g_ref[b, :, kk] = g.reshape(2, n2, c).astype(bf16)


def _freq_stage(tab, a5, hspec, order):
    B, _, K1p, N2, C = a5.shape
    nb = min(B, FREQ_CHAINS)
    nk = FREQ_CHAINS // nb
    tw_spec = pl.BlockSpec((nk, 1, N2), lambda k, b: (k, 0, 0))
    slab = pl.BlockSpec((nb, 2, nk, N2, C), lambda k, b: (b, 0, k, 0, 0))
    return pl.pallas_call(
        _freq_body,
        grid=(K1p // nk, B // nb),
        in_specs=[_const_spec((N2, N2)), _const_spec((N2, N2)), tw_spec, tw_spec, slab,
                  pl.BlockSpec((nk, 2, N2, C), lambda k, b: (k, 0, 0, order))],
        out_specs=slab,
        out_shape=jax.ShapeDtypeStruct(a5.shape, bf16),
        scratch_shapes=[pltpu.VMEM((nk, 2 * N2, 2 * N2), bf16), pltpu.VMEM((nk, 2 * N2, 2 * N2), bf16)],
        compiler_params=_cparams("parallel", "parallel"),
        name="freq_stage",
    )(tab['f2r'], tab['f2i'], tab['twr'], tab['twi'], a5, hspec)


DFT_SLABS = 16
LANE = 128


def _slabs_to_matrix(load_tile, rows, S, C, stage_ref):
    for lt in range(C // LANE):
        stage_ref[lt, :rows * S] = load_tile(lt)
    cols = [stage_ref[lt, pl.ds(sl, rows, stride=S), :] for sl in range(S) for lt in range(C // LANE)]
    return jnp.concatenate(cols, axis=1)


def _matrix_to_slabs(mat, rows, S, C, stage_ref):
    for sl in range(S):
        for lt in range(C // LANE):
            lo = sl * C + lt * LANE
            stage_ref[lt, pl.ds(sl, rows, stride=S), :] = mat[:, lo:lo + LANE]
    return [stage_ref[lt, :rows * S].reshape(rows, S, LANE) for lt in range(C // LANE)]


def _stage1_body(w_ref, z_ref, o_ref, stage_ref):
    H1, S, C = z_ref.shape
    K1p = o_ref.shape[1]
    zmat = _slabs_to_matrix(lambda lt: z_ref[:, :, lt * LANE:(lt + 1) * LANE].reshape(H1 * S, LANE), H1, S, C, stage_ref)
    a = jnp.dot(w_ref[...], zmat.astype(bf16), preferred_element_type=f32)
    for lt, tile in enumerate(_matrix_to_slabs(a, 2 * K1p, S, C, stage_ref)):
        o_ref[0, :, :, lt * LANE:(lt + 1) * LANE] = tile[:K1p].astype(bf16)
        o_ref[1, :, :, lt * LANE:(lt + 1) * LANE] = tile[K1p:].astype(bf16)


def _data_stage1(w, z4, K1p):
    B, H1, N2, C = z4.shape
    S = DFT_SLABS
    return pl.pallas_call(
        _stage1_body,
        grid=(B, N2 // S),
        in_specs=[_const_spec(w.shape), pl.BlockSpec((None, H1, S, C), lambda b, j: (b, 0, j, 0))],
        out_specs=pl.BlockSpec((None, 2, K1p, S, C), lambda b, j: (b, 0, 0, j, 0)),
        out_shape=jax.ShapeDtypeStruct((B, 2, K1p, N2, C), bf16),
        scratch_shapes=[pltpu.VMEM((C // LANE, 2 * K1p * S, LANE), f32)],
        compiler_params=_cparams("parallel", "parallel"),
        name="dft_data_stage1",
    )(w, z4)


def _conv_out_body(w_ref, g_ref, z_ref, gate_ref, bias_ref, o_ref, stage_ref):
    _, K1p, S, C = g_ref.shape
    H1 = z_ref.shape[0]

    def load_tile(lt):
        parts = [g_ref[ri, :, :, lt * LANE:(lt + 1) * LANE].astype(f32).reshape(K1p * S, LANE) for ri in range(2)]
        return jnp.concatenate(parts, axis=0)

    gmat = _slabs_to_matrix(load_tile, 2 * K1p, S, C, stage_ref)
    y = jnp.dot(w_ref[...], gmat.astype(bf16), preferred_element_type=f32)
    for lt, tile in enumerate(_matrix_to_slabs(y, H1, S, C, stage_ref)):
        sl = slice(lt * LANE, (lt + 1) * LANE)
        o_ref[:, :, sl] = gate_ref[:, :, sl] * (tile + z_ref[:, :, sl] * bias_ref[:, sl].reshape(1, 1, LANE))


def _conv_out(w, g5, z4, gate4, bias):
    B, H1, N2, C = z4.shape
    K1p = g5.shape[2]
    S = DFT_SLABS
    blk = pl.BlockSpec((None, H1, S, C), lambda b, j: (b, 0, j, 0))
    return pl.pallas_call(
        _conv_out_body,
        grid=(B, N2 // S),
        in_specs=[_const_spec(w.shape), pl.BlockSpec((None, 2, K1p, S, C), lambda b, j: (b, 0, 0, j, 0)), blk, blk,
                  _const_spec((1, C))],
        out_specs=blk,
        out_shape=jax.ShapeDtypeStruct(z4.shape, f32),
        scratch_shapes=[pltpu.VMEM((C // LANE, 2 * K1p * S, LANE), f32)],
        compiler_params=_cparams("parallel", "parallel"),
        name="conv_out",
    )(w, g5, z4, gate4, bias)


def _long_conv(v, g1, g2, circ, hy_bias, B, L, C):
    tab = _dft_tables(L)
    N2, N1, H1, K1p = DFT_N2, tab['N1'], tab['H1'], tab['K1p']
    Cw = circ.shape[1]
    a5 = _dft_stage1(tab['f1'], circ.reshape(1, N1, N2 * Cw)).reshape(1, 2, K1p, N2, Cw)
    hspec = _filter_spectrum(tab, a5)
    z4 = v.reshape(B, H1, N2, C)
    gates = (g1.reshape(B, H1, N2, C), g2.reshape(B, H1, N2, C))
    f1h = tab['f1'][:, :H1]
    for o in range(HY_ORDER):
        g5 = _freq_stage(tab, _data_stage1(f1h, z4, K1p), hspec, o)
        z4 = _conv_out(tab['if1'], g5, z4, gates[o], hy_bias[o].reshape(1, C))
    return z4.reshape(B * L, C)


def _encoder_layer(x, p, alpha):
    B, L, D = x.shape
    n = B * L
    x2 = x.reshape(n, D)
    G, P, H = p['s5_b_re'].shape[1:]
    s5w = G * H
    hyw = p['w_hy_proj'].shape[0]

    u_s5, u_hy, g_s, g_h = _inproj(x2, p['w_in'].astype(bf16), B, L, s5w, hyw)

    chained = B == 1
    Bs, Ls = (S5_ROWS, L // S5_ROWS) if chained else (B, L)
    y_g = _s5(_to_chunk_rows(u_s5, G, H), _s5_matrices(p, Ls), Bs, chained)
    y_s5 = _to_token_rows(y_g, H)

    v, g1, g2 = _shortconv(u_hy, p['hy_short_w'], p['hy_short_b'], B, L, hyw)
    circ = _circ_filters(L, p['filt_w1'], p['filt_b1'], p['filt_freq1'], p['filt_w2'], p['filt_b2'],
                         p['filt_freq2'], p['filt_w3'], p['filt_decay'])
    z = _long_conv(v, g1, g2, circ, p['hy_bias'], B, L, hyw)

    x1, aff = _merge(x2, y_s5, z, g_s, g_h, p['s5_w_glu'].astype(bf16), p['w_s5_proj'].astype(bf16),
                     p['w_hy_proj'].astype(bf16), p['w_out'].astype(bf16),
                     p['ln1_g'].reshape(1, D), p['ln1_b'].reshape(1, D), p['w_router'], B, L, alpha)

    out = _expert_choice(x1, aff, p['ex_w1_bf16'], p['ex_w3_bf16'], p['ex_w2_bf16'],
                         p['ln2_g'].reshape(1, D), p['ln2_b'].reshape(1, D), alpha)
    return out.reshape(B, L, D)


_PARAM_NAMES = ('w_in', 's5_lambda_re', 's5_lambda_im', 's5_log_dt', 's5_b_re', 's5_b_im',
                's5_c_re', 's5_c_im', 's5_d', 's5_w_glu', 'w_s5_proj',
                'hy_short_w', 'hy_short_b', 'filt_w1', 'filt_b1', 'filt_freq1', 'filt_w2', 'filt_b2',
                'filt_freq2', 'filt_w3', 'filt_decay', 'hy_bias', 'w_hy_proj', 'w_out',
                'ln1_g', 'ln1_b', 'w_router', 'ex_w1', 'ex_w3', 'ex_w2', 'ln2_g', 'ln2_b')


def kernel(x_prompt, x_sample, w_in, s5_lambda_re, s5_lambda_im, s5_log_dt, s5_b_re, s5_b_im, s5_c_re, s5_c_im, s5_d, s5_w_glu, w_s5_proj, hy_short_w, hy_short_b, filt_w1, filt_b1, filt_freq1, filt_w2, filt_b2, filt_freq2, filt_w3, filt_decay, hy_bias, w_hy_proj, w_out, ln1_g, ln1_b, w_router, ex_w1, ex_w3, ex_w2, ln2_g, ln2_b):
    stacked = (w_in, s5_lambda_re, s5_lambda_im, s5_log_dt, s5_b_re, s5_b_im, s5_c_re, s5_c_im, s5_d,
               s5_w_glu, w_s5_proj, hy_short_w, hy_short_b, filt_w1, filt_b1, filt_freq1, filt_w2,
               filt_b2, filt_freq2, filt_w3, filt_decay, hy_bias, w_hy_proj, w_out, ln1_g, ln1_b,
               w_router, ex_w1, ex_w3, ex_w2, ln2_g, ln2_b)
    depth = w_in.shape[0]
    alpha = (2.0 * depth) ** 0.25
    outs = []
    for x in (x_prompt, x_sample):
        for l in range(depth):
            p = {k: v[l] for k, v in zip(_PARAM_NAMES, stacked)}
            for k in ('ex_w1', 'ex_w3', 'ex_w2'):
                p[k + '_bf16'] = p[k].astype(bf16)
            x = _encoder_layer(x, p, alpha)
        outs.append(x)
    return tuple(outs)
```

```python
import functools
import math

import jax
import jax.numpy as jnp
import numpy as np
from jax import lax
from jax.experimental import pallas as pl
from jax.experimental.pallas import tpu as pltpu

FILT_BANDS = 8
FILT_SHIFT = 0.05
EC_CAPACITY = 2
LN_EPS = 1e-5
HY_ORDER = 2

VMEM_LIMIT_BYTES = 56 * 1024 * 1024
ROW_TILE = 512

bf16 = jnp.bfloat16
f32 = jnp.float32


def _cparams(*sem):
    return pltpu.CompilerParams(dimension_semantics=sem, vmem_limit_bytes=VMEM_LIMIT_BYTES)


def _const_spec(shape):
    return pl.BlockSpec(shape, lambda *_: (0,) * len(shape))


def _inproj_body(xp_ref, x_ref, xn_ref, w_ref, cw_ref, cb_ref, ug_ref, v_ref, g1_ref, g2_ref, gs_ref, gh_ref,
                 stage_ref, *, o1, o2, o3, nt, H):
    i = pl.program_id(1)
    proj = jnp.dot(x_ref[...].astype(bf16), w_ref[...], preferred_element_type=f32)
    _rows_to_chunks(proj[:, :o1], stage_ref, ug_ref, H)
    gs_ref[...] = jax.nn.sigmoid(proj[:, o2:o3])
    gh_ref[...] = jax.nn.sigmoid(proj[:, o3:])
    u = proj[:, o1:o2]
    w_hy = w_ref[:, o1:o2]
    prev_row = jnp.dot(xp_ref[...].astype(bf16), w_hy, preferred_element_type=f32)[7:8]
    next_row = jnp.dot(xn_ref[...].astype(bf16), w_hy, preferred_element_type=f32)[0:1]
    prev_row = jnp.where(i == 0, 0.0, prev_row)
    next_row = jnp.where(i == nt - 1, 0.0, next_row)
    tl = u.shape[0]
    rows = lax.broadcasted_iota(jnp.int32, u.shape, 0)
    um1 = jnp.where(rows == 0, prev_row, pltpu.roll(u, 1, axis=0))
    up1 = jnp.where(rows == tl - 1, next_row, pltpu.roll(u, tl - 1, axis=0))
    cw = cw_ref[...]
    hy = um1 * cw[0:1, :] + u * cw[1:2, :] + up1 * cw[2:3, :] + cb_ref[...]
    hyw = v_ref.shape[1]
    v_ref[...] = hy[:, :hyw]
    g1_ref[...] = hy[:, hyw:2 * hyw]
    g2_ref[...] = hy[:, 2 * hyw:]


def _inproj(x2, w_in, conv_w, conv_b, B, L, G, H, hyw):
    n, D = x2.shape
    cols = w_in.shape[1]
    s5w = G * H
    o1, o2 = s5w, s5w + 3 * hyw
    o3 = o2 + D
    tm = min(ROW_TILE, L)
    nt = L // tm
    r8 = tm // 8
    row = lambda b, i: (b * nt + i, 0)
    prev = lambda b, i: (jnp.maximum((b * nt + i) * r8 - 1, 0), 0)
    nxt = lambda b, i: (jnp.minimum((b * nt + i + 1) * r8, n // 8 - 1), 0)
    hy_out = jax.ShapeDtypeStruct((n, hyw), f32)
    return pl.pallas_call(
        functools.partial(_inproj_body, o1=o1, o2=o2, o3=o3, nt=nt, H=H),
        grid=(B, nt),
        in_specs=[pl.BlockSpec((8, D), prev), pl.BlockSpec((tm, D), row), pl.BlockSpec((8, D), nxt),
                  _const_spec((D, cols)), _const_spec((3, 3 * hyw)), _const_spec((1, 3 * hyw))],
        out_specs=[pl.BlockSpec((G, tm // S5_CHUNK, S5_CHUNK * H), lambda b, i: (0, b * nt + i, 0)),
                   pl.BlockSpec((tm, hyw), row), pl.BlockSpec((tm, hyw), row), pl.BlockSpec((tm, hyw), row),
                   pl.BlockSpec((tm, D), row), pl.BlockSpec((tm, D), row)],
        out_shape=[jax.ShapeDtypeStruct((G, n // S5_CHUNK, S5_CHUNK * H), bf16), hy_out, hy_out, hy_out,
                   jax.ShapeDtypeStruct((n, D), f32), jax.ShapeDtypeStruct((n, D), f32)],
        scratch_shapes=[pltpu.VMEM((s5w // LANE, tm, LANE), f32)],
        compiler_params=_cparams("parallel", "parallel"),
        name="inproj",
    )(x2, x2, x2, w_in, conv_w, conv_b.reshape(1, 3 * hyw))


S5_CHUNK = 16
S5_ROWS = 8
LANE = 128


def _s5_body(u_ref, kmat_ref, winf_ref, winb_ref, woutf_ref, woutb_ref, af_ref, ab_ref, y_ref,
             vf_ref, vb_ref, sf_ref, sb_ref, *, R, nc, chained):
    u = u_ref[0]
    hw = sf_ref.shape[1]
    seq_rows = lambda b: pl.ds(b, nc, stride=R)
    for v_ref, w_ref in ((vf_ref, winf_ref), (vb_ref, winb_ref)):
        v = jnp.dot(u, w_ref[0], preferred_element_type=f32)
        for b in range(R):
            v_ref[0, seq_rows(b), :] = v[b * nc:(b + 1) * nc, :hw]
            v_ref[1, seq_rows(b), :] = v[b * nc:(b + 1) * nc, hw:]
    bc = lambda ref, row, lo: jnp.broadcast_to(ref[0, row:row + 1, lo:lo + hw], (R, hw))
    decay = lambda ref, row: ((bc(ref, row, 0), bc(ref, row, hw)), (bc(ref, row + 1, 0), bc(ref, row + 1, hw)))
    f_a1, f_a2 = decay(af_ref, 0)
    b_a1, b_a2 = decay(ab_ref, 0)

    def mul_add(s0, s1, a1, a2, v0, v1):
        return a1[0] * s0 + a2[0] * s1 + v0, a1[1] * s1 + a2[1] * s0 + v1

    def scan(init, store):
        def step(i, carry):
            f0, f1, b0, b1 = carry
            rf = pl.ds(pl.multiple_of(i * R, R), R)
            rb = pl.ds(pl.multiple_of((nc - 1 - i) * R, R), R)
            if store:
                sf_ref[rf, :] = f0
                sb_ref[rb, :] = b0
            f0, f1 = mul_add(f0, f1, f_a1, f_a2, vf_ref[0, rf, :], vf_ref[1, rf, :])
            b0, b1 = mul_add(b0, b1, b_a1, b_a2, vb_ref[0, rb, :], vb_ref[1, rb, :])
            return f0, f1, b0, b1
        return lax.fori_loop(0, nc, step, init, unroll=4)

    zero = jnp.zeros((R, hw), f32)
    init = (zero, zero, zero, zero)
    if chained:
        ef0, ef1, eb0, eb1 = scan(init, store=False)
        fn_a1, fn_a2 = decay(af_ref, 2)
        bn_a1, bn_a2 = decay(ab_ref, 2)
        row = lax.broadcasted_iota(jnp.int32, (R, hw), 0)
        down = lambda v: jnp.where(row == 0, 0.0, pltpu.roll(v, 1, axis=0))
        up = lambda v: jnp.where(row == R - 1, 0.0, pltpu.roll(v, R - 1, axis=0))
        f0, f1, b0, b1 = init
        for _ in range(R - 1):
            t0, t1 = mul_add(f0, f1, fn_a1, fn_a2, ef0, ef1)
            f0, f1 = down(t0), down(t1)
            t0, t1 = mul_add(b0, b1, bn_a1, bn_a2, eb0, eb1)
            b0, b1 = up(t0), up(t1)
        init = (f0, f1, b0, b1)
    scan(init, store=True)
    y = jnp.dot(u, kmat_ref[0], preferred_element_type=f32)
    for s_ref, w_ref in ((sf_ref, woutf_ref), (sb_ref, woutb_ref)):
        states = jnp.concatenate([s_ref[seq_rows(b), :] for b in range(R)], axis=0)
        y = y + jnp.dot(states.astype(bf16), w_ref[0], preferred_element_type=f32)
    y_ref[0] = y


def _s5(u_g, mats, R, chained):
    G, M, W = u_g.shape
    kmat, winf, winb, woutf, woutb, af, ab = mats
    hw = woutf.shape[1]
    grp = lambda shape: pl.BlockSpec((1,) + shape, lambda g: (g, 0, 0))
    return pl.pallas_call(
        functools.partial(_s5_body, R=R, nc=M // R, chained=chained),
        grid=(G,),
        in_specs=[grp((M, W)), grp((W, W)), grp((W, 2 * hw)), grp((W, 2 * hw)), grp((hw, W)), grp((hw, W)),
                  grp((4, 2 * hw)), grp((4, 2 * hw))],
        out_specs=grp((M, W)),
        out_shape=jax.ShapeDtypeStruct((G, M, W), f32),
        scratch_shapes=[pltpu.VMEM((2, M, hw), f32), pltpu.VMEM((2, M, hw), f32),
                        pltpu.VMEM((M, hw), f32), pltpu.VMEM((M, hw), f32)],
        compiler_params=_cparams("parallel"),
        name="s5_chunked",
    )(u_g, kmat, winf, winb, woutf, woutb, af, ab)


def _rows_to_chunks(x, stage_ref, o_ref, H):
    Tc = S5_CHUNK
    nr = x.shape[0] // Tc
    gpt, per_tile = LANE // H, LANE // H
    for lt in range(x.shape[1] // LANE):
        stage_ref[lt] = x[:, lt * LANE:(lt + 1) * LANE]
        rows_t = [stage_ref[lt, pl.ds(t, nr, stride=Tc), :] for t in range(Tc)]
        for g in range(gpt):
            for j in range(Tc // per_tile):
                tile = jnp.concatenate([rows_t[per_tile * j + k][:, H * g:H * (g + 1)] for k in range(per_tile)], axis=1)
                o_ref[lt * gpt + g, :, j * LANE:(j + 1) * LANE] = tile.astype(o_ref.dtype)


def _chunks_to_rows(y_ref, stage_ref, H):
    Tc = S5_CHUNK
    nr = y_ref.shape[1]
    gpt = LANE // H
    for lt in range(stage_ref.shape[0]):
        for t in range(Tc):
            tile = jnp.concatenate([y_ref[lt * gpt + g, :, H * t:H * (t + 1)] for g in range(gpt)], axis=1)
            stage_ref[lt, pl.ds(t, nr, stride=Tc), :] = tile
    return jnp.concatenate([stage_ref[lt] for lt in range(stage_ref.shape[0])], axis=1)


def _s5_discretise(lam_re, lam_im, log_dt, b_re, b_im, powers):
    lam = lax.complex(-jnp.abs(lam_re.astype(f32)), lam_im.astype(f32))
    dt = jnp.exp(log_dt.astype(f32))[:, None]
    a_bar = jnp.exp(lam * dt)
    k = jnp.asarray(powers, f32)[None, :, None]
    apow = jnp.exp((lam * dt)[:, None, :] * k)
    b_bar = ((a_bar - 1.0) / lam)[..., None] * lax.complex(b_re.astype(f32), b_im.astype(f32))
    return apow, b_bar


def _s5_matrices(p, seg_steps):
    Tc = S5_CHUNK
    powers = list(range(Tc + 1)) + list(seg_steps)
    apf, bbf = _s5_discretise(p['s5_lambda_re'][0], p['s5_lambda_im'][0], p['s5_log_dt'][0],
                              p['s5_b_re'][0], p['s5_b_im'][0], powers)
    apb, bbb = _s5_discretise(p['s5_lambda_re'][1], p['s5_lambda_im'][1], p['s5_log_dt'][1],
                              p['s5_b_re'][1], p['s5_b_im'][1], powers)
    c = lax.complex(p['s5_c_re'].astype(f32), p['s5_c_im'].astype(f32))
    G, H, P = c.shape
    W = Tc * H
    kf = jnp.real(jnp.einsum('ghp,gtp,gpk->gthk', c, apf[:, :Tc], bbf))
    kb = jnp.real(jnp.einsum('ghp,gtp,gpk->gthk', c, apb[:, :Tc], bbb))
    tau = jnp.arange(Tc)[None, :] - jnp.arange(Tc)[:, None]
    blk = (jnp.where((tau >= 0)[None, :, :, None, None], kf[:, jnp.abs(tau)], 0.0)
           + jnp.where((tau <= 0)[None, :, :, None, None], kb[:, jnp.abs(tau)], 0.0))
    kmat = jnp.transpose(blk, (0, 1, 4, 2, 3)).reshape(G, W, W)
    kmat = kmat + jnp.eye(W, dtype=f32)[None] * jnp.tile(p['s5_d'].astype(f32), (1, Tc))[:, None, :]
    lanes = lambda z: jnp.concatenate([jnp.real(z), jnp.imag(z), jnp.imag(z), jnp.real(z)], axis=-1)
    win = lambda ap, bb: lanes(jnp.einsum('gsp,gpk->gskp', ap, bb)).reshape(G, W, 4 * P)
    winf = win(apf[:, Tc - 1::-1][:, :Tc], bbf)
    winb = win(apb[:, :Tc], bbb)
    def wout(ap):
        z = jnp.einsum('ghp,gtp->gpth', c, ap).reshape(G, P, W)
        return jnp.concatenate([jnp.real(z), -jnp.imag(z)], axis=1)
    woutf = wout(apf[:, 1:Tc + 1])
    woutb = wout(apb[:, Tc:0:-1])
    def chunk_decay(ap, seg):
        rows = []
        for k in (Tc, Tc + 1 + seg):
            ar, ai = jnp.real(ap[:, k]), jnp.imag(ap[:, k])
            rows += [jnp.concatenate([ar, ar, ar, ar], -1), jnp.concatenate([-ai, ai, ai, -ai], -1)]
        return jnp.stack(rows, axis=1)
    cast = lambda m: m.astype(bf16)
    shared = (cast(kmat), cast(winf), cast(winb), cast(woutf), cast(woutb))
    return [shared + (chunk_decay(apf, i), chunk_decay(apb, i)) for i in range(len(seg_steps))]


def _layer_norm(v, g, b):
    mu = jnp.mean(v, axis=-1, keepdims=True)
    c = v - mu
    var = jnp.mean(c * c, axis=-1, keepdims=True)
    return c * lax.rsqrt(var + LN_EPS) * g + b


def _merge_body(x_ref, yg_ref, z_ref, gs_ref, gh_ref, wglu_ref, wsp_ref, whp_ref, wout_ref,
                g_ref, b_ref, wrh_ref, wrl_ref, x1_ref, aff_ref, stage_ref, *, alpha, H):
    ys = jax.nn.gelu(_chunks_to_rows(yg_ref, stage_ref, H))
    gate = jax.nn.sigmoid(jnp.dot(ys.astype(bf16), wglu_ref[...], preferred_element_type=f32))
    branch_s = jnp.dot((ys * gate).astype(bf16), wsp_ref[...], preferred_element_type=f32)
    branch_h = jnp.dot(z_ref[...].astype(bf16), whp_ref[...], preferred_element_type=f32)
    mix = gs_ref[...] * branch_s + gh_ref[...] * branch_h
    mix = jnp.dot(mix.astype(bf16), wout_ref[...], preferred_element_type=f32)
    x1 = _layer_norm(alpha * x_ref[...] + mix, g_ref[...], b_ref[...])
    x1_ref[...] = x1
    x1_hi = x1.astype(bf16)
    x1_lo = (x1 - x1_hi.astype(f32)).astype(bf16)
    logits = (jnp.dot(x1_hi, wrh_ref[...], preferred_element_type=f32)
              + jnp.dot(x1_lo, wrh_ref[...], preferred_element_type=f32)
              + jnp.dot(x1_hi, wrl_ref[...], preferred_element_type=f32))
    m = jnp.max(logits, axis=-1, keepdims=True)
    e = jnp.exp(logits - m)
    aff_ref[...] = e / jnp.sum(e, axis=-1, keepdims=True)


def _merge(x2, y_g, z, g_s, g_h, w_glu, w_sp, w_hp, w_out, ln_g, ln_b, w_router, B, L, alpha):
    n, D = x2.shape
    G, _, W = y_g.shape
    H = W // S5_CHUNK
    s5w, hyw = G * H, z.shape[1]
    E = w_router.shape[1]
    tm = min(ROW_TILE, L)
    nt = L // tm
    row = lambda b, i: (b * nt + i, 0)
    wr_hi = w_router.astype(bf16)
    wr_lo = (w_router - wr_hi.astype(f32)).astype(bf16)
    return pl.pallas_call(
        functools.partial(_merge_body, alpha=alpha, H=H),
        grid=(B, nt),
        in_specs=[pl.BlockSpec((tm, D), row),
                  pl.BlockSpec((G, tm // S5_CHUNK, W), lambda b, i: (0, b * nt + i, 0)),
                  pl.BlockSpec((tm, hyw), row),
                  pl.BlockSpec((tm, D), row),
                  pl.BlockSpec((tm, D), row),
                  _const_spec(w_glu.shape), _const_spec(w_sp.shape), _const_spec(w_hp.shape),
                  _const_spec(w_out.shape), _const_spec((1, D)), _const_spec((1, D)),
                  _const_spec(w_router.shape), _const_spec(w_router.shape)],
        out_specs=[pl.BlockSpec((tm, D), row), pl.BlockSpec((tm, E), row)],
        out_shape=[jax.ShapeDtypeStruct((n, D), f32), jax.ShapeDtypeStruct((n, E), f32)],
        scratch_shapes=[pltpu.VMEM((s5w // LANE, tm, LANE), f32)],
        compiler_params=_cparams("parallel", "parallel"),
        name="merge",
    )(x2, y_g, z, g_s, g_h, w_glu, w_sp, w_hp, w_out, ln_g, ln_b, wr_hi, wr_lo)


ROUTE_BLOCK = 256
ROUTE_SLOTS = 64
ROW_ALIGN = 8
GATE_LANES = 128


def _select_body(aff_ref, sel_ref, *, cap, idx_bits):
    bits = pltpu.bitcast(aff_ref[...], jnp.int32)
    E = bits.shape[0]
    count = lambda m: jnp.sum(jnp.where(m, 1.0, 0.0), axis=1, keepdims=True)

    def value_bit(i, prefix):
        cand = prefix | jnp.left_shift(jnp.int32(1), 30 - i)
        return jnp.where(count(bits >= cand) >= cap, cand, prefix)

    thr = lax.fori_loop(0, 31, value_bit, jnp.zeros((E, 1), jnp.int32))
    need = cap - count(bits > thr)
    idx = lax.broadcasted_iota(jnp.int32, bits.shape, 1)
    tie_idx = jnp.where(bits == thr, idx, jnp.int32(2 ** 30))

    def index_bit(i, bound):
        cand = bound | jnp.left_shift(jnp.int32(1), idx_bits - 1 - i)
        return jnp.where(count(tie_idx < cand) <= need, cand, bound)

    bound = lax.fori_loop(0, idx_bits, index_bit, jnp.zeros((E, 1), jnp.int32))
    sel_ref[...] = jnp.where(bits > thr, 1.0, jnp.where(tie_idx < bound, 1.0, 0.0))


def _select(aff_t, cap):
    E, n = aff_t.shape
    return pl.pallas_call(
        functools.partial(_select_body, cap=float(cap), idx_bits=int(n).bit_length()),
        out_shape=jax.ShapeDtypeStruct((E, n), f32),
        compiler_params=pltpu.CompilerParams(vmem_limit_bytes=VMEM_LIMIT_BYTES),
        name="expert_select",
    )(aff_t)


def _slot_onehot(sel, p_ref, first_slot, E):
    Tb = sel.shape[1]
    S = p_ref.shape[0] // E
    r = lax.broadcasted_iota(jnp.int32, (Tb, Tb), 0)
    c = lax.broadcasted_iota(jnp.int32, (Tb, Tb), 1)
    tri = jnp.where(r <= c, 1.0, 0.0).astype(bf16)
    incl = jnp.dot(sel.astype(bf16), tri, preferred_element_type=f32)
    slot = jnp.where(sel > 0.0, incl - 1.0, -1.0)
    want = (lax.broadcasted_iota(jnp.int32, (S, Tb), 0) + first_slot).astype(f32)
    for e in range(E):
        p_ref[e * S:(e + 1) * S, :] = jnp.where(slot[e:e + 1] == want, 1.0, 0.0).astype(bf16)
    return incl[:, Tb - 1:Tb]


def _gather_copies(stage_ref, xe_hbm, sem, off_ref, j, first_slot, E, S, nblk):
    copies = []
    for e in range(E):
        row = pl.multiple_of(off_ref[e * nblk + j] + first_slot, ROW_ALIGN)
        copies.append(pltpu.make_async_copy(stage_ref.at[e * S:(e + 1) * S], xe_hbm.at[e, pl.ds(row, S)], sem))
    return copies


def _gather_body(off_ref, rounds_ref, total_ref, sel_ref, x_ref, aff_ref, xe_hbm, p_ref, stage_ref, sem,
                 *, E, nblk, cap):
    j = pl.program_id(0)
    S = p_ref.shape[0] // E
    D = x_ref.shape[1]
    x = x_ref[...].astype(bf16)
    aff = aff_ref[...]
    a_hi = aff.astype(bf16)
    rem = aff - a_hi.astype(f32)
    a_mid = rem.astype(bf16)
    a_lo = (rem - a_mid.astype(f32)).astype(bf16)
    aff3 = jnp.concatenate([a_hi, a_mid, a_lo], axis=1)
    own = (lax.broadcasted_iota(jnp.int32, (E * S, 3 * E), 0) // S) == (
        lax.broadcasted_iota(jnp.int32, (E * S, 3 * E), 1) % E)

    def one_round(r, first_step):
        first_slot = r * S
        _slot_onehot(sel_ref[...], p_ref, first_slot, E)
        p = p_ref[...]
        rows = jnp.dot(p, x, preferred_element_type=f32)
        gates = jnp.dot(p, aff3, preferred_element_type=f32)
        gate = jnp.sum(jnp.where(own, gates, 0.0), axis=1, keepdims=True)

        @pl.when(jnp.logical_not(first_step))
        def _():
            for cp in _gather_copies(stage_ref, xe_hbm, sem, off_ref, j, 0, E, S, nblk):
                cp.wait()

        stage_ref[:, :D] = rows
        stage_ref[:, D:] = jnp.broadcast_to(gate, (E * S, GATE_LANES))
        for cp in _gather_copies(stage_ref, xe_hbm, sem, off_ref, j, first_slot, E, S, nblk):
            cp.start()

    one_round(0, j == 0)

    def extra(r, carry):
        one_round(r, False)
        return carry

    lax.fori_loop(1, rounds_ref[j], extra, 0)

    @pl.when(j == nblk - 1)
    def _():
        for cp in _gather_copies(stage_ref, xe_hbm, sem, off_ref, j, 0, E, S, nblk):
            cp.wait()
        rows_pad = xe_hbm.shape[1]
        stage_ref[:S] = jnp.zeros((S, D + GATE_LANES), f32)
        nfill = -(-(rows_pad - cap) // S)

        def fill(e, row):
            return pltpu.make_async_copy(stage_ref.at[:S], xe_hbm.at[e, pl.ds(pl.multiple_of(row, ROW_ALIGN), S)], sem)

        def whole_chunks(e, action):
            def body(k, c):
                row = total_ref[e] + k * S

                @pl.when(row + S <= rows_pad)
                def _():
                    action(fill(e, row))
                return c
            lax.fori_loop(0, nfill, body, 0)

        for e in range(E):
            whole_chunks(e, lambda cp: cp.start())
        for e in range(E):
            whole_chunks(e, lambda cp: cp.wait())
        for e in range(E):
            fill(e, rows_pad - S).start()
        for e in range(E):
            fill(e, rows_pad - S).wait()


def _route_gather(sel, x1, aff, off, rounds, total, rows_pad):
    E, n = sel.shape
    D = x1.shape[1]
    Tb, S = ROUTE_BLOCK, ROUTE_SLOTS
    nblk = n // Tb
    return pl.pallas_call(
        functools.partial(_gather_body, E=E, nblk=nblk, cap=EC_CAPACITY * n // E),
        grid_spec=pltpu.PrefetchScalarGridSpec(
            num_scalar_prefetch=3,
            grid=(nblk,),
            in_specs=[pl.BlockSpec((E, Tb), lambda j, *_: (0, j)),
                      pl.BlockSpec((Tb, D), lambda j, *_: (j, 0)),
                      pl.BlockSpec((Tb, E), lambda j, *_: (j, 0))],
            out_specs=pl.BlockSpec(memory_space=pl.ANY),
            scratch_shapes=[pltpu.VMEM((E * S, Tb), bf16),
                            pltpu.VMEM((E * S, D + GATE_LANES), f32),
                            pltpu.SemaphoreType.DMA(())]),
        out_shape=jax.ShapeDtypeStruct((E, rows_pad, D + GATE_LANES), f32),
        compiler_params=_cparams("arbitrary"),
        name="route_gather",
    )(off, rounds, total, sel, x1, aff)


def _ffn_body(total_ref, xe_ref, w1_ref, w3_ref, w2_ref, ye_ref):
    e, r = pl.program_id(0), pl.program_id(1)
    tr = xe_ref.shape[1]
    D = ye_ref.shape[2]

    @pl.when(r * tr < total_ref[e])
    def _():
        xe = xe_ref[0, :, :D].astype(bf16)
        gate = xe_ref[0, :, D:D + 1]
        h1 = jnp.dot(xe, w1_ref[0], preferred_element_type=f32)
        h3 = jnp.dot(xe, w3_ref[0], preferred_element_type=f32)
        h = (jax.nn.silu(h1) * h3).astype(bf16)
        ye = jnp.dot(h, w2_ref[0], preferred_element_type=f32) * gate
        row = r * tr + lax.broadcasted_iota(jnp.int32, ye.shape, 0)
        ye_ref[0] = jnp.where(row < total_ref[e], ye, 0.0)

    @pl.when(r * tr >= total_ref[e])
    def _():
        ye_ref[...] = jnp.zeros_like(ye_ref)


def _ffn(total, xe, w1, w3, w2):
    E, rows_pad, Dx = xe.shape
    D, F = w1.shape[1:]
    tr = ROW_TILE
    last = lambda e, tot: (tot[e] - 1) // tr
    rowmap = lambda e, r, tot: (e, jnp.minimum(r, last(e, tot)), 0)
    wmap = lambda e, r, tot: (e, 0, 0)
    return pl.pallas_call(
        _ffn_body,
        grid_spec=pltpu.PrefetchScalarGridSpec(
            num_scalar_prefetch=1,
            grid=(E, rows_pad // tr),
            in_specs=[pl.BlockSpec((1, tr, Dx), rowmap),
                      pl.BlockSpec((1, D, F), wmap), pl.BlockSpec((1, D, F), wmap),
                      pl.BlockSpec((1, F, D), wmap)],
            out_specs=pl.BlockSpec((1, tr, D), lambda e, r, tot: (e, r, 0))),
        out_shape=jax.ShapeDtypeStruct((E, rows_pad, D), f32),
        compiler_params=_cparams("arbitrary", "arbitrary"),
        name="expert_ffn",
    )(total, xe, w1, w3, w2)


def _scatter_copies(ye_hbm, buf_ref, sem, off_ref, j, first_slot, E, S, nblk):
    copies = []
    for e in range(E):
        row = pl.multiple_of(off_ref[e * nblk + j] + first_slot, ROW_ALIGN)
        copies.append(pltpu.make_async_copy(ye_hbm.at[e, pl.ds(row, S)], buf_ref.at[e * S:(e + 1) * S], sem))
    return copies


def _scatter_body(off_ref, rounds_ref, sel_ref, x1_ref, g_ref, b_ref, ye_hbm, o_ref, p_ref, buf_ref, sem,
                  *, E, nblk, alpha):
    j = pl.program_id(0)
    S = p_ref.shape[0] // E
    slot = lax.rem(j, 2)

    def fetch(jj, first_slot, s):
        return _scatter_copies(ye_hbm, buf_ref.at[s], sem.at[s], off_ref, jj, first_slot, E, S, nblk)

    @pl.when(j == 0)
    def _():
        for cp in fetch(j, 0, 0):
            cp.start()

    @pl.when(j + 1 < nblk)
    def _():
        for cp in fetch(j + 1, 0, 1 - slot):
            cp.start()

    def one_round(r, s):
        count = _slot_onehot(sel_ref[...], p_ref, r * S, E)
        left = count - (r * S).astype(f32)
        srow = lax.broadcasted_iota(jnp.int32, (S, 1), 0).astype(f32)
        ye = buf_ref[s]
        parts = [jnp.where(srow < left[e:e + 1], ye[e * S:(e + 1) * S], 0.0) for e in range(E)]
        ye = jnp.concatenate(parts, axis=0).astype(bf16)
        return lax.dot_general(p_ref[...], ye, (((0,), (0,)), ((), ())), preferred_element_type=f32)

    for cp in fetch(j, 0, slot):
        cp.wait()
    moe = one_round(jnp.int32(0), slot)

    def extra(r, acc):
        for cp in fetch(j, r * S, 2):
            cp.start()
        for cp in fetch(j, r * S, 2):
            cp.wait()
        return acc + one_round(r, 2)

    moe = lax.fori_loop(1, rounds_ref[j], extra, moe)
    o_ref[...] = _layer_norm(alpha * x1_ref[...] + moe, g_ref[...], b_ref[...])


def _route_scatter(sel, x1, ye, off, rounds, ln_g, ln_b, alpha):
    E, n = sel.shape
    D = x1.shape[1]
    Tb, S = ROUTE_BLOCK, ROUTE_SLOTS
    nblk = n // Tb
    return pl.pallas_call(
        functools.partial(_scatter_body, E=E, nblk=nblk, alpha=alpha),
        grid_spec=pltpu.PrefetchScalarGridSpec(
            num_scalar_prefetch=2,
            grid=(nblk,),
            in_specs=[pl.BlockSpec((E, Tb), lambda j, *_: (0, j)),
                      pl.BlockSpec((Tb, D), lambda j, *_: (j, 0)),
                      pl.BlockSpec((1, D), lambda j, *_: (0, 0)),
                      pl.BlockSpec((1, D), lambda j, *_: (0, 0)),
                      pl.BlockSpec(memory_space=pl.ANY)],
            out_specs=pl.BlockSpec((Tb, D), lambda j, *_: (j, 0)),
            scratch_shapes=[pltpu.VMEM((E * S, Tb), bf16),
                            pltpu.VMEM((3, E * S, D), f32),
                            pltpu.SemaphoreType.DMA((3,))]),
        out_shape=jax.ShapeDtypeStruct((n, D), f32),
        compiler_params=_cparams("arbitrary"),
        name="route_scatter",
    )(off, rounds, sel, x1, ln_g, ln_b, ye)


def _expert_choice(x1, aff, w1, w3, w2, ln_g, ln_b, alpha):
    n, D = x1.shape
    E = aff.shape[1]
    cap = EC_CAPACITY * n // E
    Tb, S = ROUTE_BLOCK, ROUTE_SLOTS
    nblk = n // Tb
    sel = _select(aff.T, cap)
    cnt = sel.reshape(E, nblk, Tb).sum(-1).astype(jnp.int32)
    cnt_al = (cnt + ROW_ALIGN - 1) // ROW_ALIGN * ROW_ALIGN
    end = jnp.cumsum(cnt_al, axis=1)
    off = (end - cnt_al).reshape(E * nblk)
    total = end[:, -1]
    rounds = jnp.maximum((jnp.max(cnt, axis=0) + S - 1) // S, 1)
    rows_pad = -(-(cap + ROW_ALIGN * nblk + S) // ROW_TILE) * ROW_TILE
    xe = _route_gather(sel, x1, aff, off, rounds, total, rows_pad)
    ye = _ffn(total, xe, w1, w3, w2)
    return _route_scatter(sel, x1, ye, off, rounds, ln_g, ln_b, alpha)


FILT_FEAT_PAD = 32


def _lag_features(L):
    n = np.arange(2 * L, dtype=np.float64)
    pos = np.where(n < L, n, 2 * L - n)
    t = pos / (L - 1)
    bands = np.linspace(1e-4, FILT_BANDS - 1, FILT_BANDS)
    ang = 2.0 * math.pi * (pos / L)[:, None] * bands[None, :]
    z = np.concatenate([t[:, None], np.cos(ang), -np.sin(ang)], axis=-1)
    out = np.zeros((2 * L, FILT_FEAT_PAD), np.float32)
    out[:, :z.shape[1]] = z
    return out


def _filter_body(z_ref, w1_ref, b1_ref, f1_ref, w2_ref, b2_ref, f2_ref, w3_ref, dec_ref, o_ref, *, L):
    hp = lax.Precision.HIGHEST
    z = z_ref[...]
    h = jnp.sin(f1_ref[...] * (jnp.dot(z, w1_ref[...], precision=hp, preferred_element_type=f32) + b1_ref[...]))
    h = jnp.sin(f2_ref[...] * (jnp.dot(h, w2_ref[...], precision=hp, preferred_element_type=f32) + b2_ref[...]))
    h = jnp.dot(h, w3_ref[...], precision=hp, preferred_element_type=f32)
    window = jnp.exp(-z[:, 0:1] * jnp.abs(dec_ref[...])) + FILT_SHIFT
    tl = z.shape[0]
    row = pl.program_id(0) * tl + lax.broadcasted_iota(jnp.int32, h.shape, 0)
    o_ref[...] = jnp.where(row == L, 0.0, h * window)


def _circ_filters(L, w1, b1, f1, w2, b2, f2, w3, decay):
    nh = w1.shape[1]
    oc = w3.shape[1] // 2
    feats = jnp.asarray(_lag_features(L))
    w1p = jnp.zeros((FILT_FEAT_PAD, nh), f32).at[:w1.shape[0]].set(w1)
    tl = min(ROW_TILE, L)
    half = L // tl
    dirsel = lambda i: (i // half, 0, 0)
    return pl.pallas_call(
        functools.partial(_filter_body, L=L),
        grid=(2 * L // tl,),
        in_specs=[pl.BlockSpec((tl, FILT_FEAT_PAD), lambda i: (i, 0)),
                  _const_spec((FILT_FEAT_PAD, nh)), _const_spec((1, nh)), _const_spec((1, nh)),
                  _const_spec((nh, nh)), _const_spec((1, nh)), _const_spec((1, nh)),
                  pl.BlockSpec((None, nh, oc), dirsel),
                  pl.BlockSpec((None, 1, oc), dirsel)],
        out_specs=pl.BlockSpec((tl, oc), lambda i: (i, 0)),
        out_shape=jax.ShapeDtypeStruct((2 * L, oc), f32),
        compiler_params=_cparams("parallel"),
        name="hyena_filter",
    )(feats, w1p, b1.reshape(1, nh), f1.reshape(1, nh), w2, b2.reshape(1, nh), f2.reshape(1, nh),
      w3.reshape(nh, 2, oc).transpose(1, 0, 2), decay.reshape(2, 1, oc))


DFT_N2 = 128
FREQ_CHAINS = 8
COL_TILE = 4096


def _dft_tables(L):
    N, N2 = 2 * L, DFT_N2
    N1 = N // N2
    H1 = N1 // 2
    K1n = H1 + 1
    K1p = -(-K1n // 8) * 8
    k1 = np.arange(K1n)
    n1 = np.arange(N1)
    ang = 2.0 * math.pi * ((k1[:, None] * n1[None, :]) % N1) / N1
    f1 = np.zeros((2 * K1p, N1))
    f1[:K1n] = np.cos(ang)
    f1[K1p:K1p + K1n] = -np.sin(ang)
    wgt = np.full(K1n, 2.0)
    wgt[0] = wgt[H1] = 1.0
    ang = 2.0 * math.pi * ((np.arange(H1)[:, None] * k1[None, :]) % N1) / N1
    if1 = np.zeros((H1, 2 * K1p))
    if1[:, :K1n] = wgt * np.cos(ang) / N
    if1[:, K1p:K1p + K1n] = -wgt * np.sin(ang) / N
    k2 = np.arange(N2)
    ang = 2.0 * math.pi * ((k2[:, None] * k2[None, :]) % N2) / N2
    f2r, f2i = np.cos(ang), -np.sin(ang)
    ang = 2.0 * math.pi * (k1[:, None] * k2[None, :]) / N
    twr = np.zeros((K1p, 1, N2))
    twi = np.zeros((K1p, 1, N2))
    twr[:K1n, 0], twi[:K1n, 0] = np.cos(ang), -np.sin(ang)
    c = lambda a, dt: jnp.asarray(a.astype(np.float32)).astype(dt)
    return dict(N1=N1, H1=H1, K1p=K1p, f1=c(f1, bf16), if1=c(if1, bf16), f2r=c(f2r, f32), f2i=c(f2i, f32),
                twr=c(twr, f32), twi=c(twi, f32))


def _lmat_body(w_ref, x_ref, o_ref):
    o_ref[0] = jnp.dot(w_ref[...], x_ref[0].astype(bf16), preferred_element_type=f32).astype(o_ref.dtype)


def _dft_stage1(w, x):
    Bx, Kd, cols = x.shape
    Mo = w.shape[0]
    tc = min(COL_TILE, cols)
    return pl.pallas_call(
        _lmat_body,
        grid=(Bx, cols // tc),
        in_specs=[_const_spec((Mo, Kd)), pl.BlockSpec((1, Kd, tc), lambda b, j: (b, 0, j))],
        out_specs=pl.BlockSpec((1, Mo, tc), lambda b, j: (b, 0, j)),
        out_shape=jax.ShapeDtypeStruct((Bx, Mo, cols), bf16),
        compiler_params=_cparams("parallel", "parallel"),
        name="dft_stage1",
    )(w, x)


def _build_stage2_matrix(f2r_ref, f2i_ref, twr_ref, twi_ref, m_ref, mt_ref):
    n2 = f2r_ref.shape[0]
    twr, twi = twr_ref[...], twi_ref[...]
    re = f2r_ref[...] * twr - f2i_ref[...] * twi
    im = f2r_ref[...] * twi + f2i_ref[...] * twr
    m_ref[:n2, :n2] = re.astype(bf16)
    m_ref[:n2, n2:] = (-im).astype(bf16)
    m_ref[n2:, :n2] = im.astype(bf16)
    m_ref[n2:, n2:] = re.astype(bf16)
    if mt_ref is not None:
        ret, imt = re.T, im.T
        mt_ref[:n2, :n2] = ret.astype(bf16)
        mt_ref[:n2, n2:] = imt.astype(bf16)
        mt_ref[n2:, :n2] = (-imt).astype(bf16)
        mt_ref[n2:, n2:] = ret.astype(bf16)


def _spectrum_body(f2r_ref, f2i_ref, twr_ref, twi_ref, a_ref, x_ref, m_ref):
    _build_stage2_matrix(f2r_ref, f2i_ref, twr_ref, twi_ref, m_ref, None)
    n2, cw = a_ref.shape[1:]
    x = jnp.dot(m_ref[...], a_ref[...].reshape(2 * n2, cw), preferred_element_type=f32)
    x_ref[...] = x.reshape(2, n2, cw)


def _filter_spectrum(tab, a5):
    _, _, K1p, N2, Cw = a5.shape
    tw_spec = pl.BlockSpec((None, 1, N2), lambda k: (k, 0, 0))
    return pl.pallas_call(
        _spectrum_body,
        grid=(K1p,),
        in_specs=[_const_spec((N2, N2)), _const_spec((N2, N2)), tw_spec, tw_spec,
                  pl.BlockSpec((None, 2, None, N2, Cw), lambda k: (0, 0, k, 0, 0))],
        out_specs=pl.BlockSpec((None, 2, N2, Cw), lambda k: (k, 0, 0, 0)),
        out_shape=jax.ShapeDtypeStruct((K1p, 2, N2, Cw), f32),
        scratch_shapes=[pltpu.VMEM((2 * N2, 2 * N2), bf16)],
        compiler_params=_cparams("arbitrary"),
        name="filter_spectrum",
    )(tab['f2r'], tab['f2i'], tab['twr'], tab['twi'], a5)


def _freq_body(f2r_ref, f2i_ref, twr_ref, twi_ref, a_ref, h_ref, g_ref, m_ref, mt_ref):
    nb, _, nk, n2, c = a_ref.shape
    for kk in range(nk):
        _build_stage2_matrix(f2r_ref, f2i_ref, twr_ref.at[kk], twi_ref.at[kk], m_ref.at[kk], mt_ref.at[kk])
        hr, hi = h_ref[kk, 0], h_ref[kk, 1]
        for b in range(nb):
            x = jnp.dot(m_ref[kk], a_ref[b, :, kk].reshape(2 * n2, c), preferred_element_type=f32)
            xr, xi = x[:n2], x[n2:]
            y = jnp.concatenate([xr * hr - xi * hi, xr * hi + xi * hr], axis=0).astype(bf16)
            g = jnp.dot(mt_ref[kk], y, preferred_element_type=f32)
            g_ref[b, :, kk] = g.reshape(2, n2, c).astype(bf16)


def _freq_stage(tab, a5, hspec, order):
    B, _, K1p, N2, C = a5.shape
    nb = min(B, FREQ_CHAINS)
    nk = FREQ_CHAINS // nb
    tw_spec = pl.BlockSpec((nk, 1, N2), lambda k, b: (k, 0, 0))
    slab = pl.BlockSpec((nb, 2, nk, N2, C), lambda k, b: (b, 0, k, 0, 0))
    return pl.pallas_call(
        _freq_body,
        grid=(K1p // nk, B // nb),
        in_specs=[_const_spec((N2, N2)), _const_spec((N2, N2)), tw_spec, tw_spec, slab,
                  pl.BlockSpec((nk, 2, N2, C), lambda k, b: (k, 0, 0, order))],
        out_specs=slab,
        out_shape=jax.ShapeDtypeStruct(a5.shape, bf16),
        scratch_shapes=[pltpu.VMEM((nk, 2 * N2, 2 * N2), bf16), pltpu.VMEM((nk, 2 * N2, 2 * N2), bf16)],
        compiler_params=_cparams("parallel", "parallel"),
        name="freq_stage",
    )(tab['f2r'], tab['f2i'], tab['twr'], tab['twi'], a5, hspec)


DFT_SLABS = 16


def _slabs_to_matrix(load_tile, rows, S, C, stage_ref):
    for lt in range(C // LANE):
        stage_ref[lt, :rows * S] = load_tile(lt)
    cols = [stage_ref[lt, pl.ds(sl, rows, stride=S), :] for sl in range(S) for lt in range(C // LANE)]
    return jnp.concatenate(cols, axis=1)


def _matrix_to_slabs(mat, rows, S, C, stage_ref):
    for sl in range(S):
        for lt in range(C // LANE):
            lo = sl * C + lt * LANE
            stage_ref[lt, pl.ds(sl, rows, stride=S), :] = mat[:, lo:lo + LANE]
    return [stage_ref[lt, :rows * S].reshape(rows, S, LANE) for lt in range(C // LANE)]


def _stage1_body(w_ref, z_ref, o_ref, stage_ref):
    H1, S, C = z_ref.shape
    K1p = o_ref.shape[1]
    zmat = _slabs_to_matrix(lambda lt: z_ref[:, :, lt * LANE:(lt + 1) * LANE].reshape(H1 * S, LANE), H1, S, C, stage_ref)
    a = jnp.dot(w_ref[...], zmat.astype(bf16), preferred_element_type=f32)
    for lt, tile in enumerate(_matrix_to_slabs(a, 2 * K1p, S, C, stage_ref)):
        o_ref[0, :, :, lt * LANE:(lt + 1) * LANE] = tile[:K1p].astype(bf16)
        o_ref[1, :, :, lt * LANE:(lt + 1) * LANE] = tile[K1p:].astype(bf16)


def _data_stage1(w, z4, K1p):
    B, H1, N2, C = z4.shape
    S = DFT_SLABS
    return pl.pallas_call(
        _stage1_body,
        grid=(B, N2 // S),
        in_specs=[_const_spec(w.shape), pl.BlockSpec((None, H1, S, C), lambda b, j: (b, 0, j, 0))],
        out_specs=pl.BlockSpec((None, 2, K1p, S, C), lambda b, j: (b, 0, 0, j, 0)),
        out_shape=jax.ShapeDtypeStruct((B, 2, K1p, N2, C), bf16),
        scratch_shapes=[pltpu.VMEM((C // LANE, 2 * K1p * S, LANE), f32)],
        compiler_params=_cparams("parallel", "parallel"),
        name="dft_data_stage1",
    )(w, z4)


def _conv_out_body(w_ref, g_ref, z_ref, gate_ref, bias_ref, o_ref, stage_ref):
    _, K1p, S, C = g_ref.shape
    H1 = z_ref.shape[0]

    def load_tile(lt):
        parts = [g_ref[ri, :, :, lt * LANE:(lt + 1) * LANE].astype(f32).reshape(K1p * S, LANE) for ri in range(2)]
        return jnp.concatenate(parts, axis=0)

    gmat = _slabs_to_matrix(load_tile, 2 * K1p, S, C, stage_ref)
    y = jnp.dot(w_ref[...], gmat.astype(bf16), preferred_element_type=f32)
    for lt, tile in enumerate(_matrix_to_slabs(y, H1, S, C, stage_ref)):
        sl = slice(lt * LANE, (lt + 1) * LANE)
        o_ref[:, :, sl] = gate_ref[:, :, sl] * (tile + z_ref[:, :, sl] * bias_ref[:, sl].reshape(1, 1, LANE))


def _conv_out(w, g5, z4, gate4, bias):
    B, H1, N2, C = z4.shape
    K1p = g5.shape[2]
    S = DFT_SLABS
    blk = pl.BlockSpec((None, H1, S, C), lambda b, j: (b, 0, j, 0))
    return pl.pallas_call(
        _conv_out_body,
        grid=(B, N2 // S),
        in_specs=[_const_spec(w.shape), pl.BlockSpec((None, 2, K1p, S, C), lambda b, j: (b, 0, 0, j, 0)), blk, blk,
                  _const_spec((1, C))],
        out_specs=blk,
        out_shape=jax.ShapeDtypeStruct(z4.shape, f32),
        scratch_shapes=[pltpu.VMEM((C // LANE, 2 * K1p * S, LANE), f32)],
        compiler_params=_cparams("parallel", "parallel"),
        name="conv_out",
    )(w, g5, z4, gate4, bias)


def _long_conv(v, g1, g2, circ, hy_bias, B, L, C):
    tab = _dft_tables(L)
    N2, N1, H1, K1p = DFT_N2, tab['N1'], tab['H1'], tab['K1p']
    Cw = circ.shape[1]
    a5 = _dft_stage1(tab['f1'], circ.reshape(1, N1, N2 * Cw)).reshape(1, 2, K1p, N2, Cw)
    hspec = _filter_spectrum(tab, a5)
    z4 = v.reshape(B, H1, N2, C)
    gates = (g1.reshape(B, H1, N2, C), g2.reshape(B, H1, N2, C))
    f1h = tab['f1'][:, :H1]
    for o in range(HY_ORDER):
        g5 = _freq_stage(tab, _data_stage1(f1h, z4, K1p), hspec, o)
        z4 = _conv_out(tab['if1'], g5, z4, gates[o], hy_bias[o].reshape(1, C))
    return z4.reshape(B * L, C)


def _s5_rows(B, L):
    return (True, S5_ROWS, L // S5_ROWS) if B == 1 else (False, B, L)


def _encoder_layer(x, p, s5_mats, alpha):
    B, L, D = x.shape
    n = B * L
    x2 = x.reshape(n, D)
    G, P, H = p['s5_b_re'].shape[1:]
    s5w = G * H
    hyw = p['w_hy_proj'].shape[0]

    u_g, v, g1, g2, g_s, g_h = _inproj(x2, p['w_in'].astype(bf16), p['hy_short_w'], p['hy_short_b'], B, L, G, H, hyw)

    chained, Bs, _ = _s5_rows(B, L)
    y_g = _s5(u_g, s5_mats, Bs, chained)

    circ = _circ_filters(L, p['filt_w1'], p['filt_b1'], p['filt_freq1'], p['filt_w2'], p['filt_b2'],
                         p['filt_freq2'], p['filt_w3'], p['filt_decay'])
    z = _long_conv(v, g1, g2, circ, p['hy_bias'], B, L, hyw)

    x1, aff = _merge(x2, y_g, z, g_s, g_h, p['s5_w_glu'].astype(bf16), p['w_s5_proj'].astype(bf16),
                     p['w_hy_proj'].astype(bf16), p['w_out'].astype(bf16),
                     p['ln1_g'].reshape(1, D), p['ln1_b'].reshape(1, D), p['w_router'], B, L, alpha)

    out = _expert_choice(x1, aff, p['ex_w1_bf16'], p['ex_w3_bf16'], p['ex_w2_bf16'],
                         p['ln2_g'].reshape(1, D), p['ln2_b'].reshape(1, D), alpha)
    return out.reshape(B, L, D)


_PARAM_NAMES = ('w_in', 's5_lambda_re', 's5_lambda_im', 's5_log_dt', 's5_b_re', 's5_b_im',
                's5_c_re', 's5_c_im', 's5_d', 's5_w_glu', 'w_s5_proj',
                'hy_short_w', 'hy_short_b', 'filt_w1', 'filt_b1', 'filt_freq1', 'filt_w2', 'filt_b2',
                'filt_freq2', 'filt_w3', 'filt_decay', 'hy_bias', 'w_hy_proj', 'w_out',
                'ln1_g', 'ln1_b', 'w_router', 'ex_w1', 'ex_w3', 'ex_w2', 'ln2_g', 'ln2_b')


def kernel(x_prompt, x_sample, w_in, s5_lambda_re, s5_lambda_im, s5_log_dt, s5_b_re, s5_b_im, s5_c_re, s5_c_im, s5_d, s5_w_glu, w_s5_proj, hy_short_w, hy_short_b, filt_w1, filt_b1, filt_freq1, filt_w2, filt_b2, filt_freq2, filt_w3, filt_decay, hy_bias, w_hy_proj, w_out, ln1_g, ln1_b, w_router, ex_w1, ex_w3, ex_w2, ln2_g, ln2_b):
    stacked = (w_in, s5_lambda_re, s5_lambda_im, s5_log_dt, s5_b_re, s5_b_im, s5_c_re, s5_c_im, s5_d,
               s5_w_glu, w_s5_proj, hy_short_w, hy_short_b, filt_w1, filt_b1, filt_freq1, filt_w2,
               filt_b2, filt_freq2, filt_w3, filt_decay, hy_bias, w_hy_proj, w_out, ln1_g, ln1_b,
               w_router, ex_w1, ex_w3, ex_w2, ln2_g, ln2_b)
    depth = w_in.shape[0]
    alpha = (2.0 * depth) ** 0.25
    xs = [x_prompt, x_sample]
    for l in range(depth):
        p = {k: v[l] for k, v in zip(_PARAM_NAMES, stacked)}
        for k in ('ex_w1', 'ex_w3', 'ex_w2'):
            p[k + '_bf16'] = p[k].astype(bf16)
        s5_mats = _s5_matrices(p, [_s5_rows(x.shape[0], x.shape[1])[2] for x in xs])
        xs = [_encoder_layer(x, p, mats, alpha) for x, mats in zip(xs, s5_mats)]
    return tuple(xs)
```

```python
import functools
import math

import jax
import jax.numpy as jnp
import numpy as np
from jax import lax
from jax.experimental import pallas as pl
from jax.experimental.pallas import tpu as pltpu

FILT_BANDS = 8
FILT_SHIFT = 0.05
EC_CAPACITY = 2
LN_EPS = 1e-5
HY_ORDER = 2

VMEM_LIMIT_BYTES = 56 * 1024 * 1024
ROW_TILE = 512

bf16 = jnp.bfloat16
f32 = jnp.float32


def _cparams(*sem):
    return pltpu.CompilerParams(dimension_semantics=sem, vmem_limit_bytes=VMEM_LIMIT_BYTES)


def _const_spec(shape):
    return pl.BlockSpec(shape, lambda *_: (0,) * len(shape))


def _inproj_body(xp_ref, x_ref, xn_ref, w_ref, cw_ref, cb_ref, ug_ref, v_ref, g1_ref, g2_ref, gs_ref, gh_ref,
                 stage_ref, *, o1, o2, o3, nt, H):
    i = pl.program_id(1)
    proj = jnp.dot(x_ref[...].astype(bf16), w_ref[...], preferred_element_type=f32)
    _rows_to_chunks(proj[:, :o1], stage_ref, ug_ref, H)
    gs_ref[...] = jax.nn.sigmoid(proj[:, o2:o3])
    gh_ref[...] = jax.nn.sigmoid(proj[:, o3:])
    u = proj[:, o1:o2]
    w_hy = w_ref[:, o1:o2]
    prev_row = jnp.dot(xp_ref[...].astype(bf16), w_hy, preferred_element_type=f32)[7:8]
    next_row = jnp.dot(xn_ref[...].astype(bf16), w_hy, preferred_element_type=f32)[0:1]
    prev_row = jnp.where(i == 0, 0.0, prev_row)
    next_row = jnp.where(i == nt - 1, 0.0, next_row)
    tl = u.shape[0]
    rows = lax.broadcasted_iota(jnp.int32, u.shape, 0)
    um1 = jnp.where(rows == 0, prev_row, pltpu.roll(u, 1, axis=0))
    up1 = jnp.where(rows == tl - 1, next_row, pltpu.roll(u, tl - 1, axis=0))
    cw = cw_ref[...]
    hy = um1 * cw[0:1, :] + u * cw[1:2, :] + up1 * cw[2:3, :] + cb_ref[...]
    hyw = v_ref.shape[1]
    v_ref[...] = hy[:, :hyw]
    g1_ref[...] = hy[:, hyw:2 * hyw]
    g2_ref[...] = hy[:, 2 * hyw:]


def _inproj(x2, w_in, conv_w, conv_b, B, L, G, H, hyw):
    n, D = x2.shape
    cols = w_in.shape[1]
    s5w = G * H
    o1, o2 = s5w, s5w + 3 * hyw
    o3 = o2 + D
    tm = min(ROW_TILE, L)
    nt = L // tm
    r8 = tm // 8
    row = lambda b, i: (b * nt + i, 0)
    prev = lambda b, i: (jnp.maximum((b * nt + i) * r8 - 1, 0), 0)
    nxt = lambda b, i: (jnp.minimum((b * nt + i + 1) * r8, n // 8 - 1), 0)
    hy_out = jax.ShapeDtypeStruct((n, hyw), f32)
    return pl.pallas_call(
        functools.partial(_inproj_body, o1=o1, o2=o2, o3=o3, nt=nt, H=H),
        grid=(B, nt),
        in_specs=[pl.BlockSpec((8, D), prev), pl.BlockSpec((tm, D), row), pl.BlockSpec((8, D), nxt),
                  _const_spec((D, cols)), _const_spec((3, 3 * hyw)), _const_spec((1, 3 * hyw))],
        out_specs=[pl.BlockSpec((G, tm // S5_CHUNK, S5_CHUNK * H), lambda b, i: (0, b * nt + i, 0)),
                   pl.BlockSpec((tm, hyw), row), pl.BlockSpec((tm, hyw), row), pl.BlockSpec((tm, hyw), row),
                   pl.BlockSpec((tm, D), row), pl.BlockSpec((tm, D), row)],
        out_shape=[jax.ShapeDtypeStruct((G, n // S5_CHUNK, S5_CHUNK * H), bf16), hy_out, hy_out, hy_out,
                   jax.ShapeDtypeStruct((n, D), f32), jax.ShapeDtypeStruct((n, D), f32)],
        scratch_shapes=[pltpu.VMEM((s5w // LANE, tm, LANE), f32)],
        compiler_params=_cparams("parallel", "parallel"),
        name="inproj",
    )(x2, x2, x2, w_in, conv_w, conv_b.reshape(1, 3 * hyw))


S5_CHUNK = 16
S5_ROWS = 8
LANE = 128


def _s5_body(u_ref, kmat_ref, winf_ref, winb_ref, woutf_ref, woutb_ref, af_ref, ab_ref, y_ref,
             vf_ref, vb_ref, sf_ref, sb_ref, *, R, nc, chained):
    u = u_ref[0]
    hw = sf_ref.shape[1]
    seq_rows = lambda b: pl.ds(b, nc, stride=R)
    for v_ref, w_ref in ((vf_ref, winf_ref), (vb_ref, winb_ref)):
        v = jnp.dot(u, w_ref[0], preferred_element_type=f32)
        for b in range(R):
            v_ref[0, seq_rows(b), :] = v[b * nc:(b + 1) * nc, :hw]
            v_ref[1, seq_rows(b), :] = v[b * nc:(b + 1) * nc, hw:]
    bc = lambda ref, row, lo: jnp.broadcast_to(ref[0, row:row + 1, lo:lo + hw], (R, hw))
    decay = lambda ref, row: ((bc(ref, row, 0), bc(ref, row, hw)), (bc(ref, row + 1, 0), bc(ref, row + 1, hw)))
    f_a1, f_a2 = decay(af_ref, 0)
    b_a1, b_a2 = decay(ab_ref, 0)

    def mul_add(s0, s1, a1, a2, v0, v1):
        return a1[0] * s0 + a2[0] * s1 + v0, a1[1] * s1 + a2[1] * s0 + v1

    def scan(init, store):
        def step(i, carry):
            f0, f1, b0, b1 = carry
            rf = pl.ds(pl.multiple_of(i * R, R), R)
            rb = pl.ds(pl.multiple_of((nc - 1 - i) * R, R), R)
            if store:
                sf_ref[rf, :] = f0
                sb_ref[rb, :] = b0
            f0, f1 = mul_add(f0, f1, f_a1, f_a2, vf_ref[0, rf, :], vf_ref[1, rf, :])
            b0, b1 = mul_add(b0, b1, b_a1, b_a2, vb_ref[0, rb, :], vb_ref[1, rb, :])
            return f0, f1, b0, b1
        return lax.fori_loop(0, nc, step, init, unroll=4)

    zero = jnp.zeros((R, hw), f32)
    init = (zero, zero, zero, zero)
    if chained:
        ef0, ef1, eb0, eb1 = scan(init, store=False)
        fn_a1, fn_a2 = decay(af_ref, 2)
        bn_a1, bn_a2 = decay(ab_ref, 2)
        row = lax.broadcasted_iota(jnp.int32, (R, hw), 0)
        down = lambda v: jnp.where(row == 0, 0.0, pltpu.roll(v, 1, axis=0))
        up = lambda v: jnp.where(row == R - 1, 0.0, pltpu.roll(v, R - 1, axis=0))
        f0, f1, b0, b1 = init
        for _ in range(R - 1):
            t0, t1 = mul_add(f0, f1, fn_a1, fn_a2, ef0, ef1)
            f0, f1 = down(t0), down(t1)
            t0, t1 = mul_add(b0, b1, bn_a1, bn_a2, eb0, eb1)
            b0, b1 = up(t0), up(t1)
        init = (f0, f1, b0, b1)
    scan(init, store=True)
    y = jnp.dot(u, kmat_ref[0], preferred_element_type=f32)
    for s_ref, w_ref in ((sf_ref, woutf_ref), (sb_ref, woutb_ref)):
        states = jnp.concatenate([s_ref[seq_rows(b), :] for b in range(R)], axis=0)
        y = y + jnp.dot(states.astype(bf16), w_ref[0], preferred_element_type=f32)
    y_ref[0] = y


def _s5(u_g, mats, R, chained):
    G, M, W = u_g.shape
    kmat, winf, winb, woutf, woutb, af, ab = mats
    hw = woutf.shape[1]
    grp = lambda shape: pl.BlockSpec((1,) + shape, lambda g: (g, 0, 0))
    return pl.pallas_call(
        functools.partial(_s5_body, R=R, nc=M // R, chained=chained),
        grid=(G,),
        in_specs=[grp((M, W)), grp((W, W)), grp((W, 2 * hw)), grp((W, 2 * hw)), grp((hw, W)), grp((hw, W)),
                  grp((4, 2 * hw)), grp((4, 2 * hw))],
        out_specs=grp((M, W)),
        out_shape=jax.ShapeDtypeStruct((G, M, W), f32),
        scratch_shapes=[pltpu.VMEM((2, M, hw), f32), pltpu.VMEM((2, M, hw), f32),
                        pltpu.VMEM((M, hw), f32), pltpu.VMEM((M, hw), f32)],
        compiler_params=_cparams("parallel"),
        name="s5_chunked",
    )(u_g, kmat, winf, winb, woutf, woutb, af, ab)


def _rows_to_chunks(x, stage_ref, o_ref, H):
    Tc = S5_CHUNK
    nr = x.shape[0] // Tc
    gpt, per_tile = LANE // H, LANE // H
    for lt in range(x.shape[1] // LANE):
        stage_ref[lt] = x[:, lt * LANE:(lt + 1) * LANE]
        rows_t = [stage_ref[lt, pl.ds(t, nr, stride=Tc), :] for t in range(Tc)]
        for g in range(gpt):
            for j in range(Tc // per_tile):
                tile = jnp.concatenate([rows_t[per_tile * j + k][:, H * g:H * (g + 1)] for k in range(per_tile)], axis=1)
                o_ref[lt * gpt + g, :, j * LANE:(j + 1) * LANE] = tile.astype(o_ref.dtype)


def _chunks_to_rows(y_ref, stage_ref, H):
    Tc = S5_CHUNK
    nr = y_ref.shape[1]
    gpt = LANE // H
    for lt in range(stage_ref.shape[0]):
        for t in range(Tc):
            tile = jnp.concatenate([y_ref[lt * gpt + g, :, H * t:H * (t + 1)] for g in range(gpt)], axis=1)
            stage_ref[lt, pl.ds(t, nr, stride=Tc), :] = tile
    return jnp.concatenate([stage_ref[lt] for lt in range(stage_ref.shape[0])], axis=1)


def _s5_discretise(lam_re, lam_im, log_dt, b_re, b_im, powers):
    lam = lax.complex(-jnp.abs(lam_re.astype(f32)), lam_im.astype(f32))
    dt = jnp.exp(log_dt.astype(f32))[:, None]
    a_bar = jnp.exp(lam * dt)
    k = jnp.asarray(powers, f32)[None, :, None]
    apow = jnp.exp((lam * dt)[:, None, :] * k)
    b_bar = ((a_bar - 1.0) / lam)[..., None] * lax.complex(b_re.astype(f32), b_im.astype(f32))
    return apow, b_bar


def _s5_matrices(p, seg_steps):
    Tc = S5_CHUNK
    powers = list(range(Tc + 1)) + list(seg_steps)
    apf, bbf = _s5_discretise(p['s5_lambda_re'][0], p['s5_lambda_im'][0], p['s5_log_dt'][0],
                              p['s5_b_re'][0], p['s5_b_im'][0], powers)
    apb, bbb = _s5_discretise(p['s5_lambda_re'][1], p['s5_lambda_im'][1], p['s5_log_dt'][1],
                              p['s5_b_re'][1], p['s5_b_im'][1], powers)
    c = lax.complex(p['s5_c_re'].astype(f32), p['s5_c_im'].astype(f32))
    G, H, P = c.shape
    W = Tc * H
    kf = jnp.real(jnp.einsum('ghp,gtp,gpk->gthk', c, apf[:, :Tc], bbf))
    kb = jnp.real(jnp.einsum('ghp,gtp,gpk->gthk', c, apb[:, :Tc], bbb))
    tau = jnp.arange(Tc)[None, :] - jnp.arange(Tc)[:, None]
    blk = (jnp.where((tau >= 0)[None, :, :, None, None], kf[:, jnp.abs(tau)], 0.0)
           + jnp.where((tau <= 0)[None, :, :, None, None], kb[:, jnp.abs(tau)], 0.0))
    kmat = jnp.transpose(blk, (0, 1, 4, 2, 3)).reshape(G, W, W)
    kmat = kmat + jnp.eye(W, dtype=f32)[None] * jnp.tile(p['s5_d'].astype(f32), (1, Tc))[:, None, :]
    lanes = lambda z: jnp.concatenate([jnp.real(z), jnp.imag(z), jnp.imag(z), jnp.real(z)], axis=-1)
    win = lambda ap, bb: lanes(jnp.einsum('gsp,gpk->gskp', ap, bb)).reshape(G, W, 4 * P)
    winf = win(apf[:, Tc - 1::-1][:, :Tc], bbf)
    winb = win(apb[:, :Tc], bbb)
    def wout(ap):
        z = jnp.einsum('ghp,gtp->gpth', c, ap).reshape(G, P, W)
        return jnp.concatenate([jnp.real(z), -jnp.imag(z)], axis=1)
    woutf = wout(apf[:, 1:Tc + 1])
    woutb = wout(apb[:, Tc:0:-1])
    def chunk_decay(ap, seg):
        rows = []
        for k in (Tc, Tc + 1 + seg):
            ar, ai = jnp.real(ap[:, k]), jnp.imag(ap[:, k])
            rows += [jnp.concatenate([ar, ar, ar, ar], -1), jnp.concatenate([-ai, ai, ai, -ai], -1)]
        return jnp.stack(rows, axis=1)
    cast = lambda m: m.astype(bf16)
    shared = (cast(kmat), cast(winf), cast(winb), cast(woutf), cast(woutb))
    return [shared + (chunk_decay(apf, i), chunk_decay(apb, i)) for i in range(len(seg_steps))]


MERGE_SPLIT = 2


def _layer_norm(v, g, b):
    mu = jnp.mean(v, axis=-1, keepdims=True)
    c = v - mu
    var = jnp.mean(c * c, axis=-1, keepdims=True)
    return c * lax.rsqrt(var + LN_EPS) * g + b


def _merge_body(x_ref, yg_ref, z_ref, gs_ref, gh_ref, wglu_ref, wsp_ref, whp_ref, wout_ref,
                g_ref, b_ref, wrh_ref, wrl_ref, x1_ref, aff_ref, stage_ref, *, alpha, H):
    ys_all = jax.nn.gelu(_chunks_to_rows(yg_ref, stage_ref, H))
    tm = x_ref.shape[0]
    for r0 in range(0, tm, tm // MERGE_SPLIT):
        rs = slice(r0, r0 + tm // MERGE_SPLIT)
        ys = ys_all[rs]
        gate = jax.nn.sigmoid(jnp.dot(ys.astype(bf16), wglu_ref[...], preferred_element_type=f32))
        branch_s = jnp.dot((ys * gate).astype(bf16), wsp_ref[...], preferred_element_type=f32)
        branch_h = jnp.dot(z_ref[rs, :].astype(bf16), whp_ref[...], preferred_element_type=f32)
        mix = gs_ref[rs, :] * branch_s + gh_ref[rs, :] * branch_h
        mix = jnp.dot(mix.astype(bf16), wout_ref[...], preferred_element_type=f32)
        x1 = _layer_norm(alpha * x_ref[rs, :] + mix, g_ref[...], b_ref[...])
        x1_ref[rs, :] = x1
        x1_hi = x1.astype(bf16)
        x1_lo = (x1 - x1_hi.astype(f32)).astype(bf16)
        logits = (jnp.dot(x1_hi, wrh_ref[...], preferred_element_type=f32)
                  + jnp.dot(x1_lo, wrh_ref[...], preferred_element_type=f32)
                  + jnp.dot(x1_hi, wrl_ref[...], preferred_element_type=f32))
        m = jnp.max(logits, axis=-1, keepdims=True)
        e = jnp.exp(logits - m)
        aff_ref[rs, :] = e / jnp.sum(e, axis=-1, keepdims=True)


def _merge(x2, y_g, z, g_s, g_h, w_glu, w_sp, w_hp, w_out, ln_g, ln_b, w_router, B, L, alpha):
    n, D = x2.shape
    G, _, W = y_g.shape
    H = W // S5_CHUNK
    s5w, hyw = G * H, z.shape[1]
    E = w_router.shape[1]
    tm = min(ROW_TILE, L)
    nt = L // tm
    row = lambda b, i: (b * nt + i, 0)
    wr_hi = w_router.astype(bf16)
    wr_lo = (w_router - wr_hi.astype(f32)).astype(bf16)
    return pl.pallas_call(
        functools.partial(_merge_body, alpha=alpha, H=H),
        grid=(B, nt),
        in_specs=[pl.BlockSpec((tm, D), row),
                  pl.BlockSpec((G, tm // S5_CHUNK, W), lambda b, i: (0, b * nt + i, 0)),
                  pl.BlockSpec((tm, hyw), row),
                  pl.BlockSpec((tm, D), row),
                  pl.BlockSpec((tm, D), row),
                  _const_spec(w_glu.shape), _const_spec(w_sp.shape), _const_spec(w_hp.shape),
                  _const_spec(w_out.shape), _const_spec((1, D)), _const_spec((1, D)),
                  _const_spec(w_router.shape), _const_spec(w_router.shape)],
        out_specs=[pl.BlockSpec((tm, D), row), pl.BlockSpec((tm, E), row)],
        out_shape=[jax.ShapeDtypeStruct((n, D), f32), jax.ShapeDtypeStruct((n, E), f32)],
        scratch_shapes=[pltpu.VMEM((s5w // LANE, tm, LANE), f32)],
        compiler_params=_cparams("parallel", "parallel"),
        name="merge",
    )(x2, y_g, z, g_s, g_h, w_glu, w_sp, w_hp, w_out, ln_g, ln_b, wr_hi, wr_lo)


ROUTE_BLOCK = 256
ROUTE_SLOTS = 64
ROW_ALIGN = 8
GATE_LANES = 128


def _select_body(aff_ref, sel_ref, *, cap, idx_bits):
    bits = pltpu.bitcast(aff_ref[...], jnp.int32)
    E = bits.shape[0]
    count = lambda m: jnp.sum(jnp.where(m, 1.0, 0.0), axis=1, keepdims=True)

    def value_bit(i, prefix):
        cand = prefix | jnp.left_shift(jnp.int32(1), 30 - i)
        return jnp.where(count(bits >= cand) >= cap, cand, prefix)

    thr = lax.fori_loop(0, 31, value_bit, jnp.zeros((E, 1), jnp.int32))
    need = cap - count(bits > thr)
    idx = lax.broadcasted_iota(jnp.int32, bits.shape, 1)
    tie_idx = jnp.where(bits == thr, idx, jnp.int32(2 ** 30))

    def index_bit(i, bound):
        cand = bound | jnp.left_shift(jnp.int32(1), idx_bits - 1 - i)
        return jnp.where(count(tie_idx < cand) <= need, cand, bound)

    bound = lax.fori_loop(0, idx_bits, index_bit, jnp.zeros((E, 1), jnp.int32))
    sel_ref[...] = jnp.where(bits > thr, 1.0, jnp.where(tie_idx < bound, 1.0, 0.0))


def _select(aff_t, cap):
    E, n = aff_t.shape
    return pl.pallas_call(
        functools.partial(_select_body, cap=float(cap), idx_bits=int(n).bit_length()),
        out_shape=jax.ShapeDtypeStruct((E, n), f32),
        compiler_params=pltpu.CompilerParams(vmem_limit_bytes=VMEM_LIMIT_BYTES),
        name="expert_select",
    )(aff_t)


def _slot_onehot(sel, p_ref, first_slot, E):
    Tb = sel.shape[1]
    S = p_ref.shape[0] // E
    r = lax.broadcasted_iota(jnp.int32, (Tb, Tb), 0)
    c = lax.broadcasted_iota(jnp.int32, (Tb, Tb), 1)
    tri = jnp.where(r <= c, 1.0, 0.0).astype(bf16)
    incl = jnp.dot(sel.astype(bf16), tri, preferred_element_type=f32)
    slot = jnp.where(sel > 0.0, incl - 1.0, -1.0)
    want = (lax.broadcasted_iota(jnp.int32, (S, Tb), 0) + first_slot).astype(f32)
    for e in range(E):
        p_ref[e * S:(e + 1) * S, :] = jnp.where(slot[e:e + 1] == want, 1.0, 0.0).astype(bf16)


def _gather_copies(stage_ref, xe_hbm, sem, off_ref, j, first_slot, E, S, nblk):
    copies = []
    for e in range(E):
        row = pl.multiple_of(off_ref[e * nblk + j] + first_slot, ROW_ALIGN)
        copies.append(pltpu.make_async_copy(stage_ref.at[e * S:(e + 1) * S], xe_hbm.at[e, pl.ds(row, S)], sem))
    return copies


def _gather_body(off_ref, rounds_ref, total_ref, sel_ref, x_ref, aff_ref, xe_hbm, p_ref, stage_ref, sem, nwrites_ref,
                 *, E, nblk, cap):
    j = pl.program_id(0)
    S = p_ref.shape[0] // E
    D = x_ref.shape[1]

    @pl.when(j == 0)
    def _():
        nwrites_ref[0] = 0

    x = x_ref[...].astype(bf16)
    aff = aff_ref[...]
    a_hi = aff.astype(bf16)
    rem = aff - a_hi.astype(f32)
    a_mid = rem.astype(bf16)
    a_lo = (rem - a_mid.astype(f32)).astype(bf16)
    aff3 = jnp.concatenate([a_hi, a_mid, a_lo], axis=1)
    own = (lax.broadcasted_iota(jnp.int32, (E * S, 3 * E), 0) // S) == (
        lax.broadcasted_iota(jnp.int32, (E * S, 3 * E), 1) % E)

    def one_round(r):
        w = nwrites_ref[0]
        slot = lax.rem(w, 2)
        first_slot = r * S
        _slot_onehot(sel_ref[...], p_ref, first_slot, E)
        p = p_ref[...]
        stage_ref[slot, :, :D] = jnp.dot(p, x, preferred_element_type=f32)
        gates = jnp.dot(p, aff3, preferred_element_type=f32)
        gate = jnp.sum(jnp.where(own, gates, 0.0), axis=1, keepdims=True)
        stage_ref[slot, :, D:] = jnp.broadcast_to(gate, (E * S, GATE_LANES))

        @pl.when(w > 0)
        def _():
            for cp in _gather_copies(stage_ref.at[1 - slot], xe_hbm, sem, off_ref, j, 0, E, S, nblk):
                cp.wait()

        for cp in _gather_copies(stage_ref.at[slot], xe_hbm, sem, off_ref, j, first_slot, E, S, nblk):
            cp.start()
        nwrites_ref[0] = w + 1

    one_round(0)

    def extra(r, carry):
        one_round(r)
        return carry

    lax.fori_loop(1, rounds_ref[j], extra, 0)

    @pl.when(j == nblk - 1)
    def _():
        last = lax.rem(nwrites_ref[0] - 1, 2)
        for cp in _gather_copies(stage_ref.at[last], xe_hbm, sem, off_ref, j, 0, E, S, nblk):
            cp.wait()
        rows_pad = xe_hbm.shape[1]
        zeros_ref = stage_ref.at[0, :S]
        zeros_ref[...] = jnp.zeros((S, D + GATE_LANES), f32)
        nfill = -(-(rows_pad - cap) // S)

        def fill(e, row):
            return pltpu.make_async_copy(zeros_ref, xe_hbm.at[e, pl.ds(pl.multiple_of(row, ROW_ALIGN), S)], sem)

        def whole_chunks(e, action):
            def body(k, c):
                row = total_ref[e] + k * S

                @pl.when(row + S <= rows_pad)
                def _():
                    action(fill(e, row))
                return c
            lax.fori_loop(0, nfill, body, 0)

        for e in range(E):
            whole_chunks(e, lambda cp: cp.start())
        for e in range(E):
            whole_chunks(e, lambda cp: cp.wait())
        for e in range(E):
            fill(e, rows_pad - S).start()
        for e in range(E):
            fill(e, rows_pad - S).wait()


def _route_gather(sel, x1, aff, off, rounds, total, rows_pad):
    E, n = sel.shape
    D = x1.shape[1]
    Tb, S = ROUTE_BLOCK, ROUTE_SLOTS
    nblk = n // Tb
    return pl.pallas_call(
        functools.partial(_gather_body, E=E, nblk=nblk, cap=EC_CAPACITY * n // E),
        grid_spec=pltpu.PrefetchScalarGridSpec(
            num_scalar_prefetch=3,
            grid=(nblk,),
            in_specs=[pl.BlockSpec((E, Tb), lambda j, *_: (0, j)),
                      pl.BlockSpec((Tb, D), lambda j, *_: (j, 0)),
                      pl.BlockSpec((Tb, E), lambda j, *_: (j, 0))],
            out_specs=pl.BlockSpec(memory_space=pl.ANY),
            scratch_shapes=[pltpu.VMEM((E * S, Tb), bf16),
                            pltpu.VMEM((2, E * S, D + GATE_LANES), f32),
                            pltpu.SemaphoreType.DMA(()),
                            pltpu.SMEM((1,), jnp.int32)]),
        out_shape=jax.ShapeDtypeStruct((E, rows_pad, D + GATE_LANES), f32),
        compiler_params=_cparams("arbitrary"),
        name="route_gather",
    )(off, rounds, total, sel, x1, aff)


def _ffn_body(total_ref, xe_ref, w1_ref, w3_ref, w2_ref, ye_ref):
    e, r = pl.program_id(0), pl.program_id(1)
    tr = xe_ref.shape[1]
    D = ye_ref.shape[2]

    @pl.when(r * tr < total_ref[e])
    def _():
        xe = xe_ref[0, :, :D].astype(bf16)
        gate = xe_ref[0, :, D:D + 1]
        h1 = jnp.dot(xe, w1_ref[0], preferred_element_type=f32)
        h3 = jnp.dot(xe, w3_ref[0], preferred_element_type=f32)
        h = (jax.nn.silu(h1) * h3).astype(bf16)
        ye = jnp.dot(h, w2_ref[0], preferred_element_type=f32) * gate
        row = r * tr + lax.broadcasted_iota(jnp.int32, ye.shape, 0)
        ye_ref[0] = jnp.where(row < total_ref[e], ye, 0.0)

    @pl.when(r * tr >= total_ref[e])
    def _():
        ye_ref[...] = jnp.zeros_like(ye_ref)


def _ffn(total, xe, w1, w3, w2):
    E, rows_pad, Dx = xe.shape
    D, F = w1.shape[1:]
    tr = ROW_TILE
    last = lambda e, tot: (tot[e] - 1) // tr
    rowmap = lambda e, r, tot: (e, jnp.minimum(r, last(e, tot)), 0)
    wmap = lambda e, r, tot: (e, 0, 0)
    return pl.pallas_call(
        _ffn_body,
        grid_spec=pltpu.PrefetchScalarGridSpec(
            num_scalar_prefetch=1,
            grid=(E, rows_pad // tr),
            in_specs=[pl.BlockSpec((1, tr, Dx), rowmap),
                      pl.BlockSpec((1, D, F), wmap), pl.BlockSpec((1, D, F), wmap),
                      pl.BlockSpec((1, F, D), wmap)],
            out_specs=pl.BlockSpec((1, tr, D), lambda e, r, tot: (e, r, 0))),
        out_shape=jax.ShapeDtypeStruct((E, rows_pad, D), f32),
        compiler_params=_cparams("arbitrary", "arbitrary"),
        name="expert_ffn",
    )(total, xe, w1, w3, w2)


def _scatter_copies(ye_hbm, buf_ref, sem, off_ref, j, first_slot, E, S, nblk):
    copies = []
    for e in range(E):
        row = pl.multiple_of(off_ref[e * nblk + j] + first_slot, ROW_ALIGN)
        copies.append(pltpu.make_async_copy(ye_hbm.at[e, pl.ds(row, S)], buf_ref.at[e * S:(e + 1) * S], sem))
    return copies


def _scatter_body(off_ref, rounds_ref, sel_ref, x1_ref, g_ref, b_ref, ye_hbm, o_ref, p_ref, buf_ref, sem,
                  *, E, nblk, alpha):
    j = pl.program_id(0)
    S = p_ref.shape[0] // E
    slot = lax.rem(j, 2)

    def fetch(jj, first_slot, s):
        return _scatter_copies(ye_hbm, buf_ref.at[s], sem.at[s], off_ref, jj, first_slot, E, S, nblk)

    @pl.when(j == 0)
    def _():
        for cp in fetch(j, 0, 0):
            cp.start()

    @pl.when(j + 1 < nblk)
    def _():
        for cp in fetch(j + 1, 0, 1 - slot):
            cp.start()

    def one_round(r, s):
        _slot_onehot(sel_ref[...], p_ref, r * S, E)
        ye = buf_ref[s].astype(bf16)
        return lax.dot_general(p_ref[...], ye, (((0,), (0,)), ((), ())), preferred_element_type=f32)

    for cp in fetch(j, 0, slot):
        cp.wait()
    moe = one_round(jnp.int32(0), slot)

    def extra(r, acc):
        for cp in fetch(j, r * S, 2):
            cp.start()
        for cp in fetch(j, r * S, 2):
            cp.wait()
        return acc + one_round(r, 2)

    moe = lax.fori_loop(1, rounds_ref[j], extra, moe)
    o_ref[...] = _layer_norm(alpha * x1_ref[...] + moe, g_ref[...], b_ref[...])


def _route_scatter(sel, x1, ye, off, rounds, ln_g, ln_b, alpha):
    E, n = sel.shape
    D = x1.shape[1]
    Tb, S = ROUTE_BLOCK, ROUTE_SLOTS
    nblk = n // Tb
    return pl.pallas_call(
        functools.partial(_scatter_body, E=E, nblk=nblk, alpha=alpha),
        grid_spec=pltpu.PrefetchScalarGridSpec(
            num_scalar_prefetch=2,
            grid=(nblk,),
            in_specs=[pl.BlockSpec((E, Tb), lambda j, *_: (0, j)),
                      pl.BlockSpec((Tb, D), lambda j, *_: (j, 0)),
                      pl.BlockSpec((1, D), lambda j, *_: (0, 0)),
                      pl.BlockSpec((1, D), lambda j, *_: (0, 0)),
                      pl.BlockSpec(memory_space=pl.ANY)],
            out_specs=pl.BlockSpec((Tb, D), lambda j, *_: (j, 0)),
            scratch_shapes=[pltpu.VMEM((E * S, Tb), bf16),
                            pltpu.VMEM((3, E * S, D), f32),
                            pltpu.SemaphoreType.DMA((3,))]),
        out_shape=jax.ShapeDtypeStruct((n, D), f32),
        compiler_params=_cparams("arbitrary"),
        name="route_scatter",
    )(off, rounds, sel, x1, ln_g, ln_b, ye)


def _expert_choice(x1, aff, w1, w3, w2, ln_g, ln_b, alpha):
    n, D = x1.shape
    E = aff.shape[1]
    cap = EC_CAPACITY * n // E
    Tb, S = ROUTE_BLOCK, ROUTE_SLOTS
    nblk = n // Tb
    sel = _select(aff.T, cap)
    cnt = sel.reshape(E, nblk, Tb).sum(-1).astype(jnp.int32)
    cnt_al = (cnt + ROW_ALIGN - 1) // ROW_ALIGN * ROW_ALIGN
    end = jnp.cumsum(cnt_al, axis=1)
    off = (end - cnt_al).reshape(E * nblk)
    total = end[:, -1]
    rounds = jnp.maximum((jnp.max(cnt, axis=0) + S - 1) // S, 1)
    rows_pad = -(-(cap + ROW_ALIGN * nblk + S) // ROW_TILE) * ROW_TILE
    xe = _route_gather(sel, x1, aff, off, rounds, total, rows_pad)
    ye = _ffn(total, xe, w1, w3, w2)
    return _route_scatter(sel, x1, ye, off, rounds, ln_g, ln_b, alpha)


FILT_FEAT_PAD = 32


def _lag_features(L):
    n = np.arange(2 * L, dtype=np.float64)
    pos = np.where(n < L, n, 2 * L - n)
    t = pos / (L - 1)
    bands = np.linspace(1e-4, FILT_BANDS - 1, FILT_BANDS)
    ang = 2.0 * math.pi * (pos / L)[:, None] * bands[None, :]
    z = np.concatenate([t[:, None], np.cos(ang), -np.sin(ang)], axis=-1)
    out = np.zeros((2 * L, FILT_FEAT_PAD), np.float32)
    out[:, :z.shape[1]] = z
    return out


def _filter_body(z_ref, w1_ref, b1_ref, f1_ref, w2_ref, b2_ref, f2_ref, w3_ref, dec_ref, o_ref, *, L):
    hp = lax.Precision.HIGHEST
    z = z_ref[...]
    h = jnp.sin(f1_ref[...] * (jnp.dot(z, w1_ref[...], precision=hp, preferred_element_type=f32) + b1_ref[...]))
    h = jnp.sin(f2_ref[...] * (jnp.dot(h, w2_ref[...], precision=hp, preferred_element_type=f32) + b2_ref[...]))
    h = jnp.dot(h, w3_ref[...], precision=hp, preferred_element_type=f32)
    window = jnp.exp(-z[:, 0:1] * jnp.abs(dec_ref[...])) + FILT_SHIFT
    tl = z.shape[0]
    row = pl.program_id(0) * tl + lax.broadcasted_iota(jnp.int32, h.shape, 0)
    o_ref[...] = jnp.where(row == L, 0.0, h * window)


def _circ_filters(L, w1, b1, f1, w2, b2, f2, w3, decay):
    nh = w1.shape[1]
    oc = w3.shape[1] // 2
    feats = jnp.asarray(_lag_features(L))
    w1p = jnp.zeros((FILT_FEAT_PAD, nh), f32).at[:w1.shape[0]].set(w1)
    tl = min(ROW_TILE, L)
    half = L // tl
    dirsel = lambda i: (i // half, 0, 0)
    return pl.pallas_call(
        functools.partial(_filter_body, L=L),
        grid=(2 * L // tl,),
        in_specs=[pl.BlockSpec((tl, FILT_FEAT_PAD), lambda i: (i, 0)),
                  _const_spec((FILT_FEAT_PAD, nh)), _const_spec((1, nh)), _const_spec((1, nh)),
                  _const_spec((nh, nh)), _const_spec((1, nh)), _const_spec((1, nh)),
                  pl.BlockSpec((None, nh, oc), dirsel),
                  pl.BlockSpec((None, 1, oc), dirsel)],
        out_specs=pl.BlockSpec((tl, oc), lambda i: (i, 0)),
        out_shape=jax.ShapeDtypeStruct((2 * L, oc), f32),
        compiler_params=_cparams("parallel"),
        name="hyena_filter",
    )(feats, w1p, b1.reshape(1, nh), f1.reshape(1, nh), w2, b2.reshape(1, nh), f2.reshape(1, nh),
      w3.reshape(nh, 2, oc).transpose(1, 0, 2), decay.reshape(2, 1, oc))


DFT_N2 = 128
FREQ_CHAINS = 8
COL_TILE = 4096


def _dft_tables(L):
    N, N2 = 2 * L, DFT_N2
    N1 = N // N2
    H1 = N1 // 2
    K1n = H1 + 1
    K1p = -(-K1n // 8) * 8
    k1 = np.arange(K1n)
    n1 = np.arange(N1)
    ang = 2.0 * math.pi * ((k1[:, None] * n1[None, :]) % N1) / N1
    f1 = np.zeros((2 * K1p, N1))
    f1[:K1n] = np.cos(ang)
    f1[K1p:K1p + K1n] = -np.sin(ang)
    wgt = np.full(K1n, 2.0)
    wgt[0] = wgt[H1] = 1.0
    ang = 2.0 * math.pi * ((np.arange(H1)[:, None] * k1[None, :]) % N1) / N1
    if1 = np.zeros((H1, 2 * K1p))
    if1[:, :K1n] = wgt * np.cos(ang) / N
    if1[:, K1p:K1p + K1n] = -wgt * np.sin(ang) / N
    k2 = np.arange(N2)
    ang = 2.0 * math.pi * ((k2[:, None] * k2[None, :]) % N2) / N2
    f2r, f2i = np.cos(ang), -np.sin(ang)
    ang = 2.0 * math.pi * (k1[:, None] * k2[None, :]) / N
    twr = np.zeros((K1p, 1, N2))
    twi = np.zeros((K1p, 1, N2))
    twr[:K1n, 0], twi[:K1n, 0] = np.cos(ang), -np.sin(ang)
    c = lambda a, dt: jnp.asarray(a.astype(np.float32)).astype(dt)
    return dict(N1=N1, H1=H1, K1p=K1p, f1=c(f1, bf16), if1=c(if1, bf16), f2r=c(f2r, f32), f2i=c(f2i, f32),
                twr=c(twr, f32), twi=c(twi, f32))


def _lmat_body(w_ref, x_ref, o_ref):
    o_ref[0] = jnp.dot(w_ref[...], x_ref[0].astype(bf16), preferred_element_type=f32).astype(o_ref.dtype)


def _dft_stage1(w, x):
    Bx, Kd, cols = x.shape
    Mo = w.shape[0]
    tc = min(COL_TILE, cols)
    return pl.pallas_call(
        _lmat_body,
        grid=(Bx, cols // tc),
        in_specs=[_const_spec((Mo, Kd)), pl.BlockSpec((1, Kd, tc), lambda b, j: (b, 0, j))],
        out_specs=pl.BlockSpec((1, Mo, tc), lambda b, j: (b, 0, j)),
        out_shape=jax.ShapeDtypeStruct((Bx, Mo, cols), bf16),
        compiler_params=_cparams("parallel", "parallel"),
        name="dft_stage1",
    )(w, x)


def _build_stage2_matrix(f2r_ref, f2i_ref, twr_ref, twi_ref, m_ref, mt_ref):
    n2 = f2r_ref.shape[0]
    twr, twi = twr_ref[...], twi_ref[...]
    re = f2r_ref[...] * twr - f2i_ref[...] * twi
    im = f2r_ref[...] * twi + f2i_ref[...] * twr
    m_ref[:n2, :n2] = re.astype(bf16)
    m_ref[:n2, n2:] = (-im).astype(bf16)
    m_ref[n2:, :n2] = im.astype(bf16)
    m_ref[n2:, n2:] = re.astype(bf16)
    if mt_ref is not None:
        ret, imt = re.T, im.T
        mt_ref[:n2, :n2] = ret.astype(bf16)
        mt_ref[:n2, n2:] = imt.astype(bf16)
        mt_ref[n2:, :n2] = (-imt).astype(bf16)
        mt_ref[n2:, n2:] = ret.astype(bf16)


def _spectrum_body(f2r_ref, f2i_ref, twr_ref, twi_ref, a_ref, x_ref, m_ref):
    _build_stage2_matrix(f2r_ref, f2i_ref, twr_ref, twi_ref, m_ref, None)
    n2, cw = a_ref.shape[1:]
    x = jnp.dot(m_ref[...], a_ref[...].reshape(2 * n2, cw), preferred_element_type=f32)
    x_ref[...] = x.reshape(2, n2, cw)


def _filter_spectrum(tab, a5):
    _, _, K1p, N2, Cw = a5.shape
    tw_spec = pl.BlockSpec((None, 1, N2), lambda k: (k, 0, 0))
    return pl.pallas_call(
        _spectrum_body,
        grid=(K1p,),
        in_specs=[_const_spec((N2, N2)), _const_spec((N2, N2)), tw_spec, tw_spec,
                  pl.BlockSpec((None, 2, None, N2, Cw), lambda k: (0, 0, k, 0, 0))],
        out_specs=pl.BlockSpec((None, 2, N2, Cw), lambda k: (k, 0, 0, 0)),
        out_shape=jax.ShapeDtypeStruct((K1p, 2, N2, Cw), f32),
        scratch_shapes=[pltpu.VMEM((2 * N2, 2 * N2), bf16)],
        compiler_params=_cparams("arbitrary"),
        name="filter_spectrum",
    )(tab['f2r'], tab['f2i'], tab['twr'], tab['twi'], a5)


def _freq_body(f2r_ref, f2i_ref, twr_ref, twi_ref, a_ref, h_ref, g_ref, m_ref, mt_ref):
    nb, _, nk, n2, c = a_ref.shape
    for kk in range(nk):
        _build_stage2_matrix(f2r_ref, f2i_ref, twr_ref.at[kk], twi_ref.at[kk], m_ref.at[kk], mt_ref.at[kk])
        hr, hi = h_ref[kk, 0], h_ref[kk, 1]
        for b in range(nb):
            x = jnp.dot(m_ref[kk], a_ref[b, :, kk].reshape(2 * n2, c), preferred_element_type=f32)
            xr, xi = x[:n2], x[n2:]
            y = jnp.concatenate([xr * hr - xi * hi, xr * hi + xi * hr], axis=0).astype(bf16)
            g = jnp.dot(mt_ref[kk], y, preferred_element_type=f32)
            g_ref[b, :, kk] = g.reshape(2, n2, c).astype(bf16)


def _freq_stage(tab, a5, hspec, order):
    B, _, K1p, N2, C = a5.shape
    nb = min(B, FREQ_CHAINS)
    nk = FREQ_CHAINS // nb
    tw_spec = pl.BlockSpec((nk, 1, N2), lambda k, b: (k, 0, 0))
    slab = pl.BlockSpec((nb, 2, nk, N2, C), lambda k, b: (b, 0, k, 0, 0))
    return pl.pallas_call(
        _freq_body,
        grid=(K1p // nk, B // nb),
        in_specs=[_const_spec((N2, N2)), _const_spec((N2, N2)), tw_spec, tw_spec, slab,
                  pl.BlockSpec((nk, 2, N2, C), lambda k, b: (k, 0, 0, order))],
        out_specs=slab,
        out_shape=jax.ShapeDtypeStruct(a5.shape, bf16),
        scratch_shapes=[pltpu.VMEM((nk, 2 * N2, 2 * N2), bf16), pltpu.VMEM((nk, 2 * N2, 2 * N2), bf16)],
        compiler_params=_cparams("parallel", "parallel"),
        name="freq_stage",
    )(tab['f2r'], tab['f2i'], tab['twr'], tab['twi'], a5, hspec)


DFT_SLABS = 16


def _slabs_to_matrix(load_tile, rows, S, C, stage_ref):
    for lt in range(C // LANE):
        stage_ref[lt, :rows * S] = load_tile(lt)
    cols = [stage_ref[lt, pl.ds(sl, rows, stride=S), :] for sl in range(S) for lt in range(C // LANE)]
    return jnp.concatenate(cols, axis=1)


def _matrix_to_slabs(mat, rows, S, C, stage_ref):
    for sl in range(S):
        for lt in range(C // LANE):
            lo = sl * C + lt * LANE
            stage_ref[lt, pl.ds(sl, rows, stride=S), :] = mat[:, lo:lo + LANE]
    return [stage_ref[lt, :rows * S].reshape(rows, S, LANE) for lt in range(C // LANE)]


def _stage1_body(w_ref, z_ref, o_ref, stage_ref):
    H1, S, C = z_ref.shape
    K1p = o_ref.shape[1]
    zmat = _slabs_to_matrix(lambda lt: z_ref[:, :, lt * LANE:(lt + 1) * LANE].reshape(H1 * S, LANE), H1, S, C, stage_ref)
    a = jnp.dot(w_ref[...], zmat.astype(bf16), preferred_element_type=f32)
    for lt, tile in enumerate(_matrix_to_slabs(a, 2 * K1p, S, C, stage_ref)):
        o_ref[0, :, :, lt * LANE:(lt + 1) * LANE] = tile[:K1p].astype(bf16)
        o_ref[1, :, :, lt * LANE:(lt + 1) * LANE] = tile[K1p:].astype(bf16)


def _data_stage1(w, z4, K1p):
    B, H1, N2, C = z4.shape
    S = DFT_SLABS
    return pl.pallas_call(
        _stage1_body,
        grid=(B, N2 // S),
        in_specs=[_const_spec(w.shape), pl.BlockSpec((None, H1, S, C), lambda b, j: (b, 0, j, 0))],
        out_specs=pl.BlockSpec((None, 2, K1p, S, C), lambda b, j: (b, 0, 0, j, 0)),
        out_shape=jax.ShapeDtypeStruct((B, 2, K1p, N2, C), bf16),
        scratch_shapes=[pltpu.VMEM((C // LANE, 2 * K1p * S, LANE), f32)],
        compiler_params=_cparams("parallel", "parallel"),
        name="dft_data_stage1",
    )(w, z4)


def _conv_out_body(w_ref, g_ref, z_ref, gate_ref, bias_ref, o_ref, stage_ref):
    _, K1p, S, C = g_ref.shape
    H1 = z_ref.shape[0]

    def load_tile(lt):
        parts = [g_ref[ri, :, :, lt * LANE:(lt + 1) * LANE].astype(f32).reshape(K1p * S, LANE) for ri in range(2)]
        return jnp.concatenate(parts, axis=0)

    gmat = _slabs_to_matrix(load_tile, 2 * K1p, S, C, stage_ref)
    y = jnp.dot(w_ref[...], gmat.astype(bf16), preferred_element_type=f32)
    for lt, tile in enumerate(_matrix_to_slabs(y, H1, S, C, stage_ref)):
        sl = slice(lt * LANE, (lt + 1) * LANE)
        o_ref[:, :, sl] = gate_ref[:, :, sl] * (tile + z_ref[:, :, sl] * bias_ref[:, sl].reshape(1, 1, LANE))


def _conv_out(w, g5, z4, gate4, bias):
    B, H1, N2, C = z4.shape
    K1p = g5.shape[2]
    S = DFT_SLABS
    blk = pl.BlockSpec((None, H1, S, C), lambda b, j: (b, 0, j, 0))
    return pl.pallas_call(
        _conv_out_body,
        grid=(B, N2 // S),
        in_specs=[_const_spec(w.shape), pl.BlockSpec((None, 2, K1p, S, C), lambda b, j: (b, 0, 0, j, 0)), blk, blk,
                  _const_spec((1, C))],
        out_specs=blk,
        out_shape=jax.ShapeDtypeStruct(z4.shape, f32),
        scratch_shapes=[pltpu.VMEM((C // LANE, 2 * K1p * S, LANE), f32)],
        compiler_params=_cparams("parallel", "parallel"),
        name="conv_out",
    )(w, g5, z4, gate4, bias)


def _long_conv(v, g1, g2, circ, hy_bias, B, L, C):
    tab = _dft_tables(L)
    N2, N1, H1, K1p = DFT_N2, tab['N1'], tab['H1'], tab['K1p']
    Cw = circ.shape[1]
    a5 = _dft_stage1(tab['f1'], circ.reshape(1, N1, N2 * Cw)).reshape(1, 2, K1p, N2, Cw)
    hspec = _filter_spectrum(tab, a5)
    z4 = v.reshape(B, H1, N2, C)
    gates = (g1.reshape(B, H1, N2, C), g2.reshape(B, H1, N2, C))
    f1h = tab['f1'][:, :H1]
    for o in range(HY_ORDER):
        g5 = _freq_stage(tab, _data_stage1(f1h, z4, K1p), hspec, o)
        z4 = _conv_out(tab['if1'], g5, z4, gates[o], hy_bias[o].reshape(1, C))
    return z4.reshape(B * L, C)


def _s5_rows(B, L):
    return (True, S5_ROWS, L // S5_ROWS) if B == 1 else (False, B, L)


def _encoder_layer(x, p, s5_mats, alpha):
    B, L, D = x.shape
    n = B * L
    x2 = x.reshape(n, D)
    G, P, H = p['s5_b_re'].shape[1:]
    s5w = G * H
    hyw = p['w_hy_proj'].shape[0]

    u_g, v, g1, g2, g_s, g_h = _inproj(x2, p['w_in'].astype(bf16), p['hy_short_w'], p['hy_short_b'], B, L, G, H, hyw)

    chained, Bs, _ = _s5_rows(B, L)
    y_g = _s5(u_g, s5_mats, Bs, chained)

    circ = _circ_filters(L, p['filt_w1'], p['filt_b1'], p['filt_freq1'], p['filt_w2'], p['filt_b2'],
                         p['filt_freq2'], p['filt_w3'], p['filt_decay'])
    z = _long_conv(v, g1, g2, circ, p['hy_bias'], B, L, hyw)

    x1, aff = _merge(x2, y_g, z, g_s, g_h, p['s5_w_glu'].astype(bf16), p['w_s5_proj'].astype(bf16),
                     p['w_hy_proj'].astype(bf16), p['w_out'].astype(bf16),
                     p['ln1_g'].reshape(1, D), p['ln1_b'].reshape(1, D), p['w_router'], B, L, alpha)

    out = _expert_choice(x1, aff, p['ex_w1_bf16'], p['ex_w3_bf16'], p['ex_w2_bf16'],
                         p['ln2_g'].reshape(1, D), p['ln2_b'].reshape(1, D), alpha)
    return out.reshape(B, L, D)


_PARAM_NAMES = ('w_in', 's5_lambda_re', 's5_lambda_im', 's5_log_dt', 's5_b_re', 's5_b_im',
                's5_c_re', 's5_c_im', 's5_d', 's5_w_glu', 'w_s5_proj',
                'hy_short_w', 'hy_short_b', 'filt_w1', 'filt_b1', 'filt_freq1', 'filt_w2', 'filt_b2',
                'filt_freq2', 'filt_w3', 'filt_decay', 'hy_bias', 'w_hy_proj', 'w_out',
                'ln1_g', 'ln1_b', 'w_router', 'ex_w1', 'ex_w3', 'ex_w2', 'ln2_g', 'ln2_b')


def kernel(x_prompt, x_sample, w_in, s5_lambda_re, s5_lambda_im, s5_log_dt, s5_b_re, s5_b_im, s5_c_re, s5_c_im, s5_d, s5_w_glu, w_s5_proj, hy_short_w, hy_short_b, filt_w1, filt_b1, filt_freq1, filt_w2, filt_b2, filt_freq2, filt_w3, filt_decay, hy_bias, w_hy_proj, w_out, ln1_g, ln1_b, w_router, ex_w1, ex_w3, ex_w2, ln2_g, ln2_b):
    stacked = (w_in, s5_lambda_re, s5_lambda_im, s5_log_dt, s5_b_re, s5_b_im, s5_c_re, s5_c_im, s5_d,
               s5_w_glu, w_s5_proj, hy_short_w, hy_short_b, filt_w1, filt_b1, filt_freq1, filt_w2,
               filt_b2, filt_freq2, filt_w3, filt_decay, hy_bias, w_hy_proj, w_out, ln1_g, ln1_b,
               w_router, ex_w1, ex_w3, ex_w2, ln2_g, ln2_b)
    depth = w_in.shape[0]
    alpha = (2.0 * depth) ** 0.25
    xs = [x_prompt, x_sample]
    for l in range(depth):
        p = {k: v[l] for k, v in zip(_PARAM_NAMES, stacked)}
        for k in ('ex_w1', 'ex_w3', 'ex_w2'):
            p[k + '_bf16'] = p[k].astype(bf16)
        s5_mats = _s5_matrices(p, [_s5_rows(x.shape[0], x.shape[1])[2] for x in xs])
        xs = [_encoder_layer(x, p, mats, alpha) for x, mats in zip(xs, s5_mats)]
    return tuple(xs)
```

```python
import functools
import math

import jax
import jax.numpy as jnp
import numpy as np
from jax import lax
from jax.experimental import pallas as pl
from jax.experimental.pallas import tpu as pltpu

FILT_BANDS = 8
FILT_SHIFT = 0.05
EC_CAPACITY = 2
LN_EPS = 1e-5
HY_ORDER = 2

VMEM_LIMIT_BYTES = 56 * 1024 * 1024
ROW_TILE = 512

bf16 = jnp.bfloat16
f32 = jnp.float32


def _cparams(*sem):
    return pltpu.CompilerParams(dimension_semantics=sem, vmem_limit_bytes=VMEM_LIMIT_BYTES)


def _const_spec(shape):
    return pl.BlockSpec(shape, lambda *_: (0,) * len(shape))


def _inproj_body(xp_ref, x_ref, xn_ref, w_ref, cw_ref, cb_ref, ug_ref, v_ref, g1_ref, g2_ref, gs_ref, gh_ref,
                 stage_ref, *, o1, o2, o3, nt, H):
    i = pl.program_id(1)
    proj = jnp.dot(x_ref[...].astype(bf16), w_ref[...], preferred_element_type=f32)
    _rows_to_chunks(proj[:, :o1], stage_ref, ug_ref, H)
    gs_ref[...] = jax.nn.sigmoid(proj[:, o2:o3])
    gh_ref[...] = jax.nn.sigmoid(proj[:, o3:])
    u = proj[:, o1:o2]
    w_hy = w_ref[:, o1:o2]
    prev_row = jnp.dot(xp_ref[...].astype(bf16), w_hy, preferred_element_type=f32)[7:8]
    next_row = jnp.dot(xn_ref[...].astype(bf16), w_hy, preferred_element_type=f32)[0:1]
    prev_row = jnp.where(i == 0, 0.0, prev_row)
    next_row = jnp.where(i == nt - 1, 0.0, next_row)
    tl = u.shape[0]
    rows = lax.broadcasted_iota(jnp.int32, u.shape, 0)
    um1 = jnp.where(rows == 0, prev_row, pltpu.roll(u, 1, axis=0))
    up1 = jnp.where(rows == tl - 1, next_row, pltpu.roll(u, tl - 1, axis=0))
    cw = cw_ref[...]
    hy = um1 * cw[0:1, :] + u * cw[1:2, :] + up1 * cw[2:3, :] + cb_ref[...]
    hyw = v_ref.shape[1]
    v_ref[...] = hy[:, :hyw]
    g1_ref[...] = hy[:, hyw:2 * hyw]
    g2_ref[...] = hy[:, 2 * hyw:]


def _inproj(x2, w_in, conv_w, conv_b, B, L, G, H, hyw):
    n, D = x2.shape
    cols = w_in.shape[1]
    s5w = G * H
    o1, o2 = s5w, s5w + 3 * hyw
    o3 = o2 + D
    tm = min(ROW_TILE, L)
    nt = L // tm
    r8 = tm // 8
    row = lambda b, i: (b * nt + i, 0)
    prev = lambda b, i: (jnp.maximum((b * nt + i) * r8 - 1, 0), 0)
    nxt = lambda b, i: (jnp.minimum((b * nt + i + 1) * r8, n // 8 - 1), 0)
    hy_out = jax.ShapeDtypeStruct((n, hyw), f32)
    return pl.pallas_call(
        functools.partial(_inproj_body, o1=o1, o2=o2, o3=o3, nt=nt, H=H),
        grid=(B, nt),
        in_specs=[pl.BlockSpec((8, D), prev), pl.BlockSpec((tm, D), row), pl.BlockSpec((8, D), nxt),
                  _const_spec((D, cols)), _const_spec((3, 3 * hyw)), _const_spec((1, 3 * hyw))],
        out_specs=[pl.BlockSpec((G, tm // S5_CHUNK, S5_CHUNK * H), lambda b, i: (0, b * nt + i, 0)),
                   pl.BlockSpec((tm, hyw), row), pl.BlockSpec((tm, hyw), row), pl.BlockSpec((tm, hyw), row),
                   pl.BlockSpec((tm, D), row), pl.BlockSpec((tm, D), row)],
        out_shape=[jax.ShapeDtypeStruct((G, n // S5_CHUNK, S5_CHUNK * H), bf16), hy_out, hy_out, hy_out,
                   jax.ShapeDtypeStruct((n, D), f32), jax.ShapeDtypeStruct((n, D), f32)],
        scratch_shapes=[pltpu.VMEM((s5w // LANE, tm, LANE), f32)],
        compiler_params=_cparams("parallel", "parallel"),
        name="inproj",
    )(x2, x2, x2, w_in, conv_w, conv_b.reshape(1, 3 * hyw))


S5_CHUNK = 16
S5_ROWS = 8
LANE = 128


def _s5_body(u_ref, kmat_ref, winf_ref, winb_ref, woutf_ref, woutb_ref, af_ref, ab_ref, y_ref,
             vf_ref, vb_ref, sf_ref, sb_ref, *, R, nc, chained):
    u = u_ref[0]
    hw = sf_ref.shape[1]
    seq_rows = lambda b: pl.ds(b, nc, stride=R)
    for v_ref, w_ref in ((vf_ref, winf_ref), (vb_ref, winb_ref)):
        v = jnp.dot(u, w_ref[0], preferred_element_type=f32)
        for b in range(R):
            v_ref[0, seq_rows(b), :] = v[b * nc:(b + 1) * nc, :hw]
            v_ref[1, seq_rows(b), :] = v[b * nc:(b + 1) * nc, hw:]
    bc = lambda ref, row, lo: jnp.broadcast_to(ref[0, row:row + 1, lo:lo + hw], (R, hw))
    decay = lambda ref, row: ((bc(ref, row, 0), bc(ref, row, hw)), (bc(ref, row + 1, 0), bc(ref, row + 1, hw)))
    f_a1, f_a2 = decay(af_ref, 0)
    b_a1, b_a2 = decay(ab_ref, 0)

    def mul_add(s0, s1, a1, a2, v0, v1):
        return a1[0] * s0 + a2[0] * s1 + v0, a1[1] * s1 + a2[1] * s0 + v1

    def scan(init, store):
        def step(i, carry):
            f0, f1, b0, b1 = carry
            rf = pl.ds(pl.multiple_of(i * R, R), R)
            rb = pl.ds(pl.multiple_of((nc - 1 - i) * R, R), R)
            if store:
                sf_ref[rf, :] = f0
                sb_ref[rb, :] = b0
            f0, f1 = mul_add(f0, f1, f_a1, f_a2, vf_ref[0, rf, :], vf_ref[1, rf, :])
            b0, b1 = mul_add(b0, b1, b_a1, b_a2, vb_ref[0, rb, :], vb_ref[1, rb, :])
            return f0, f1, b0, b1
        return lax.fori_loop(0, nc, step, init, unroll=4)

    zero = jnp.zeros((R, hw), f32)
    init = (zero, zero, zero, zero)
    if chained:
        ef0, ef1, eb0, eb1 = scan(init, store=False)
        fn_a1, fn_a2 = decay(af_ref, 2)
        bn_a1, bn_a2 = decay(ab_ref, 2)
        row = lax.broadcasted_iota(jnp.int32, (R, hw), 0)
        down = lambda v: jnp.where(row == 0, 0.0, pltpu.roll(v, 1, axis=0))
        up = lambda v: jnp.where(row == R - 1, 0.0, pltpu.roll(v, R - 1, axis=0))
        f0, f1, b0, b1 = init
        for _ in range(R - 1):
            t0, t1 = mul_add(f0, f1, fn_a1, fn_a2, ef0, ef1)
            f0, f1 = down(t0), down(t1)
            t0, t1 = mul_add(b0, b1, bn_a1, bn_a2, eb0, eb1)
            b0, b1 = up(t0), up(t1)
        init = (f0, f1, b0, b1)
    scan(init, store=True)
    y = jnp.dot(u, kmat_ref[0], preferred_element_type=f32)
    for s_ref, w_ref in ((sf_ref, woutf_ref), (sb_ref, woutb_ref)):
        states = jnp.concatenate([s_ref[seq_rows(b), :] for b in range(R)], axis=0)
        y = y + jnp.dot(states.astype(bf16), w_ref[0], preferred_element_type=f32)
    y_ref[0] = y


def _s5(u_g, mats, R, chained):
    G, M, W = u_g.shape
    kmat, winf, winb, woutf, woutb, af, ab = mats
    hw = woutf.shape[1]
    grp = lambda shape: pl.BlockSpec((1,) + shape, lambda g: (g, 0, 0))
    return pl.pallas_call(
        functools.partial(_s5_body, R=R, nc=M // R, chained=chained),
        grid=(G,),
        in_specs=[grp((M, W)), grp((W, W)), grp((W, 2 * hw)), grp((W, 2 * hw)), grp((hw, W)), grp((hw, W)),
                  grp((4, 2 * hw)), grp((4, 2 * hw))],
        out_specs=grp((M, W)),
        out_shape=jax.ShapeDtypeStruct((G, M, W), f32),
        scratch_shapes=[pltpu.VMEM((2, M, hw), f32), pltpu.VMEM((2, M, hw), f32),
                        pltpu.VMEM((M, hw), f32), pltpu.VMEM((M, hw), f32)],
        compiler_params=_cparams("parallel"),
        name="s5_chunked",
    )(u_g, kmat, winf, winb, woutf, woutb, af, ab)


def _rows_to_chunks(x, stage_ref, o_ref, H):
    Tc = S5_CHUNK
    nr = x.shape[0] // Tc
    gpt, per_tile = LANE // H, LANE // H
    for lt in range(x.shape[1] // LANE):
        stage_ref[lt] = x[:, lt * LANE:(lt + 1) * LANE]
        rows_t = [stage_ref[lt, pl.ds(t, nr, stride=Tc), :] for t in range(Tc)]
        for g in range(gpt):
            for j in range(Tc // per_tile):
                tile = jnp.concatenate([rows_t[per_tile * j + k][:, H * g:H * (g + 1)] for k in range(per_tile)], axis=1)
                o_ref[lt * gpt + g, :, j * LANE:(j + 1) * LANE] = tile.astype(o_ref.dtype)


def _chunks_to_rows(y_ref, stage_ref, H):
    Tc = S5_CHUNK
    nr = y_ref.shape[1]
    gpt = LANE // H
    for lt in range(stage_ref.shape[0]):
        for t in range(Tc):
            tile = jnp.concatenate([y_ref[lt * gpt + g, :, H * t:H * (t + 1)] for g in range(gpt)], axis=1)
            stage_ref[lt, pl.ds(t, nr, stride=Tc), :] = tile
    return jnp.concatenate([stage_ref[lt] for lt in range(stage_ref.shape[0])], axis=1)


def _s5_discretise(lam_re, lam_im, log_dt, b_re, b_im, powers):
    lam = lax.complex(-jnp.abs(lam_re.astype(f32)), lam_im.astype(f32))
    dt = jnp.exp(log_dt.astype(f32))[:, None]
    a_bar = jnp.exp(lam * dt)
    k = jnp.asarray(powers, f32)[None, :, None]
    apow = jnp.exp((lam * dt)[:, None, :] * k)
    b_bar = ((a_bar - 1.0) / lam)[..., None] * lax.complex(b_re.astype(f32), b_im.astype(f32))
    return apow, b_bar


def _s5_matrices(p, seg_steps):
    Tc = S5_CHUNK
    powers = list(range(Tc + 1)) + list(seg_steps)
    apf, bbf = _s5_discretise(p['s5_lambda_re'][0], p['s5_lambda_im'][0], p['s5_log_dt'][0],
                              p['s5_b_re'][0], p['s5_b_im'][0], powers)
    apb, bbb = _s5_discretise(p['s5_lambda_re'][1], p['s5_lambda_im'][1], p['s5_log_dt'][1],
                              p['s5_b_re'][1], p['s5_b_im'][1], powers)
    c = lax.complex(p['s5_c_re'].astype(f32), p['s5_c_im'].astype(f32))
    G, H, P = c.shape
    W = Tc * H
    kf = jnp.real(jnp.einsum('ghp,gtp,gpk->gthk', c, apf[:, :Tc], bbf))
    kb = jnp.real(jnp.einsum('ghp,gtp,gpk->gthk', c, apb[:, :Tc], bbb))
    tau = jnp.arange(Tc)[None, :] - jnp.arange(Tc)[:, None]
    blk = (jnp.where((tau >= 0)[None, :, :, None, None], kf[:, jnp.abs(tau)], 0.0)
           + jnp.where((tau <= 0)[None, :, :, None, None], kb[:, jnp.abs(tau)], 0.0))
    kmat = jnp.transpose(blk, (0, 1, 4, 2, 3)).reshape(G, W, W)
    kmat = kmat + jnp.eye(W, dtype=f32)[None] * jnp.tile(p['s5_d'].astype(f32), (1, Tc))[:, None, :]
    lanes = lambda z: jnp.concatenate([jnp.real(z), jnp.imag(z), jnp.imag(z), jnp.real(z)], axis=-1)
    win = lambda ap, bb: lanes(jnp.einsum('gsp,gpk->gskp', ap, bb)).reshape(G, W, 4 * P)
    winf = win(apf[:, Tc - 1::-1][:, :Tc], bbf)
    winb = win(apb[:, :Tc], bbb)
    def wout(ap):
        z = jnp.einsum('ghp,gtp->gpth', c, ap).reshape(G, P, W)
        return jnp.concatenate([jnp.real(z), -jnp.imag(z)], axis=1)
    woutf = wout(apf[:, 1:Tc + 1])
    woutb = wout(apb[:, Tc:0:-1])
    def chunk_decay(ap, seg):
        rows = []
        for k in (Tc, Tc + 1 + seg):
            ar, ai = jnp.real(ap[:, k]), jnp.imag(ap[:, k])
            rows += [jnp.concatenate([ar, ar, ar, ar], -1), jnp.concatenate([-ai, ai, ai, -ai], -1)]
        return jnp.stack(rows, axis=1)
    cast = lambda m: m.astype(bf16)
    shared = (cast(kmat), cast(winf), cast(winb), cast(woutf), cast(woutb))
    return [shared + (chunk_decay(apf, i), chunk_decay(apb, i)) for i in range(len(seg_steps))]


MERGE_SPLIT = 2


def _layer_norm(v, g, b):
    mu = jnp.mean(v, axis=-1, keepdims=True)
    c = v - mu
    var = jnp.mean(c * c, axis=-1, keepdims=True)
    return c * lax.rsqrt(var + LN_EPS) * g + b


def _merge_body(x_ref, yg_ref, z_ref, gs_ref, gh_ref, wglu_ref, wsp_ref, whp_ref, wout_ref,
                g_ref, b_ref, wrh_ref, wrl_ref, x1_ref, aff_ref, stage_ref, *, alpha, H):
    ys_all = jax.nn.gelu(_chunks_to_rows(yg_ref, stage_ref, H))
    tm = x_ref.shape[0]
    for r0 in range(0, tm, tm // MERGE_SPLIT):
        rs = slice(r0, r0 + tm // MERGE_SPLIT)
        ys = ys_all[rs]
        gate = jax.nn.sigmoid(jnp.dot(ys.astype(bf16), wglu_ref[...], preferred_element_type=f32))
        branch_s = jnp.dot((ys * gate).astype(bf16), wsp_ref[...], preferred_element_type=f32)
        branch_h = jnp.dot(z_ref[rs, :].astype(bf16), whp_ref[...], preferred_element_type=f32)
        mix = gs_ref[rs, :] * branch_s + gh_ref[rs, :] * branch_h
        mix = jnp.dot(mix.astype(bf16), wout_ref[...], preferred_element_type=f32)
        x1 = _layer_norm(alpha * x_ref[rs, :] + mix, g_ref[...], b_ref[...])
        x1_ref[rs, :] = x1
        x1_hi = x1.astype(bf16)
        x1_lo = (x1 - x1_hi.astype(f32)).astype(bf16)
        logits = (jnp.dot(x1_hi, wrh_ref[...], preferred_element_type=f32)
                  + jnp.dot(x1_lo, wrh_ref[...], preferred_element_type=f32)
                  + jnp.dot(x1_hi, wrl_ref[...], preferred_element_type=f32))
        m = jnp.max(logits, axis=-1, keepdims=True)
        e = jnp.exp(logits - m)
        aff_ref[rs, :] = e / jnp.sum(e, axis=-1, keepdims=True)


def _merge(x2, y_g, z, g_s, g_h, w_glu, w_sp, w_hp, w_out, ln_g, ln_b, w_router, B, L, alpha):
    n, D = x2.shape
    G, _, W = y_g.shape
    H = W // S5_CHUNK
    s5w, hyw = G * H, z.shape[1]
    E = w_router.shape[1]
    tm = min(ROW_TILE, L)
    nt = L // tm
    row = lambda b, i: (b * nt + i, 0)
    wr_hi = w_router.astype(bf16)
    wr_lo = (w_router - wr_hi.astype(f32)).astype(bf16)
    return pl.pallas_call(
        functools.partial(_merge_body, alpha=alpha, H=H),
        grid=(B, nt),
        in_specs=[pl.BlockSpec((tm, D), row),
                  pl.BlockSpec((G, tm // S5_CHUNK, W), lambda b, i: (0, b * nt + i, 0)),
                  pl.BlockSpec((tm, hyw), row),
                  pl.BlockSpec((tm, D), row),
                  pl.BlockSpec((tm, D), row),
                  _const_spec(w_glu.shape), _const_spec(w_sp.shape), _const_spec(w_hp.shape),
                  _const_spec(w_out.shape), _const_spec((1, D)), _const_spec((1, D)),
                  _const_spec(w_router.shape), _const_spec(w_router.shape)],
        out_specs=[pl.BlockSpec((tm, D), row), pl.BlockSpec((tm, E), row)],
        out_shape=[jax.ShapeDtypeStruct((n, D), f32), jax.ShapeDtypeStruct((n, E), f32)],
        scratch_shapes=[pltpu.VMEM((s5w // LANE, tm, LANE), f32)],
        compiler_params=_cparams("parallel", "parallel"),
        name="merge",
    )(x2, y_g, z, g_s, g_h, w_glu, w_sp, w_hp, w_out, ln_g, ln_b, wr_hi, wr_lo)


ROUTE_BLOCK = 256
ROUTE_SLOTS = 64
ROW_ALIGN = 8
GATE_LANES = 128


def _select_body(aff_ref, sel_ref, *, cap, idx_bits):
    bits = pltpu.bitcast(aff_ref[...], jnp.int32)
    E = bits.shape[0]
    count = lambda m: jnp.sum(jnp.where(m, 1.0, 0.0), axis=1, keepdims=True)

    def value_bit(i, prefix):
        cand = prefix | jnp.left_shift(jnp.int32(1), 30 - i)
        return jnp.where(count(bits >= cand) >= cap, cand, prefix)

    thr = lax.fori_loop(0, 31, value_bit, jnp.zeros((E, 1), jnp.int32))
    need = cap - count(bits > thr)
    idx = lax.broadcasted_iota(jnp.int32, bits.shape, 1)
    tie_idx = jnp.where(bits == thr, idx, jnp.int32(2 ** 30))

    def index_bit(i, bound):
        cand = bound | jnp.left_shift(jnp.int32(1), idx_bits - 1 - i)
        return jnp.where(count(tie_idx < cand) <= need, cand, bound)

    bound = lax.fori_loop(0, idx_bits, index_bit, jnp.zeros((E, 1), jnp.int32))
    sel_ref[...] = jnp.where(bits > thr, 1.0, jnp.where(tie_idx < bound, 1.0, 0.0))


def _select(aff_t, cap):
    E, n = aff_t.shape
    return pl.pallas_call(
        functools.partial(_select_body, cap=float(cap), idx_bits=int(n).bit_length()),
        out_shape=jax.ShapeDtypeStruct((E, n), f32),
        compiler_params=pltpu.CompilerParams(vmem_limit_bytes=VMEM_LIMIT_BYTES),
        name="expert_select",
    )(aff_t)


def _slot_onehot(sel, p_ref, first_slot, E):
    Tb = sel.shape[1]
    S = p_ref.shape[0] // E
    r = lax.broadcasted_iota(jnp.int32, (Tb, Tb), 0)
    c = lax.broadcasted_iota(jnp.int32, (Tb, Tb), 1)
    tri = jnp.where(r <= c, 1.0, 0.0).astype(bf16)
    incl = jnp.dot(sel.astype(bf16), tri, preferred_element_type=f32)
    slot = jnp.where(sel > 0.0, incl - 1.0, -1.0)
    want = (lax.broadcasted_iota(jnp.int32, (S, Tb), 0) + first_slot).astype(f32)
    for e in range(E):
        p_ref[e * S:(e + 1) * S, :] = jnp.where(slot[e:e + 1] == want, 1.0, 0.0).astype(bf16)


def _gather_copies(stage_ref, xe_hbm, sem, off_ref, j, first_slot, E, S, nblk):
    copies = []
    for e in range(E):
        row = pl.multiple_of(off_ref[e * nblk + j] + first_slot, ROW_ALIGN)
        copies.append(pltpu.make_async_copy(stage_ref.at[e * S:(e + 1) * S], xe_hbm.at[e, pl.ds(row, S)], sem))
    return copies


def _gather_body(off_ref, rounds_ref, total_ref, sel_ref, x_ref, aff_ref, xe_hbm, p_ref, stage_ref, sem, nwrites_ref,
                 *, E, nblk, cap):
    j = pl.program_id(0)
    S = p_ref.shape[0] // E
    D = x_ref.shape[1]

    @pl.when(j == 0)
    def _():
        nwrites_ref[0] = 0

    x = x_ref[...].astype(bf16)
    aff = aff_ref[...]
    a_hi = aff.astype(bf16)
    rem = aff - a_hi.astype(f32)
    a_mid = rem.astype(bf16)
    a_lo = (rem - a_mid.astype(f32)).astype(bf16)
    aff3 = jnp.concatenate([a_hi, a_mid, a_lo], axis=1)
    own = (lax.broadcasted_iota(jnp.int32, (E * S, 3 * E), 0) // S) == (
        lax.broadcasted_iota(jnp.int32, (E * S, 3 * E), 1) % E)

    def one_round(r):
        w = nwrites_ref[0]
        slot = lax.rem(w, 2)
        first_slot = r * S
        _slot_onehot(sel_ref[...], p_ref, first_slot, E)
        p = p_ref[...]
        stage_ref[slot, :, :D] = jnp.dot(p, x, preferred_element_type=f32)
        gates = jnp.dot(p, aff3, preferred_element_type=f32)
        gate = jnp.sum(jnp.where(own, gates, 0.0), axis=1, keepdims=True)
        stage_ref[slot, :, D:] = jnp.broadcast_to(gate, (E * S, GATE_LANES))

        @pl.when(w > 0)
        def _():
            for cp in _gather_copies(stage_ref.at[1 - slot], xe_hbm, sem, off_ref, j, 0, E, S, nblk):
                cp.wait()

        for cp in _gather_copies(stage_ref.at[slot], xe_hbm, sem, off_ref, j, first_slot, E, S, nblk):
            cp.start()
        nwrites_ref[0] = w + 1

    one_round(0)

    def extra(r, carry):
        one_round(r)
        return carry

    lax.fori_loop(1, rounds_ref[j], extra, 0)

    @pl.when(j == nblk - 1)
    def _():
        last = lax.rem(nwrites_ref[0] - 1, 2)
        for cp in _gather_copies(stage_ref.at[last], xe_hbm, sem, off_ref, j, 0, E, S, nblk):
            cp.wait()
        rows_pad = xe_hbm.shape[1]
        zeros_ref = stage_ref.at[0, :S]
        zeros_ref[...] = jnp.zeros((S, D + GATE_LANES), f32)
        nfill = -(-(rows_pad - cap) // S)

        def fill(e, row):
            return pltpu.make_async_copy(zeros_ref, xe_hbm.at[e, pl.ds(pl.multiple_of(row, ROW_ALIGN), S)], sem)

        def whole_chunks(e, action):
            def body(k, c):
                row = total_ref[e] + k * S

                @pl.when(row + S <= rows_pad)
                def _():
                    action(fill(e, row))
                return c
            lax.fori_loop(0, nfill, body, 0)

        for e in range(E):
            whole_chunks(e, lambda cp: cp.start())
        for e in range(E):
            whole_chunks(e, lambda cp: cp.wait())
        for e in range(E):
            fill(e, rows_pad - S).start()
        for e in range(E):
            fill(e, rows_pad - S).wait()


def _route_gather(sel, x1, aff, off, rounds, total, rows_pad):
    E, n = sel.shape
    D = x1.shape[1]
    Tb, S = ROUTE_BLOCK, ROUTE_SLOTS
    nblk = n // Tb
    return pl.pallas_call(
        functools.partial(_gather_body, E=E, nblk=nblk, cap=EC_CAPACITY * n // E),
        grid_spec=pltpu.PrefetchScalarGridSpec(
            num_scalar_prefetch=3,
            grid=(nblk,),
            in_specs=[pl.BlockSpec((E, Tb), lambda j, *_: (0, j)),
                      pl.BlockSpec((Tb, D), lambda j, *_: (j, 0)),
                      pl.BlockSpec((Tb, E), lambda j, *_: (j, 0))],
            out_specs=pl.BlockSpec(memory_space=pl.ANY),
            scratch_shapes=[pltpu.VMEM((E * S, Tb), bf16),
                            pltpu.VMEM((2, E * S, D + GATE_LANES), f32),
                            pltpu.SemaphoreType.DMA(()),
                            pltpu.SMEM((1,), jnp.int32)]),
        out_shape=jax.ShapeDtypeStruct((E, rows_pad, D + GATE_LANES), f32),
        compiler_params=_cparams("arbitrary"),
        name="route_gather",
    )(off, rounds, total, sel, x1, aff)


def _ffn_body(total_ref, xe_ref, w1_ref, w3_ref, w2_ref, ye_ref):
    e, r = pl.program_id(0), pl.program_id(1)
    tr = xe_ref.shape[1]
    D = ye_ref.shape[2]

    @pl.when(r * tr < total_ref[e])
    def _():
        xe = xe_ref[0, :, :D].astype(bf16)
        gate = xe_ref[0, :, D:D + 1]
        h1 = jnp.dot(xe, w1_ref[0], preferred_element_type=f32)
        h3 = jnp.dot(xe, w3_ref[0], preferred_element_type=f32)
        h = (jax.nn.silu(h1) * h3).astype(bf16)
        ye = jnp.dot(h, w2_ref[0], preferred_element_type=f32) * gate
        row = r * tr + lax.broadcasted_iota(jnp.int32, ye.shape, 0)
        ye_ref[0] = jnp.where(row < total_ref[e], ye, 0.0)

    @pl.when(r * tr >= total_ref[e])
    def _():
        ye_ref[...] = jnp.zeros_like(ye_ref)


def _ffn(total, xe, w1, w3, w2):
    E, rows_pad, Dx = xe.shape
    D, F = w1.shape[1:]
    tr = ROW_TILE
    last = lambda e, tot: (tot[e] - 1) // tr
    rowmap = lambda e, r, tot: (e, jnp.minimum(r, last(e, tot)), 0)
    wmap = lambda e, r, tot: (e, 0, 0)
    return pl.pallas_call(
        _ffn_body,
        grid_spec=pltpu.PrefetchScalarGridSpec(
            num_scalar_prefetch=1,
            grid=(E, rows_pad // tr),
            in_specs=[pl.BlockSpec((1, tr, Dx), rowmap),
                      pl.BlockSpec((1, D, F), wmap), pl.BlockSpec((1, D, F), wmap),
                      pl.BlockSpec((1, F, D), wmap)],
            out_specs=pl.BlockSpec((1, tr, D), lambda e, r, tot: (e, r, 0))),
        out_shape=jax.ShapeDtypeStruct((E, rows_pad, D), f32),
        compiler_params=_cparams("arbitrary", "arbitrary"),
        name="expert_ffn",
    )(total, xe, w1, w3, w2)


def _scatter_copies(ye_hbm, buf_ref, sem, off_ref, j, first_slot, E, S, nblk):
    copies = []
    for e in range(E):
        row = pl.multiple_of(off_ref[e * nblk + j] + first_slot, ROW_ALIGN)
        copies.append(pltpu.make_async_copy(ye_hbm.at[e, pl.ds(row, S)], buf_ref.at[e * S:(e + 1) * S], sem))
    return copies


def _scatter_body(off_ref, rounds_ref, sel_ref, x1_ref, g_ref, b_ref, ye_hbm, o_ref, p_ref, buf_ref, sem,
                  *, E, nblk, alpha):
    j = pl.program_id(0)
    S = p_ref.shape[0] // E
    slot = lax.rem(j, 2)

    def fetch(jj, first_slot, s):
        return _scatter_copies(ye_hbm, buf_ref.at[s], sem.at[s], off_ref, jj, first_slot, E, S, nblk)

    @pl.when(j == 0)
    def _():
        for cp in fetch(j, 0, 0):
            cp.start()

    @pl.when(j + 1 < nblk)
    def _():
        for cp in fetch(j + 1, 0, 1 - slot):
            cp.start()

    def one_round(r, s):
        _slot_onehot(sel_ref[...], p_ref, r * S, E)
        ye = buf_ref[s].astype(bf16)
        return lax.dot_general(p_ref[...], ye, (((0,), (0,)), ((), ())), preferred_element_type=f32)

    for cp in fetch(j, 0, slot):
        cp.wait()
    moe = one_round(jnp.int32(0), slot)

    def extra(r, acc):
        for cp in fetch(j, r * S, 2):
            cp.start()
        for cp in fetch(j, r * S, 2):
            cp.wait()
        return acc + one_round(r, 2)

    moe = lax.fori_loop(1, rounds_ref[j], extra, moe)
    o_ref[...] = _layer_norm(alpha * x1_ref[...] + moe, g_ref[...], b_ref[...])


def _route_scatter(sel, x1, ye, off, rounds, ln_g, ln_b, alpha):
    E, n = sel.shape
    D = x1.shape[1]
    Tb, S = ROUTE_BLOCK, ROUTE_SLOTS
    nblk = n // Tb
    return pl.pallas_call(
        functools.partial(_scatter_body, E=E, nblk=nblk, alpha=alpha),
        grid_spec=pltpu.PrefetchScalarGridSpec(
            num_scalar_prefetch=2,
            grid=(nblk,),
            in_specs=[pl.BlockSpec((E, Tb), lambda j, *_: (0, j)),
                      pl.BlockSpec((Tb, D), lambda j, *_: (j, 0)),
                      pl.BlockSpec((1, D), lambda j, *_: (0, 0)),
                      pl.BlockSpec((1, D), lambda j, *_: (0, 0)),
                      pl.BlockSpec(memory_space=pl.ANY)],
            out_specs=pl.BlockSpec((Tb, D), lambda j, *_: (j, 0)),
            scratch_shapes=[pltpu.VMEM((E * S, Tb), bf16),
                            pltpu.VMEM((3, E * S, D), f32),
                            pltpu.SemaphoreType.DMA((3,))]),
        out_shape=jax.ShapeDtypeStruct((n, D), f32),
        compiler_params=_cparams("arbitrary"),
        name="route_scatter",
    )(off, rounds, sel, x1, ln_g, ln_b, ye)


def _expert_choice(x1, aff, w1, w3, w2, ln_g, ln_b, alpha):
    n, D = x1.shape
    E = aff.shape[1]
    cap = EC_CAPACITY * n // E
    Tb, S = ROUTE_BLOCK, ROUTE_SLOTS
    nblk = n // Tb
    sel = _select(aff.T, cap)
    cnt = sel.reshape(E, nblk, Tb).sum(-1).astype(jnp.int32)
    cnt_al = (cnt + ROW_ALIGN - 1) // ROW_ALIGN * ROW_ALIGN
    end = jnp.cumsum(cnt_al, axis=1)
    off = (end - cnt_al).reshape(E * nblk)
    total = end[:, -1]
    rounds = jnp.maximum((jnp.max(cnt, axis=0) + S - 1) // S, 1)
    rows_pad = -(-(cap + ROW_ALIGN * nblk + S) // ROW_TILE) * ROW_TILE
    xe = _route_gather(sel, x1, aff, off, rounds, total, rows_pad)
    ye = _ffn(total, xe, w1, w3, w2)
    return _route_scatter(sel, x1, ye, off, rounds, ln_g, ln_b, alpha)


FILT_FEAT_PAD = 32


def _lag_features(L):
    n = np.arange(2 * L, dtype=np.float64)
    pos = np.where(n < L, n, 2 * L - n)
    t = pos / (L - 1)
    bands = np.linspace(1e-4, FILT_BANDS - 1, FILT_BANDS)
    ang = 2.0 * math.pi * (pos / L)[None, :] * bands[:, None]
    z = np.concatenate([t[None, :], np.cos(ang), -np.sin(ang)], axis=0)
    out = np.zeros((FILT_FEAT_PAD, 2 * L), np.float32)
    out[:z.shape[0]] = z
    return out, t.astype(np.float32)[:, None]


def _filter_body(z_ref, t_ref, w1_ref, b1_ref, f1_ref, w2_ref, b2_ref, f2_ref, w3_ref, dec_ref, o_ref, *, L):
    hp = lax.Precision.HIGHEST
    h = jnp.sin(f1_ref[...] * (jnp.dot(w1_ref[...], z_ref[...], precision=hp, preferred_element_type=f32) + b1_ref[...]))
    h = jnp.sin(f2_ref[...] * (jnp.dot(w2_ref[...], h, precision=hp, preferred_element_type=f32) + b2_ref[...]))
    h = lax.dot_general(h, w3_ref[...], (((0,), (0,)), ((), ())), precision=hp, preferred_element_type=f32)
    window = jnp.exp(-t_ref[...] * jnp.abs(dec_ref[...])) + FILT_SHIFT
    tl = h.shape[0]
    row = pl.program_id(0) * tl + lax.broadcasted_iota(jnp.int32, h.shape, 0)
    o_ref[...] = jnp.where(row == L, 0.0, h * window)


def _circ_filters(L, w1, b1, f1, w2, b2, f2, w3, decay):
    nh = w1.shape[1]
    oc = w3.shape[1] // 2
    feats, t_col = _lag_features(L)
    w1t = jnp.zeros((nh, FILT_FEAT_PAD), f32).at[:, :w1.shape[0]].set(w1.T)
    tl = min(ROW_TILE, L)
    half = L // tl
    dirsel = lambda i: (i // half, 0, 0)
    col = lambda a: a.reshape(nh, 1)
    return pl.pallas_call(
        functools.partial(_filter_body, L=L),
        grid=(2 * L // tl,),
        in_specs=[pl.BlockSpec((FILT_FEAT_PAD, tl), lambda i: (0, i)), pl.BlockSpec((tl, 1), lambda i: (i, 0)),
                  _const_spec((nh, FILT_FEAT_PAD)), _const_spec((nh, 1)), _const_spec((nh, 1)),
                  _const_spec((nh, nh)), _const_spec((nh, 1)), _const_spec((nh, 1)),
                  pl.BlockSpec((None, nh, oc), dirsel),
                  pl.BlockSpec((None, 1, oc), dirsel)],
        out_specs=pl.BlockSpec((tl, oc), lambda i: (i, 0)),
        out_shape=jax.ShapeDtypeStruct((2 * L, oc), f32),
        compiler_params=_cparams("parallel"),
        name="hyena_filter",
    )(jnp.asarray(feats), jnp.asarray(t_col), w1t, col(b1), col(f1), w2.T, col(b2), col(f2),
      w3.reshape(nh, 2, oc).transpose(1, 0, 2), decay.reshape(2, 1, oc))


DFT_N2 = 128
FREQ_CHAINS = 8


def _dft_tables(L):
    N, N2 = 2 * L, DFT_N2
    N1 = N // N2
    H1 = N1 // 2
    K1n = H1 + 1
    K1p = -(-K1n // 8) * 8
    k1 = np.arange(K1n)
    n1 = np.arange(N1)
    ang = 2.0 * math.pi * ((k1[:, None] * n1[None, :]) % N1) / N1
    f1 = np.zeros((2 * K1p, N1))
    f1[:K1n] = np.cos(ang)
    f1[K1p:K1p + K1n] = -np.sin(ang)
    wgt = np.full(K1n, 2.0)
    wgt[0] = wgt[H1] = 1.0
    ang = 2.0 * math.pi * ((np.arange(H1)[:, None] * k1[None, :]) % N1) / N1
    if1 = np.zeros((H1, 2 * K1p))
    if1[:, :K1n] = wgt * np.cos(ang) / N
    if1[:, K1p:K1p + K1n] = -wgt * np.sin(ang) / N
    k2 = np.arange(N2)
    ang = 2.0 * math.pi * ((k2[:, None] * k2[None, :]) % N2) / N2
    f2r, f2i = np.cos(ang), -np.sin(ang)
    ang = 2.0 * math.pi * (k1[:, None] * k2[None, :]) / N
    twr = np.zeros((K1p, 1, N2))
    twi = np.zeros((K1p, 1, N2))
    twr[:K1n, 0], twi[:K1n, 0] = np.cos(ang), -np.sin(ang)
    c = lambda a, dt: jnp.asarray(a.astype(np.float32)).astype(dt)
    return dict(N1=N1, H1=H1, K1p=K1p, f1=c(f1, bf16), if1=c(if1, bf16), f2r=c(f2r, f32), f2i=c(f2i, f32),
                twr=c(twr, f32), twi=c(twi, f32))


def _build_stage2_matrix(f2r_ref, f2i_ref, twr_ref, twi_ref, m_ref, mt_ref):
    n2 = f2r_ref.shape[0]
    twr, twi = twr_ref[...], twi_ref[...]
    re = f2r_ref[...] * twr - f2i_ref[...] * twi
    im = f2r_ref[...] * twi + f2i_ref[...] * twr
    m_ref[:n2, :n2] = re.astype(bf16)
    m_ref[:n2, n2:] = (-im).astype(bf16)
    m_ref[n2:, :n2] = im.astype(bf16)
    m_ref[n2:, n2:] = re.astype(bf16)
    if mt_ref is not None:
        ret, imt = re.T, im.T
        mt_ref[:n2, :n2] = ret.astype(bf16)
        mt_ref[:n2, n2:] = imt.astype(bf16)
        mt_ref[n2:, :n2] = (-imt).astype(bf16)
        mt_ref[n2:, n2:] = ret.astype(bf16)


def _spectrum_body(f2r_ref, f2i_ref, twr_ref, twi_ref, a_ref, x_ref, m_ref):
    _, nk, n2, c = a_ref.shape
    for kk in range(nk):
        _build_stage2_matrix(f2r_ref, f2i_ref, twr_ref.at[kk], twi_ref.at[kk], m_ref.at[kk], None)
        x = jnp.dot(m_ref[kk], a_ref[:, kk].reshape(2 * n2, c), preferred_element_type=f32)
        x_ref[kk] = x.reshape(2, n2, c)


def _filter_spectrum(tab, a5):
    NO, _, K1p, N2, C = a5.shape
    nk = FREQ_CHAINS
    tw_spec = pl.BlockSpec((nk, 1, N2), lambda k, o: (k, 0, 0))
    return pl.pallas_call(
        _spectrum_body,
        grid=(K1p // nk, NO),
        in_specs=[_const_spec((N2, N2)), _const_spec((N2, N2)), tw_spec, tw_spec,
                  pl.BlockSpec((None, 2, nk, N2, C), lambda k, o: (o, 0, k, 0, 0))],
        out_specs=pl.BlockSpec((nk, 2, N2, C), lambda k, o: (k, 0, 0, o)),
        out_shape=jax.ShapeDtypeStruct((K1p, 2, N2, NO * C), f32),
        scratch_shapes=[pltpu.VMEM((nk, 2 * N2, 2 * N2), bf16)],
        compiler_params=_cparams("parallel", "parallel"),
        name="filter_spectrum",
    )(tab['f2r'], tab['f2i'], tab['twr'], tab['twi'], a5)


def _freq_body(f2r_ref, f2i_ref, twr_ref, twi_ref, a_ref, h_ref, g_ref, m_ref, mt_ref):
    nb, _, nk, n2, c = a_ref.shape
    for kk in range(nk):
        _build_stage2_matrix(f2r_ref, f2i_ref, twr_ref.at[kk], twi_ref.at[kk], m_ref.at[kk], mt_ref.at[kk])
        hr, hi = h_ref[kk, 0], h_ref[kk, 1]
        for b in range(nb):
            x = jnp.dot(m_ref[kk], a_ref[b, :, kk].reshape(2 * n2, c), preferred_element_type=f32)
            xr, xi = x[:n2], x[n2:]
            y = jnp.concatenate([xr * hr - xi * hi, xr * hi + xi * hr], axis=0).astype(bf16)
            g = jnp.dot(mt_ref[kk], y, preferred_element_type=f32)
            g_ref[b, :, kk] = g.reshape(2, n2, c).astype(bf16)


def _freq_stage(tab, a5, hspec, order):
    B, _, K1p, N2, C = a5.shape
    nb = min(B, FREQ_CHAINS)
    nk = FREQ_CHAINS // nb
    tw_spec = pl.BlockSpec((nk, 1, N2), lambda k, b: (k, 0, 0))
    slab = pl.BlockSpec((nb, 2, nk, N2, C), lambda k, b: (b, 0, k, 0, 0))
    return pl.pallas_call(
        _freq_body,
        grid=(K1p // nk, B // nb),
        in_specs=[_const_spec((N2, N2)), _const_spec((N2, N2)), tw_spec, tw_spec, slab,
                  pl.BlockSpec((nk, 2, N2, C), lambda k, b: (k, 0, 0, order))],
        out_specs=slab,
        out_shape=jax.ShapeDtypeStruct(a5.shape, bf16),
        scratch_shapes=[pltpu.VMEM((nk, 2 * N2, 2 * N2), bf16), pltpu.VMEM((nk, 2 * N2, 2 * N2), bf16)],
        compiler_params=_cparams("parallel", "parallel"),
        name="freq_stage",
    )(tab['f2r'], tab['f2i'], tab['twr'], tab['twi'], a5, hspec)


DFT_SLABS = 16


def _slabs_to_matrix(load_tile, rows, S, C, stage_ref):
    for lt in range(C // LANE):
        stage_ref[lt, :rows * S] = load_tile(lt)
    cols = [stage_ref[lt, pl.ds(sl, rows, stride=S), :] for sl in range(S) for lt in range(C // LANE)]
    return jnp.concatenate(cols, axis=1)


def _matrix_to_slabs(mat, rows, S, C, stage_ref):
    for sl in range(S):
        for lt in range(C // LANE):
            lo = sl * C + lt * LANE
            stage_ref[lt, pl.ds(sl, rows, stride=S), :] = mat[:, lo:lo + LANE]
    return [stage_ref[lt, :rows * S].reshape(rows, S, LANE) for lt in range(C // LANE)]


def _stage1_body(w_ref, z_ref, o_ref, stage_ref):
    H1, S, C = z_ref.shape
    K1p = o_ref.shape[1]
    zmat = _slabs_to_matrix(lambda lt: z_ref[:, :, lt * LANE:(lt + 1) * LANE].reshape(H1 * S, LANE), H1, S, C, stage_ref)
    a = jnp.dot(w_ref[...], zmat.astype(bf16), preferred_element_type=f32)
    for lt, tile in enumerate(_matrix_to_slabs(a, 2 * K1p, S, C, stage_ref)):
        o_ref[0, :, :, lt * LANE:(lt + 1) * LANE] = tile[:K1p].astype(bf16)
        o_ref[1, :, :, lt * LANE:(lt + 1) * LANE] = tile[K1p:].astype(bf16)


def _data_stage1(w, z4, K1p, C):
    B, R, N2, Cw = z4.shape
    nlb = Cw // C
    S = DFT_SLABS
    return pl.pallas_call(
        _stage1_body,
        grid=(B, nlb, N2 // S),
        in_specs=[_const_spec(w.shape), pl.BlockSpec((None, R, S, C), lambda b, lb, j: (b, 0, j, lb))],
        out_specs=pl.BlockSpec((None, 2, K1p, S, C), lambda b, lb, j: (b * nlb + lb, 0, 0, j, 0)),
        out_shape=jax.ShapeDtypeStruct((B * nlb, 2, K1p, N2, C), bf16),
        scratch_shapes=[pltpu.VMEM((C // LANE, max(R, 2 * K1p) * S, LANE), f32)],
        compiler_params=_cparams("parallel", "parallel", "parallel"),
        name="dft_data_stage1",
    )(w, z4)


def _conv_out_body(w_ref, g_ref, z_ref, gate_ref, bias_ref, o_ref, stage_ref):
    _, K1p, S, C = g_ref.shape
    H1 = z_ref.shape[0]

    def load_tile(lt):
        parts = [g_ref[ri, :, :, lt * LANE:(lt + 1) * LANE].astype(f32).reshape(K1p * S, LANE) for ri in range(2)]
        return jnp.concatenate(parts, axis=0)

    gmat = _slabs_to_matrix(load_tile, 2 * K1p, S, C, stage_ref)
    y = jnp.dot(w_ref[...], gmat.astype(bf16), preferred_element_type=f32)
    for lt, tile in enumerate(_matrix_to_slabs(y, H1, S, C, stage_ref)):
        sl = slice(lt * LANE, (lt + 1) * LANE)
        o_ref[:, :, sl] = gate_ref[:, :, sl] * (tile + z_ref[:, :, sl] * bias_ref[:, sl].reshape(1, 1, LANE))


def _conv_out(w, g5, z4, gate4, bias):
    B, H1, N2, C = z4.shape
    K1p = g5.shape[2]
    S = DFT_SLABS
    blk = pl.BlockSpec((None, H1, S, C), lambda b, j: (b, 0, j, 0))
    return pl.pallas_call(
        _conv_out_body,
        grid=(B, N2 // S),
        in_specs=[_const_spec(w.shape), pl.BlockSpec((None, 2, K1p, S, C), lambda b, j: (b, 0, 0, j, 0)), blk, blk,
                  _const_spec((1, C))],
        out_specs=blk,
        out_shape=jax.ShapeDtypeStruct(z4.shape, f32),
        scratch_shapes=[pltpu.VMEM((C // LANE, 2 * K1p * S, LANE), f32)],
        compiler_params=_cparams("parallel", "parallel"),
        name="conv_out",
    )(w, g5, z4, gate4, bias)


def _long_conv(v, g1, g2, circ, hy_bias, B, L, C):
    tab = _dft_tables(L)
    N2, N1, H1, K1p = DFT_N2, tab['N1'], tab['H1'], tab['K1p']
    hspec = _filter_spectrum(tab, _data_stage1(tab['f1'], circ.reshape(1, N1, N2, circ.shape[1]), K1p, C))
    z4 = v.reshape(B, H1, N2, C)
    gates = (g1.reshape(B, H1, N2, C), g2.reshape(B, H1, N2, C))
    f1h = tab['f1'][:, :H1]
    for o in range(HY_ORDER):
        g5 = _freq_stage(tab, _data_stage1(f1h, z4, K1p, C), hspec, o)
        z4 = _conv_out(tab['if1'], g5, z4, gates[o], hy_bias[o].reshape(1, C))
    return z4.reshape(B * L, C)


def _s5_rows(B, L):
    return (True, S5_ROWS, L // S5_ROWS) if B == 1 else (False, B, L)


def _encoder_layer(x, p, s5_mats, alpha):
    B, L, D = x.shape
    n = B * L
    x2 = x.reshape(n, D)
    G, P, H = p['s5_b_re'].shape[1:]
    s5w = G * H
    hyw = p['w_hy_proj'].shape[0]

    u_g, v, g1, g2, g_s, g_h = _inproj(x2, p['w_in'].astype(bf16), p['hy_short_w'], p['hy_short_b'], B, L, G, H, hyw)

    chained, Bs, _ = _s5_rows(B, L)
    y_g = _s5(u_g, s5_mats, Bs, chained)

    circ = _circ_filters(L, p['filt_w1'], p['filt_b1'], p['filt_freq1'], p['filt_w2'], p['filt_b2'],
                         p['filt_freq2'], p['filt_w3'], p['filt_decay'])
    z = _long_conv(v, g1, g2, circ, p['hy_bias'], B, L, hyw)

    x1, aff = _merge(x2, y_g, z, g_s, g_h, p['s5_w_glu'].astype(bf16), p['w_s5_proj'].astype(bf16),
                     p['w_hy_proj'].astype(bf16), p['w_out'].astype(bf16),
                     p['ln1_g'].reshape(1, D), p['ln1_b'].reshape(1, D), p['w_router'], B, L, alpha)

    out = _expert_choice(x1, aff, p['ex_w1_bf16'], p['ex_w3_bf16'], p['ex_w2_bf16'],
                         p['ln2_g'].reshape(1, D), p['ln2_b'].reshape(1, D), alpha)
    return out.reshape(B, L, D)


_PARAM_NAMES = ('w_in', 's5_lambda_re', 's5_lambda_im', 's5_log_dt', 's5_b_re', 's5_b_im',
                's5_c_re', 's5_c_im', 's5_d', 's5_w_glu', 'w_s5_proj',
                'hy_short_w', 'hy_short_b', 'filt_w1', 'filt_b1', 'filt_freq1', 'filt_w2', 'filt_b2',
                'filt_freq2', 'filt_w3', 'filt_decay', 'hy_bias', 'w_hy_proj', 'w_out',
                'ln1_g', 'ln1_b', 'w_router', 'ex_w1', 'ex_w3', 'ex_w2', 'ln2_g', 'ln2_b')


def kernel(x_prompt, x_sample, w_in, s5_lambda_re, s5_lambda_im, s5_log_dt, s5_b_re, s5_b_im, s5_c_re, s5_c_im, s5_d, s5_w_glu, w_s5_proj, hy_short_w, hy_short_b, filt_w1, filt_b1, filt_freq1, filt_w2, filt_b2, filt_freq2, filt_w3, filt_decay, hy_bias, w_hy_proj, w_out, ln1_g, ln1_b, w_router, ex_w1, ex_w3, ex_w2, ln2_g, ln2_b):
    stacked = (w_in, s5_lambda_re, s5_lambda_im, s5_log_dt, s5_b_re, s5_b_im, s5_c_re, s5_c_im, s5_d,
               s5_w_glu, w_s5_proj, hy_short_w, hy_short_b, filt_w1, filt_b1, filt_freq1, filt_w2,
               filt_b2, filt_freq2, filt_w3, filt_decay, hy_bias, w_hy_proj, w_out, ln1_g, ln1_b,
               w_router, ex_w1, ex_w3, ex_w2, ln2_g, ln2_b)
    depth = w_in.shape[0]
    alpha = (2.0 * depth) ** 0.25
    xs = [x_prompt, x_sample]
    for l in range(depth):
        p = {k: v[l] for k, v in zip(_PARAM_NAMES, stacked)}
        for k in ('ex_w1', 'ex_w3', 'ex_w2'):
            p[k + '_bf16'] = p[k].astype(bf16)
        s5_mats = _s5_matrices(p, [_s5_rows(x.shape[0], x.shape[1])[2] for x in xs])
        xs = [_encoder_layer(x, p, mats, alpha) for x, mats in zip(xs, s5_mats)]
    return tuple(xs)
```

```python
import functools
import math

import jax
import jax.numpy as jnp
import numpy as np
from jax import lax
from jax.experimental import pallas as pl
from jax.experimental.pallas import tpu as pltpu

FILT_BANDS = 8
FILT_SHIFT = 0.05
EC_CAPACITY = 2
LN_EPS = 1e-5
HY_ORDER = 2

VMEM_LIMIT_BYTES = 56 * 1024 * 1024
ROW_TILE = 512

bf16 = jnp.bfloat16
f32 = jnp.float32


def _cparams(*sem):
    return pltpu.CompilerParams(dimension_semantics=sem, vmem_limit_bytes=VMEM_LIMIT_BYTES)


def _const_spec(shape):
    return pl.BlockSpec(shape, lambda *_: (0,) * len(shape))


def _inproj_body(xp_ref, x_ref, xn_ref, w_ref, cw_ref, cb_ref, ug_ref, v_ref, g1_ref, g2_ref, gs_ref, gh_ref,
                 stage_ref, *, o1, o2, o3, nt, H):
    i = pl.program_id(1)
    proj = jnp.dot(x_ref[...].astype(bf16), w_ref[...], preferred_element_type=f32)
    _rows_to_chunks(proj[:, :o1], stage_ref, ug_ref, H)
    gs_ref[...] = jax.nn.sigmoid(proj[:, o2:o3])
    gh_ref[...] = jax.nn.sigmoid(proj[:, o3:])
    u = proj[:, o1:o2]
    w_hy = w_ref[:, o1:o2]
    prev_row = jnp.dot(xp_ref[...].astype(bf16), w_hy, preferred_element_type=f32)[7:8]
    next_row = jnp.dot(xn_ref[...].astype(bf16), w_hy, preferred_element_type=f32)[0:1]
    prev_row = jnp.where(i == 0, 0.0, prev_row)
    next_row = jnp.where(i == nt - 1, 0.0, next_row)
    tl = u.shape[0]
    rows = lax.broadcasted_iota(jnp.int32, u.shape, 0)
    um1 = jnp.where(rows == 0, prev_row, pltpu.roll(u, 1, axis=0))
    up1 = jnp.where(rows == tl - 1, next_row, pltpu.roll(u, tl - 1, axis=0))
    cw = cw_ref[...]
    hy = um1 * cw[0:1, :] + u * cw[1:2, :] + up1 * cw[2:3, :] + cb_ref[...]
    hyw = v_ref.shape[1]
    v_ref[...] = hy[:, :hyw]
    g1_ref[...] = hy[:, hyw:2 * hyw]
    g2_ref[...] = hy[:, 2 * hyw:]


def _inproj(x2, w_in, conv_w, conv_b, B, L, G, H, hyw):
    n, D = x2.shape
    cols = w_in.shape[1]
    s5w = G * H
    o1, o2 = s5w, s5w + 3 * hyw
    o3 = o2 + D
    tm = min(ROW_TILE, L)
    nt = L // tm
    r8 = tm // 8
    row = lambda b, i: (b * nt + i, 0)
    prev = lambda b, i: (jnp.maximum((b * nt + i) * r8 - 1, 0), 0)
    nxt = lambda b, i: (jnp.minimum((b * nt + i + 1) * r8, n // 8 - 1), 0)
    hy_out = jax.ShapeDtypeStruct((n, hyw), f32)
    return pl.pallas_call(
        functools.partial(_inproj_body, o1=o1, o2=o2, o3=o3, nt=nt, H=H),
        grid=(B, nt),
        in_specs=[pl.BlockSpec((8, D), prev), pl.BlockSpec((tm, D), row), pl.BlockSpec((8, D), nxt),
                  _const_spec((D, cols)), _const_spec((3, 3 * hyw)), _const_spec((1, 3 * hyw))],
        out_specs=[pl.BlockSpec((G, tm // S5_CHUNK, S5_CHUNK * H), lambda b, i: (0, b * nt + i, 0)),
                   pl.BlockSpec((tm, hyw), row), pl.BlockSpec((tm, hyw), row), pl.BlockSpec((tm, hyw), row),
                   pl.BlockSpec((tm, D), row), pl.BlockSpec((tm, D), row)],
        out_shape=[jax.ShapeDtypeStruct((G, n // S5_CHUNK, S5_CHUNK * H), bf16), hy_out, hy_out, hy_out,
                   jax.ShapeDtypeStruct((n, D), f32), jax.ShapeDtypeStruct((n, D), f32)],
        scratch_shapes=[pltpu.VMEM((s5w // LANE, tm, LANE), f32)],
        compiler_params=_cparams("parallel", "parallel"),
        name="inproj",
    )(x2, x2, x2, w_in, conv_w, conv_b.reshape(1, 3 * hyw))


S5_CHUNK = 16
S5_ROWS = 8
LANE = 128


def _s5_body(u_ref, kmat_ref, winf_ref, winb_ref, woutf_ref, woutb_ref, af_ref, ab_ref, y_ref,
             vf_ref, vb_ref, sf_ref, sb_ref, *, R, nc, chained):
    u = u_ref[0]
    hw = sf_ref.shape[1]
    seq_rows = lambda b: pl.ds(b, nc, stride=R)
    for v_ref, w_ref in ((vf_ref, winf_ref), (vb_ref, winb_ref)):
        v = jnp.dot(u, w_ref[0], preferred_element_type=f32)
        for b in range(R):
            v_ref[0, seq_rows(b), :] = v[b * nc:(b + 1) * nc, :hw]
            v_ref[1, seq_rows(b), :] = v[b * nc:(b + 1) * nc, hw:]
    bc = lambda ref, row, lo: jnp.broadcast_to(ref[0, row:row + 1, lo:lo + hw], (R, hw))
    decay = lambda ref, row: ((bc(ref, row, 0), bc(ref, row, hw)), (bc(ref, row + 1, 0), bc(ref, row + 1, hw)))
    f_a1, f_a2 = decay(af_ref, 0)
    b_a1, b_a2 = decay(ab_ref, 0)

    def mul_add(s0, s1, a1, a2, v0, v1):
        return a1[0] * s0 + a2[0] * s1 + v0, a1[1] * s1 + a2[1] * s0 + v1

    def scan(init, store):
        def step(i, carry):
            f0, f1, b0, b1 = carry
            rf = pl.ds(pl.multiple_of(i * R, R), R)
            rb = pl.ds(pl.multiple_of((nc - 1 - i) * R, R), R)
            if store:
                sf_ref[rf, :] = f0
                sb_ref[rb, :] = b0
            f0, f1 = mul_add(f0, f1, f_a1, f_a2, vf_ref[0, rf, :], vf_ref[1, rf, :])
            b0, b1 = mul_add(b0, b1, b_a1, b_a2, vb_ref[0, rb, :], vb_ref[1, rb, :])
            return f0, f1, b0, b1
        return lax.fori_loop(0, nc, step, init, unroll=4)

    zero = jnp.zeros((R, hw), f32)
    init = (zero, zero, zero, zero)
    if chained:
        ef0, ef1, eb0, eb1 = scan(init, store=False)
        fn_a1, fn_a2 = decay(af_ref, 2)
        bn_a1, bn_a2 = decay(ab_ref, 2)
        row = lax.broadcasted_iota(jnp.int32, (R, hw), 0)
        down = lambda v: jnp.where(row == 0, 0.0, pltpu.roll(v, 1, axis=0))
        up = lambda v: jnp.where(row == R - 1, 0.0, pltpu.roll(v, R - 1, axis=0))
        f0, f1, b0, b1 = init
        for _ in range(R - 1):
            t0, t1 = mul_add(f0, f1, fn_a1, fn_a2, ef0, ef1)
            f0, f1 = down(t0), down(t1)
            t0, t1 = mul_add(b0, b1, bn_a1, bn_a2, eb0, eb1)
            b0, b1 = up(t0), up(t1)
        init = (f0, f1, b0, b1)
    scan(init, store=True)
    y = jnp.dot(u, kmat_ref[0], preferred_element_type=f32)
    for s_ref, w_ref in ((sf_ref, woutf_ref), (sb_ref, woutb_ref)):
        states = jnp.concatenate([s_ref[seq_rows(b), :] for b in range(R)], axis=0)
        y = y + jnp.dot(states.astype(bf16), w_ref[0], preferred_element_type=f32)
    y_ref[0] = y


def _s5(u_g, mats, R, chained):
    G, M, W = u_g.shape
    kmat, winf, winb, woutf, woutb, af, ab = mats
    hw = woutf.shape[1]
    grp = lambda shape: pl.BlockSpec((1,) + shape, lambda g: (g, 0, 0))
    return pl.pallas_call(
        functools.partial(_s5_body, R=R, nc=M // R, chained=chained),
        grid=(G,),
        in_specs=[grp((M, W)), grp((W, W)), grp((W, 2 * hw)), grp((W, 2 * hw)), grp((hw, W)), grp((hw, W)),
                  grp((4, 2 * hw)), grp((4, 2 * hw))],
        out_specs=grp((M, W)),
        out_shape=jax.ShapeDtypeStruct((G, M, W), f32),
        scratch_shapes=[pltpu.VMEM((2, M, hw), f32), pltpu.VMEM((2, M, hw), f32),
                        pltpu.VMEM((M, hw), f32), pltpu.VMEM((M, hw), f32)],
        compiler_params=_cparams("parallel"),
        name="s5_chunked",
    )(u_g, kmat, winf, winb, woutf, woutb, af, ab)


def _rows_to_chunks(x, stage_ref, o_ref, H):
    Tc = S5_CHUNK
    nr = x.shape[0] // Tc
    gpt, per_tile = LANE // H, LANE // H
    for lt in range(x.shape[1] // LANE):
        stage_ref[lt] = x[:, lt * LANE:(lt + 1) * LANE]
        rows_t = [stage_ref[lt, pl.ds(t, nr, stride=Tc), :] for t in range(Tc)]
        for g in range(gpt):
            for j in range(Tc // per_tile):
                tile = jnp.concatenate([rows_t[per_tile * j + k][:, H * g:H * (g + 1)] for k in range(per_tile)], axis=1)
                o_ref[lt * gpt + g, :, j * LANE:(j + 1) * LANE] = tile.astype(o_ref.dtype)


def _chunks_to_rows(y_ref, stage_ref, H):
    Tc = S5_CHUNK
    nr = y_ref.shape[1]
    gpt = LANE // H
    for lt in range(stage_ref.shape[0]):
        for t in range(Tc):
            tile = jnp.concatenate([y_ref[lt * gpt + g, :, H * t:H * (t + 1)] for g in range(gpt)], axis=1)
            stage_ref[lt, pl.ds(t, nr, stride=Tc), :] = tile
    return jnp.concatenate([stage_ref[lt] for lt in range(stage_ref.shape[0])], axis=1)


def _s5_discretise(lam_re, lam_im, log_dt, b_re, b_im, powers):
    lam = lax.complex(-jnp.abs(lam_re.astype(f32)), lam_im.astype(f32))
    dt = jnp.exp(log_dt.astype(f32))[:, None]
    a_bar = jnp.exp(lam * dt)
    k = jnp.asarray(powers, f32)[None, :, None]
    apow = jnp.exp((lam * dt)[:, None, :] * k)
    b_bar = ((a_bar - 1.0) / lam)[..., None] * lax.complex(b_re.astype(f32), b_im.astype(f32))
    return apow, b_bar


def _s5_matrices(p, seg_steps):
    Tc = S5_CHUNK
    powers = list(range(Tc + 1)) + list(seg_steps)
    apf, bbf = _s5_discretise(p['s5_lambda_re'][0], p['s5_lambda_im'][0], p['s5_log_dt'][0],
                              p['s5_b_re'][0], p['s5_b_im'][0], powers)
    apb, bbb = _s5_discretise(p['s5_lambda_re'][1], p['s5_lambda_im'][1], p['s5_log_dt'][1],
                              p['s5_b_re'][1], p['s5_b_im'][1], powers)
    c = lax.complex(p['s5_c_re'].astype(f32), p['s5_c_im'].astype(f32))
    G, H, P = c.shape
    W = Tc * H
    kf = jnp.real(jnp.einsum('ghp,gtp,gpk->gthk', c, apf[:, :Tc], bbf))
    kb = jnp.real(jnp.einsum('ghp,gtp,gpk->gthk', c, apb[:, :Tc], bbb))
    tau = jnp.arange(Tc)[None, :] - jnp.arange(Tc)[:, None]
    blk = (jnp.where((tau >= 0)[None, :, :, None, None], kf[:, jnp.abs(tau)], 0.0)
           + jnp.where((tau <= 0)[None, :, :, None, None], kb[:, jnp.abs(tau)], 0.0))
    kmat = jnp.transpose(blk, (0, 1, 4, 2, 3)).reshape(G, W, W)
    kmat = kmat + jnp.eye(W, dtype=f32)[None] * jnp.tile(p['s5_d'].astype(f32), (1, Tc))[:, None, :]
    lanes = lambda z: jnp.concatenate([jnp.real(z), jnp.imag(z), jnp.imag(z), jnp.real(z)], axis=-1)
    win = lambda ap, bb: lanes(jnp.einsum('gsp,gpk->gskp', ap, bb)).reshape(G, W, 4 * P)
    winf = win(apf[:, Tc - 1::-1][:, :Tc], bbf)
    winb = win(apb[:, :Tc], bbb)
    def wout(ap):
        z = jnp.einsum('ghp,gtp->gpth', c, ap).reshape(G, P, W)
        return jnp.concatenate([jnp.real(z), -jnp.imag(z)], axis=1)
    woutf = wout(apf[:, 1:Tc + 1])
    woutb = wout(apb[:, Tc:0:-1])
    def chunk_decay(ap, seg):
        rows = []
        for k in (Tc, Tc + 1 + seg):
            ar, ai = jnp.real(ap[:, k]), jnp.imag(ap[:, k])
            rows += [jnp.concatenate([ar, ar, ar, ar], -1), jnp.concatenate([-ai, ai, ai, -ai], -1)]
        return jnp.stack(rows, axis=1)
    cast = lambda m: m.astype(bf16)
    shared = (cast(kmat), cast(winf), cast(winb), cast(woutf), cast(woutb))
    return [shared + (chunk_decay(apf, i), chunk_decay(apb, i)) for i in range(len(seg_steps))]


MERGE_SPLIT = 2


def _layer_norm(v, g, b):
    mu = jnp.mean(v, axis=-1, keepdims=True)
    c = v - mu
    var = jnp.mean(c * c, axis=-1, keepdims=True)
    return c * lax.rsqrt(var + LN_EPS) * g + b


def _merge_body(x_ref, yg_ref, z_ref, gs_ref, gh_ref, wglu_ref, wsp_ref, whp_ref, wout_ref,
                g_ref, b_ref, wrh_ref, wrl_ref, x1_ref, aff_ref, stage_ref, *, alpha, H):
    ys_all = jax.nn.gelu(_chunks_to_rows(yg_ref, stage_ref, H))
    tm = x_ref.shape[0]
    for r0 in range(0, tm, tm // MERGE_SPLIT):
        rs = slice(r0, r0 + tm // MERGE_SPLIT)
        ys = ys_all[rs]
        gate = jax.nn.sigmoid(jnp.dot(ys.astype(bf16), wglu_ref[...], preferred_element_type=f32))
        branch_s = jnp.dot((ys * gate).astype(bf16), wsp_ref[...], preferred_element_type=f32)
        branch_h = jnp.dot(z_ref[rs, :].astype(bf16), whp_ref[...], preferred_element_type=f32)
        mix = gs_ref[rs, :] * branch_s + gh_ref[rs, :] * branch_h
        mix = jnp.dot(mix.astype(bf16), wout_ref[...], preferred_element_type=f32)
        x1 = _layer_norm(alpha * x_ref[rs, :] + mix, g_ref[...], b_ref[...])
        x1_ref[rs, :] = x1
        x1_hi = x1.astype(bf16)
        x1_lo = (x1 - x1_hi.astype(f32)).astype(bf16)
        logits = (jnp.dot(x1_hi, wrh_ref[...], preferred_element_type=f32)
                  + jnp.dot(x1_lo, wrh_ref[...], preferred_element_type=f32)
                  + jnp.dot(x1_hi, wrl_ref[...], preferred_element_type=f32))
        m = jnp.max(logits, axis=-1, keepdims=True)
        e = jnp.exp(logits - m)
        aff_ref[rs, :] = e / jnp.sum(e, axis=-1, keepdims=True)


def _merge(x2, y_g, z, g_s, g_h, w_glu, w_sp, w_hp, w_out, ln_g, ln_b, w_router, B, L, alpha):
    n, D = x2.shape
    G, _, W = y_g.shape
    H = W // S5_CHUNK
    s5w, hyw = G * H, z.shape[1]
    E = w_router.shape[1]
    tm = min(ROW_TILE, L)
    nt = L // tm
    row = lambda b, i: (b * nt + i, 0)
    wr_hi = w_router.astype(bf16)
    wr_lo = (w_router - wr_hi.astype(f32)).astype(bf16)
    return pl.pallas_call(
        functools.partial(_merge_body, alpha=alpha, H=H),
        grid=(B, nt),
        in_specs=[pl.BlockSpec((tm, D), row),
                  pl.BlockSpec((G, tm // S5_CHUNK, W), lambda b, i: (0, b * nt + i, 0)),
                  pl.BlockSpec((tm, hyw), row),
                  pl.BlockSpec((tm, D), row),
                  pl.BlockSpec((tm, D), row),
                  _const_spec(w_glu.shape), _const_spec(w_sp.shape), _const_spec(w_hp.shape),
                  _const_spec(w_out.shape), _const_spec((1, D)), _const_spec((1, D)),
                  _const_spec(w_router.shape), _const_spec(w_router.shape)],
        out_specs=[pl.BlockSpec((tm, D), row), pl.BlockSpec((tm, E), row)],
        out_shape=[jax.ShapeDtypeStruct((n, D), f32), jax.ShapeDtypeStruct((n, E), f32)],
        scratch_shapes=[pltpu.VMEM((s5w // LANE, tm, LANE), f32)],
        compiler_params=_cparams("parallel", "parallel"),
        name="merge",
    )(x2, y_g, z, g_s, g_h, w_glu, w_sp, w_hp, w_out, ln_g, ln_b, wr_hi, wr_lo)


ROUTE_BLOCK = 256
ROUTE_SLOTS = 64
ROW_ALIGN = 8
GATE_LANES = 128


def _select_body(aff_ref, sel_ref, *, cap, idx_bits):
    bits = pltpu.bitcast(aff_ref[...], jnp.int32)
    E = bits.shape[0]
    count = lambda m: jnp.sum(jnp.where(m, 1.0, 0.0), axis=1, keepdims=True)

    def value_bit(i, prefix):
        cand = prefix | jnp.left_shift(jnp.int32(1), 30 - i)
        return jnp.where(count(bits >= cand) >= cap, cand, prefix)

    thr = lax.fori_loop(0, 31, value_bit, jnp.zeros((E, 1), jnp.int32))
    need = cap - count(bits > thr)
    idx = lax.broadcasted_iota(jnp.int32, bits.shape, 1)
    tie_idx = jnp.where(bits == thr, idx, jnp.int32(2 ** 30))

    def index_bit(i, bound):
        cand = bound | jnp.left_shift(jnp.int32(1), idx_bits - 1 - i)
        return jnp.where(count(tie_idx < cand) <= need, cand, bound)

    bound = lax.fori_loop(0, idx_bits, index_bit, jnp.zeros((E, 1), jnp.int32))
    sel_ref[...] = jnp.where(bits > thr, 1.0, jnp.where(tie_idx < bound, 1.0, 0.0))


def _select(aff_t, cap):
    E, n = aff_t.shape
    return pl.pallas_call(
        functools.partial(_select_body, cap=float(cap), idx_bits=int(n).bit_length()),
        out_shape=jax.ShapeDtypeStruct((E, n), f32),
        compiler_params=pltpu.CompilerParams(vmem_limit_bytes=VMEM_LIMIT_BYTES),
        name="expert_select",
    )(aff_t)


def _slot_onehot(sel, p_ref, first_slot, E):
    Tb = sel.shape[1]
    S = p_ref.shape[0] // E
    r = lax.broadcasted_iota(jnp.int32, (Tb, Tb), 0)
    c = lax.broadcasted_iota(jnp.int32, (Tb, Tb), 1)
    tri = jnp.where(r <= c, 1.0, 0.0).astype(bf16)
    incl = jnp.dot(sel.astype(bf16), tri, preferred_element_type=f32)
    slot = jnp.where(sel > 0.0, incl - 1.0, -1.0)
    want = (lax.broadcasted_iota(jnp.int32, (S, Tb), 0) + first_slot).astype(f32)
    for e in range(E):
        p_ref[e * S:(e + 1) * S, :] = jnp.where(slot[e:e + 1] == want, 1.0, 0.0).astype(bf16)


def _gather_copies(stage_ref, xe_hbm, sem, off_ref, j, first_slot, E, S, nblk):
    copies = []
    for e in range(E):
        row = pl.multiple_of(off_ref[e * nblk + j] + first_slot, ROW_ALIGN)
        copies.append(pltpu.make_async_copy(stage_ref.at[e * S:(e + 1) * S], xe_hbm.at[e, pl.ds(row, S)], sem))
    return copies


def _gather_body(off_ref, rounds_ref, total_ref, sel_ref, x_ref, aff_ref, xe_hbm, p_ref, stage_ref, sem, nwrites_ref,
                 *, E, nblk, cap):
    j = pl.program_id(0)
    S = p_ref.shape[0] // E
    D = x_ref.shape[1]

    @pl.when(j == 0)
    def _():
        nwrites_ref[0] = 0

    x = x_ref[...].astype(bf16)
    aff = aff_ref[...]
    a_hi = aff.astype(bf16)
    rem = aff - a_hi.astype(f32)
    a_mid = rem.astype(bf16)
    a_lo = (rem - a_mid.astype(f32)).astype(bf16)
    aff3 = jnp.concatenate([a_hi, a_mid, a_lo], axis=1)
    own = (lax.broadcasted_iota(jnp.int32, (E * S, 3 * E), 0) // S) == (
        lax.broadcasted_iota(jnp.int32, (E * S, 3 * E), 1) % E)

    def one_round(r):
        w = nwrites_ref[0]
        slot = lax.rem(w, 2)
        first_slot = r * S
        _slot_onehot(sel_ref[...], p_ref, first_slot, E)
        p = p_ref[...]
        stage_ref[slot, :, :D] = jnp.dot(p, x, preferred_element_type=f32)
        gates = jnp.dot(p, aff3, preferred_element_type=f32)
        gate = jnp.sum(jnp.where(own, gates, 0.0), axis=1, keepdims=True)
        stage_ref[slot, :, D:] = jnp.broadcast_to(gate, (E * S, GATE_LANES))

        @pl.when(w > 0)
        def _():
            for cp in _gather_copies(stage_ref.at[1 - slot], xe_hbm, sem.at[1 - slot], off_ref, j, 0, E, S, nblk):
                cp.wait()

        for cp in _gather_copies(stage_ref.at[slot], xe_hbm, sem.at[slot], off_ref, j, first_slot, E, S, nblk):
            cp.start()
        nwrites_ref[0] = w + 1

    one_round(0)

    def extra(r, carry):
        one_round(r)
        return carry

    lax.fori_loop(1, rounds_ref[j], extra, 0)

    @pl.when(j == nblk - 1)
    def _():
        last = lax.rem(nwrites_ref[0] - 1, 2)
        for cp in _gather_copies(stage_ref.at[last], xe_hbm, sem.at[last], off_ref, j, 0, E, S, nblk):
            cp.wait()
        rows_pad = xe_hbm.shape[1]
        zeros_ref = stage_ref.at[0, :S]
        zeros_ref[...] = jnp.zeros((S, D + GATE_LANES), f32)
        nfill = -(-(rows_pad - cap) // S)

        def fill(e, row):
            return pltpu.make_async_copy(zeros_ref, xe_hbm.at[e, pl.ds(pl.multiple_of(row, ROW_ALIGN), S)], sem.at[0])

        def whole_chunks(e, action):
            def body(k, c):
                row = total_ref[e] + k * S

                @pl.when(row + S <= rows_pad)
                def _():
                    action(fill(e, row))
                return c
            lax.fori_loop(0, nfill, body, 0)

        for e in range(E):
            whole_chunks(e, lambda cp: cp.start())
        for e in range(E):
            whole_chunks(e, lambda cp: cp.wait())
        for e in range(E):
            fill(e, rows_pad - S).start()
        for e in range(E):
            fill(e, rows_pad - S).wait()


def _route_gather(sel, x1, aff, off, rounds, total, rows_pad):
    E, n = sel.shape
    D = x1.shape[1]
    Tb, S = ROUTE_BLOCK, ROUTE_SLOTS
    nblk = n // Tb
    return pl.pallas_call(
        functools.partial(_gather_body, E=E, nblk=nblk, cap=EC_CAPACITY * n // E),
        grid_spec=pltpu.PrefetchScalarGridSpec(
            num_scalar_prefetch=3,
            grid=(nblk,),
            in_specs=[pl.BlockSpec((E, Tb), lambda j, *_: (0, j)),
                      pl.BlockSpec((Tb, D), lambda j, *_: (j, 0)),
                      pl.BlockSpec((Tb, E), lambda j, *_: (j, 0))],
            out_specs=pl.BlockSpec(memory_space=pl.ANY),
            scratch_shapes=[pltpu.VMEM((E * S, Tb), bf16),
                            pltpu.VMEM((2, E * S, D + GATE_LANES), f32),
                            pltpu.SemaphoreType.DMA((2,)),
                            pltpu.SMEM((1,), jnp.int32)]),
        out_shape=jax.ShapeDtypeStruct((E, rows_pad, D + GATE_LANES), f32),
        compiler_params=_cparams("arbitrary"),
        name="route_gather",
    )(off, rounds, total, sel, x1, aff)


def _ffn_body(total_ref, xe_ref, w1_ref, w3_ref, w2_ref, ye_ref):
    e, r = pl.program_id(0), pl.program_id(1)
    tr = xe_ref.shape[1]
    D = ye_ref.shape[2]

    @pl.when(r * tr < total_ref[e])
    def _():
        xe = xe_ref[0, :, :D].astype(bf16)
        gate = xe_ref[0, :, D:D + 1]
        h1 = jnp.dot(xe, w1_ref[0], preferred_element_type=f32)
        h3 = jnp.dot(xe, w3_ref[0], preferred_element_type=f32)
        h = (jax.nn.silu(h1) * h3).astype(bf16)
        ye = jnp.dot(h, w2_ref[0], preferred_element_type=f32) * gate
        row = r * tr + lax.broadcasted_iota(jnp.int32, ye.shape, 0)
        ye_ref[0] = jnp.where(row < total_ref[e], ye, 0.0)

    @pl.when(r * tr >= total_ref[e])
    def _():
        ye_ref[...] = jnp.zeros_like(ye_ref)


def _ffn(total, xe, w1, w3, w2):
    E, rows_pad, Dx = xe.shape
    D, F = w1.shape[1:]
    tr = ROW_TILE
    last = lambda e, tot: (tot[e] - 1) // tr
    rowmap = lambda e, r, tot: (e, jnp.minimum(r, last(e, tot)), 0)
    wmap = lambda e, r, tot: (e, 0, 0)
    return pl.pallas_call(
        _ffn_body,
        grid_spec=pltpu.PrefetchScalarGridSpec(
            num_scalar_prefetch=1,
            grid=(E, rows_pad // tr),
            in_specs=[pl.BlockSpec((1, tr, Dx), rowmap),
                      pl.BlockSpec((1, D, F), wmap), pl.BlockSpec((1, D, F), wmap),
                      pl.BlockSpec((1, F, D), wmap)],
            out_specs=pl.BlockSpec((1, tr, D), lambda e, r, tot: (e, r, 0))),
        out_shape=jax.ShapeDtypeStruct((E, rows_pad, D), f32),
        compiler_params=_cparams("arbitrary", "arbitrary"),
        name="expert_ffn",
    )(total, xe, w1, w3, w2)


def _scatter_copies(ye_hbm, buf_ref, sem, off_ref, j, first_slot, E, S, nblk):
    copies = []
    for e in range(E):
        row = pl.multiple_of(off_ref[e * nblk + j] + first_slot, ROW_ALIGN)
        copies.append(pltpu.make_async_copy(ye_hbm.at[e, pl.ds(row, S)], buf_ref.at[e * S:(e + 1) * S], sem))
    return copies


def _scatter_body(off_ref, rounds_ref, sel_ref, x1_ref, g_ref, b_ref, ye_hbm, o_ref, p_ref, buf_ref, sem,
                  *, E, nblk, alpha):
    j = pl.program_id(0)
    S = p_ref.shape[0] // E
    slot = lax.rem(j, 2)

    def fetch(jj, first_slot, s):
        return _scatter_copies(ye_hbm, buf_ref.at[s], sem.at[s], off_ref, jj, first_slot, E, S, nblk)

    @pl.when(j == 0)
    def _():
        for cp in fetch(j, 0, 0):
            cp.start()

    @pl.when(j + 1 < nblk)
    def _():
        for cp in fetch(j + 1, 0, 1 - slot):
            cp.start()

    def one_round(r, s):
        _slot_onehot(sel_ref[...], p_ref, r * S, E)
        ye = buf_ref[s].astype(bf16)
        return lax.dot_general(p_ref[...], ye, (((0,), (0,)), ((), ())), preferred_element_type=f32)

    for cp in fetch(j, 0, slot):
        cp.wait()
    moe = one_round(jnp.int32(0), slot)

    def extra(r, acc):
        for cp in fetch(j, r * S, 2):
            cp.start()
        for cp in fetch(j, r * S, 2):
            cp.wait()
        return acc + one_round(r, 2)

    moe = lax.fori_loop(1, rounds_ref[j], extra, moe)
    o_ref[...] = _layer_norm(alpha * x1_ref[...] + moe, g_ref[...], b_ref[...])


def _route_scatter(sel, x1, ye, off, rounds, ln_g, ln_b, alpha):
    E, n = sel.shape
    D = x1.shape[1]
    Tb, S = ROUTE_BLOCK, ROUTE_SLOTS
    nblk = n // Tb
    return pl.pallas_call(
        functools.partial(_scatter_body, E=E, nblk=nblk, alpha=alpha),
        grid_spec=pltpu.PrefetchScalarGridSpec(
            num_scalar_prefetch=2,
            grid=(nblk,),
            in_specs=[pl.BlockSpec((E, Tb), lambda j, *_: (0, j)),
                      pl.BlockSpec((Tb, D), lambda j, *_: (j, 0)),
                      pl.BlockSpec((1, D), lambda j, *_: (0, 0)),
                      pl.BlockSpec((1, D), lambda j, *_: (0, 0)),
                      pl.BlockSpec(memory_space=pl.ANY)],
            out_specs=pl.BlockSpec((Tb, D), lambda j, *_: (j, 0)),
            scratch_shapes=[pltpu.VMEM((E * S, Tb), bf16),
                            pltpu.VMEM((3, E * S, D), f32),
                            pltpu.SemaphoreType.DMA((3,))]),
        out_shape=jax.ShapeDtypeStruct((n, D), f32),
        compiler_params=_cparams("arbitrary"),
        name="route_scatter",
    )(off, rounds, sel, x1, ln_g, ln_b, ye)


def _expert_choice(x1, aff, w1, w3, w2, ln_g, ln_b, alpha):
    n, D = x1.shape
    E = aff.shape[1]
    cap = EC_CAPACITY * n // E
    Tb, S = ROUTE_BLOCK, ROUTE_SLOTS
    nblk = n // Tb
    sel = _select(aff.T, cap)
    cnt = sel.reshape(E, nblk, Tb).sum(-1).astype(jnp.int32)
    cnt_al = (cnt + ROW_ALIGN - 1) // ROW_ALIGN * ROW_ALIGN
    end = jnp.cumsum(cnt_al, axis=1)
    off = (end - cnt_al).reshape(E * nblk)
    total = end[:, -1]
    rounds = jnp.maximum((jnp.max(cnt, axis=0) + S - 1) // S, 1)
    rows_pad = -(-(cap + ROW_ALIGN * nblk + Tb) // ROW_TILE) * ROW_TILE
    xe = _route_gather(sel, x1, aff, off, rounds, total, rows_pad)
    ye = _ffn(total, xe, w1, w3, w2)
    return _route_scatter(sel, x1, ye, off, rounds, ln_g, ln_b, alpha)


FILT_FEAT_PAD = 32


def _lag_features(L):
    n = np.arange(2 * L, dtype=np.float64)
    pos = np.where(n < L, n, 2 * L - n)
    t = pos / (L - 1)
    bands = np.linspace(1e-4, FILT_BANDS - 1, FILT_BANDS)
    ang = 2.0 * math.pi * (pos / L)[None, :] * bands[:, None]
    z = np.concatenate([t[None, :], np.cos(ang), -np.sin(ang)], axis=0)
    out = np.zeros((FILT_FEAT_PAD, 2 * L), np.float32)
    out[:z.shape[0]] = z
    return out, t.astype(np.float32)[:, None]


def _filter_body(z_ref, t_ref, w1_ref, b1_ref, f1_ref, w2_ref, b2_ref, f2_ref, w3_ref, dec_ref, o_ref, *, L):
    hp = lax.Precision.HIGHEST
    h = jnp.sin(f1_ref[...] * (jnp.dot(w1_ref[...], z_ref[...], precision=hp, preferred_element_type=f32) + b1_ref[...]))
    h = jnp.sin(f2_ref[...] * (jnp.dot(w2_ref[...], h, precision=hp, preferred_element_type=f32) + b2_ref[...]))
    h = lax.dot_general(h, w3_ref[...], (((0,), (0,)), ((), ())), precision=hp, preferred_element_type=f32)
    window = jnp.exp(-t_ref[...] * jnp.abs(dec_ref[...])) + FILT_SHIFT
    tl = h.shape[0]
    row = pl.program_id(0) * tl + lax.broadcasted_iota(jnp.int32, h.shape, 0)
    o_ref[...] = jnp.where(row == L, 0.0, h * window)


def _circ_filters(L, w1, b1, f1, w2, b2, f2, w3, decay):
    nh = w1.shape[1]
    oc = w3.shape[1] // 2
    feats, t_col = _lag_features(L)
    w1t = jnp.zeros((nh, FILT_FEAT_PAD), f32).at[:, :w1.shape[0]].set(w1.T)
    tl = min(ROW_TILE, L)
    half = L // tl
    dirsel = lambda i: (i // half, 0, 0)
    col = lambda a: a.reshape(nh, 1)
    return pl.pallas_call(
        functools.partial(_filter_body, L=L),
        grid=(2 * L // tl,),
        in_specs=[pl.BlockSpec((FILT_FEAT_PAD, tl), lambda i: (0, i)), pl.BlockSpec((tl, 1), lambda i: (i, 0)),
                  _const_spec((nh, FILT_FEAT_PAD)), _const_spec((nh, 1)), _const_spec((nh, 1)),
                  _const_spec((nh, nh)), _const_spec((nh, 1)), _const_spec((nh, 1)),
                  pl.BlockSpec((None, nh, oc), dirsel),
                  pl.BlockSpec((None, 1, oc), dirsel)],
        out_specs=pl.BlockSpec((tl, oc), lambda i: (i, 0)),
        out_shape=jax.ShapeDtypeStruct((2 * L, oc), f32),
        compiler_params=_cparams("parallel"),
        name="hyena_filter",
    )(jnp.asarray(feats), jnp.asarray(t_col), w1t, col(b1), col(f1), w2.T, col(b2), col(f2),
      w3.reshape(nh, 2, oc).transpose(1, 0, 2), decay.reshape(2, 1, oc))


DFT_N2 = 128
FREQ_CHAINS = 8


def _dft_tables(L):
    N, N2 = 2 * L, DFT_N2
    N1 = N // N2
    H1 = N1 // 2
    K1n = H1 + 1
    K1p = -(-K1n // 8) * 8
    k1 = np.arange(K1n)
    n1 = np.arange(N1)
    ang = 2.0 * math.pi * ((k1[:, None] * n1[None, :]) % N1) / N1
    f1 = np.zeros((2 * K1p, N1))
    f1[:K1n] = np.cos(ang)
    f1[K1p:K1p + K1n] = -np.sin(ang)
    wgt = np.full(K1n, 2.0)
    wgt[0] = wgt[H1] = 1.0
    ang = 2.0 * math.pi * ((np.arange(H1)[:, None] * k1[None, :]) % N1) / N1
    if1 = np.zeros((H1, 2 * K1p))
    if1[:, :K1n] = wgt * np.cos(ang) / N
    if1[:, K1p:K1p + K1n] = -wgt * np.sin(ang) / N
    k2 = np.arange(N2)
    ang = 2.0 * math.pi * ((k2[:, None] * k2[None, :]) % N2) / N2
    f2r, f2i = np.cos(ang), -np.sin(ang)
    ang = 2.0 * math.pi * (k1[:, None] * k2[None, :]) / N
    twr = np.zeros((K1p, 1, N2))
    twi = np.zeros((K1p, 1, N2))
    twr[:K1n, 0], twi[:K1n, 0] = np.cos(ang), -np.sin(ang)
    c = lambda a, dt: jnp.asarray(a.astype(np.float32)).astype(dt)
    return dict(N1=N1, H1=H1, K1p=K1p, f1=c(f1, bf16), if1=c(if1, bf16), f2r=c(f2r, f32), f2i=c(f2i, f32),
                twr=c(twr, f32), twi=c(twi, f32))


def _build_stage2_matrix(f2r_ref, f2i_ref, twr_ref, twi_ref, m_ref, mt_ref):
    n2 = f2r_ref.shape[0]
    twr, twi = twr_ref[...], twi_ref[...]
    re = f2r_ref[...] * twr - f2i_ref[...] * twi
    im = f2r_ref[...] * twi + f2i_ref[...] * twr
    m_ref[:n2, :n2] = re.astype(bf16)
    m_ref[:n2, n2:] = (-im).astype(bf16)
    m_ref[n2:, :n2] = im.astype(bf16)
    m_ref[n2:, n2:] = re.astype(bf16)
    if mt_ref is not None:
        ret, imt = re.T, im.T
        mt_ref[:n2, :n2] = ret.astype(bf16)
        mt_ref[:n2, n2:] = imt.astype(bf16)
        mt_ref[n2:, :n2] = (-imt).astype(bf16)
        mt_ref[n2:, n2:] = ret.astype(bf16)


def _spectrum_body(f2r_ref, f2i_ref, twr_ref, twi_ref, a_ref, x_ref, m_ref):
    _, nk, n2, c = a_ref.shape
    for kk in range(nk):
        _build_stage2_matrix(f2r_ref, f2i_ref, twr_ref.at[kk], twi_ref.at[kk], m_ref.at[kk], None)
        x = jnp.dot(m_ref[kk], a_ref[:, kk].reshape(2 * n2, c), preferred_element_type=f32)
        x_ref[kk] = x.reshape(2, n2, c)


def _filter_spectrum(tab, a5):
    NO, _, K1p, N2, C = a5.shape
    nk = FREQ_CHAINS
    tw_spec = pl.BlockSpec((nk, 1, N2), lambda k, o: (k, 0, 0))
    return pl.pallas_call(
        _spectrum_body,
        grid=(K1p // nk, NO),
        in_specs=[_const_spec((N2, N2)), _const_spec((N2, N2)), tw_spec, tw_spec,
                  pl.BlockSpec((None, 2, nk, N2, C), lambda k, o: (o, 0, k, 0, 0))],
        out_specs=pl.BlockSpec((nk, 2, N2, C), lambda k, o: (k, 0, 0, o)),
        out_shape=jax.ShapeDtypeStruct((K1p, 2, N2, NO * C), f32),
        scratch_shapes=[pltpu.VMEM((nk, 2 * N2, 2 * N2), bf16)],
        compiler_params=_cparams("parallel", "parallel"),
        name="filter_spectrum",
    )(tab['f2r'], tab['f2i'], tab['twr'], tab['twi'], a5)


def _freq_body(f2r_ref, f2i_ref, twr_ref, twi_ref, a_ref, h_ref, g_ref, m_ref, mt_ref):
    nb, _, nk, n2, c = a_ref.shape
    for kk in range(nk):
        _build_stage2_matrix(f2r_ref, f2i_ref, twr_ref.at[kk], twi_ref.at[kk], m_ref.at[kk], mt_ref.at[kk])
        hr, hi = h_ref[kk, 0], h_ref[kk, 1]
        for b in range(nb):
            x = jnp.dot(m_ref[kk], a_ref[b, :, kk].reshape(2 * n2, c), preferred_element_type=f32)
            xr, xi = x[:n2], x[n2:]
            y = jnp.concatenate([xr * hr - xi * hi, xr * hi + xi * hr], axis=0).astype(bf16)
            g = jnp.dot(mt_ref[kk], y, preferred_element_type=f32)
            g_ref[b, :, kk] = g.reshape(2, n2, c).astype(bf16)


def _freq_stage(tab, a5, hspec, order):
    B, _, K1p, N2, C = a5.shape
    nb = min(B, FREQ_CHAINS)
    nk = FREQ_CHAINS // nb
    tw_spec = pl.BlockSpec((nk, 1, N2), lambda k, b: (k, 0, 0))
    slab = pl.BlockSpec((nb, 2, nk, N2, C), lambda k, b: (b, 0, k, 0, 0))
    return pl.pallas_call(
        _freq_body,
        grid=(K1p // nk, B // nb),
        in_specs=[_const_spec((N2, N2)), _const_spec((N2, N2)), tw_spec, tw_spec, slab,
                  pl.BlockSpec((nk, 2, N2, C), lambda k, b: (k, 0, 0, order))],
        out_specs=slab,
        out_shape=jax.ShapeDtypeStruct(a5.shape, bf16),
        scratch_shapes=[pltpu.VMEM((nk, 2 * N2, 2 * N2), bf16), pltpu.VMEM((nk, 2 * N2, 2 * N2), bf16)],
        compiler_params=_cparams("parallel", "parallel"),
        name="freq_stage",
    )(tab['f2r'], tab['f2i'], tab['twr'], tab['twi'], a5, hspec)


DFT_SLABS = 16


def _slabs_to_matrix(load_tile, rows, S, C, stage_ref):
    for lt in range(C // LANE):
        stage_ref[lt, :rows * S] = load_tile(lt)
    cols = [stage_ref[lt, pl.ds(sl, rows, stride=S), :] for sl in range(S) for lt in range(C // LANE)]
    return jnp.concatenate(cols, axis=1)


def _matrix_to_slabs(mat, rows, S, C, stage_ref):
    for sl in range(S):
        for lt in range(C // LANE):
            lo = sl * C + lt * LANE
            stage_ref[lt, pl.ds(sl, rows, stride=S), :] = mat[:, lo:lo + LANE]
    return [stage_ref[lt, :rows * S].reshape(rows, S, LANE) for lt in range(C // LANE)]


def _stage1_body(w_ref, z_ref, o_ref, stage_ref):
    H1, S, C = z_ref.shape
    K1p = o_ref.shape[1]
    zmat = _slabs_to_matrix(lambda lt: z_ref[:, :, lt * LANE:(lt + 1) * LANE].reshape(H1 * S, LANE), H1, S, C, stage_ref)
    a = jnp.dot(w_ref[...], zmat.astype(bf16), preferred_element_type=f32)
    for lt, tile in enumerate(_matrix_to_slabs(a, 2 * K1p, S, C, stage_ref)):
        o_ref[0, :, :, lt * LANE:(lt + 1) * LANE] = tile[:K1p].astype(bf16)
        o_ref[1, :, :, lt * LANE:(lt + 1) * LANE] = tile[K1p:].astype(bf16)


def _data_stage1(w, z4, K1p, C):
    B, R, N2, Cw = z4.shape
    nlb = Cw // C
    S = DFT_SLABS
    return pl.pallas_call(
        _stage1_body,
        grid=(B, nlb, N2 // S),
        in_specs=[_const_spec(w.shape), pl.BlockSpec((None, R, S, C), lambda b, lb, j: (b, 0, j, lb))],
        out_specs=pl.BlockSpec((None, 2, K1p, S, C), lambda b, lb, j: (b * nlb + lb, 0, 0, j, 0)),
        out_shape=jax.ShapeDtypeStruct((B * nlb, 2, K1p, N2, C), bf16),
        scratch_shapes=[pltpu.VMEM((C // LANE, max(R, 2 * K1p) * S, LANE), f32)],
        compiler_params=_cparams("parallel", "parallel", "parallel"),
        name="dft_data_stage1",
    )(w, z4)


def _conv_out_body(w_ref, g_ref, z_ref, gate_ref, bias_ref, o_ref, stage_ref):
    _, K1p, S, C = g_ref.shape
    H1 = z_ref.shape[0]

    def load_tile(lt):
        parts = [g_ref[ri, :, :, lt * LANE:(lt + 1) * LANE].astype(f32).reshape(K1p * S, LANE) for ri in range(2)]
        return jnp.concatenate(parts, axis=0)

    gmat = _slabs_to_matrix(load_tile, 2 * K1p, S, C, stage_ref)
    y = jnp.dot(w_ref[...], gmat.astype(bf16), preferred_element_type=f32)
    for lt, tile in enumerate(_matrix_to_slabs(y, H1, S, C, stage_ref)):
        sl = slice(lt * LANE, (lt + 1) * LANE)
        o_ref[:, :, sl] = gate_ref[:, :, sl] * (tile + z_ref[:, :, sl] * bias_ref[:, sl].reshape(1, 1, LANE))


def _conv_out(w, g5, z4, gate4, bias):
    B, H1, N2, C = z4.shape
    K1p = g5.shape[2]
    S = DFT_SLABS
    blk = pl.BlockSpec((None, H1, S, C), lambda b, j: (b, 0, j, 0))
    return pl.pallas_call(
        _conv_out_body,
        grid=(B, N2 // S),
        in_specs=[_const_spec(w.shape), pl.BlockSpec((None, 2, K1p, S, C), lambda b, j: (b, 0, 0, j, 0)), blk, blk,
                  _const_spec((1, C))],
        out_specs=blk,
        out_shape=jax.ShapeDtypeStruct(z4.shape, f32),
        scratch_shapes=[pltpu.VMEM((C // LANE, 2 * K1p * S, LANE), f32)],
        compiler_params=_cparams("parallel", "parallel"),
        name="conv_out",
    )(w, g5, z4, gate4, bias)


def _long_conv(v, g1, g2, circ, hy_bias, B, L, C):
    tab = _dft_tables(L)
    N2, N1, H1, K1p = DFT_N2, tab['N1'], tab['H1'], tab['K1p']
    hspec = _filter_spectrum(tab, _data_stage1(tab['f1'], circ.reshape(1, N1, N2, circ.shape[1]), K1p, C))
    z4 = v.reshape(B, H1, N2, C)
    gates = (g1.reshape(B, H1, N2, C), g2.reshape(B, H1, N2, C))
    f1h = tab['f1'][:, :H1]
    for o in range(HY_ORDER):
        g5 = _freq_stage(tab, _data_stage1(f1h, z4, K1p, C), hspec, o)
        z4 = _conv_out(tab['if1'], g5, z4, gates[o], hy_bias[o].reshape(1, C))
    return z4.reshape(B * L, C)


def _s5_rows(B, L):
    return (True, S5_ROWS, L // S5_ROWS) if B == 1 else (False, B, L)


def _encoder_layer(x, p, s5_mats, alpha):
    B, L, D = x.shape
    n = B * L
    x2 = x.reshape(n, D)
    G, P, H = p['s5_b_re'].shape[1:]
    s5w = G * H
    hyw = p['w_hy_proj'].shape[0]

    u_g, v, g1, g2, g_s, g_h = _inproj(x2, p['w_in'].astype(bf16), p['hy_short_w'], p['hy_short_b'], B, L, G, H, hyw)

    chained, Bs, _ = _s5_rows(B, L)
    y_g = _s5(u_g, s5_mats, Bs, chained)

    circ = _circ_filters(L, p['filt_w1'], p['filt_b1'], p['filt_freq1'], p['filt_w2'], p['filt_b2'],
                         p['filt_freq2'], p['filt_w3'], p['filt_decay'])
    z = _long_conv(v, g1, g2, circ, p['hy_bias'], B, L, hyw)

    x1, aff = _merge(x2, y_g, z, g_s, g_h, p['s5_w_glu'].astype(bf16), p['w_s5_proj'].astype(bf16),
                     p['w_hy_proj'].astype(bf16), p['w_out'].astype(bf16),
                     p['ln1_g'].reshape(1, D), p['ln1_b'].reshape(1, D), p['w_router'], B, L, alpha)

    out = _expert_choice(x1, aff, p['ex_w1_bf16'], p['ex_w3_bf16'], p['ex_w2_bf16'],
                         p['ln2_g'].reshape(1, D), p['ln2_b'].reshape(1, D), alpha)
    return out.reshape(B, L, D)


_PARAM_NAMES = ('w_in', 's5_lambda_re', 's5_lambda_im', 's5_log_dt', 's5_b_re', 's5_b_im',
                's5_c_re', 's5_c_im', 's5_d', 's5_w_glu', 'w_s5_proj',
                'hy_short_w', 'hy_short_b', 'filt_w1', 'filt_b1', 'filt_freq1', 'filt_w2', 'filt_b2',
                'filt_freq2', 'filt_w3', 'filt_decay', 'hy_bias', 'w_hy_proj', 'w_out',
                'ln1_g', 'ln1_b', 'w_router', 'ex_w1', 'ex_w3', 'ex_w2', 'ln2_g', 'ln2_b')


def kernel(x_prompt, x_sample, w_in, s5_lambda_re, s5_lambda_im, s5_log_dt, s5_b_re, s5_b_im, s5_c_re, s5_c_im, s5_d, s5_w_glu, w_s5_proj, hy_short_w, hy_short_b, filt_w1, filt_b1, filt_freq1, filt_w2, filt_b2, filt_freq2, filt_w3, filt_decay, hy_bias, w_hy_proj, w_out, ln1_g, ln1_b, w_router, ex_w1, ex_w3, ex_w2, ln2_g, ln2_b):
    stacked = (w_in, s5_lambda_re, s5_lambda_im, s5_log_dt, s5_b_re, s5_b_im, s5_c_re, s5_c_im, s5_d,
               s5_w_glu, w_s5_proj, hy_short_w, hy_short_b, filt_w1, filt_b1, filt_freq1, filt_w2,
               filt_b2, filt_freq2, filt_w3, filt_decay, hy_bias, w_hy_proj, w_out, ln1_g, ln1_b,
               w_router, ex_w1, ex_w3, ex_w2, ln2_g, ln2_b)
    depth = w_in.shape[0]
    alpha = (2.0 * depth) ** 0.25
    xs = [x_prompt, x_sample]
    for l in range(depth):
        p = {k: v[l] for k, v in zip(_PARAM_NAMES, stacked)}
        for k in ('ex_w1', 'ex_w3', 'ex_w2'):
            p[k + '_bf16'] = p[k].astype(bf16)
        s5_mats = _s5_matrices(p, [_s5_rows(x.shape[0], x.shape[1])[2] for x in xs])
        xs = [_encoder_layer(x, p, mats, alpha) for x, mats in zip(xs, s5_mats)]
    return tuple(xs)
```

```python
import functools
import math

import jax
import jax.numpy as jnp
import numpy as np
from jax import lax
from jax.experimental import pallas as pl
from jax.experimental.pallas import tpu as pltpu

FILT_BANDS = 8
FILT_SHIFT = 0.05
EC_CAPACITY = 2
LN_EPS = 1e-5
HY_ORDER = 2

VMEM_LIMIT_BYTES = 56 * 1024 * 1024
ROW_TILE = 512

bf16 = jnp.bfloat16
f32 = jnp.float32


def _cparams(*sem):
    return pltpu.CompilerParams(dimension_semantics=sem, vmem_limit_bytes=VMEM_LIMIT_BYTES)


def _const_spec(shape):
    return pl.BlockSpec(shape, lambda *_: (0,) * len(shape))


def _inproj_body(xp_ref, x_ref, xn_ref, w_ref, cw_ref, cb_ref, ug_ref, v_ref, g1_ref, g2_ref, gs_ref, gh_ref,
                 stage_ref, *, o1, o2, o3, nt, H):
    i = pl.program_id(1)
    proj = jnp.dot(x_ref[...].astype(bf16), w_ref[...], preferred_element_type=f32)
    _rows_to_chunks(proj[:, :o1], stage_ref, ug_ref, H)
    gs_ref[...] = jax.nn.sigmoid(proj[:, o2:o3])
    gh_ref[...] = jax.nn.sigmoid(proj[:, o3:])
    u = proj[:, o1:o2]
    w_hy = w_ref[:, o1:o2]
    prev_row = jnp.dot(xp_ref[...].astype(bf16), w_hy, preferred_element_type=f32)[7:8]
    next_row = jnp.dot(xn_ref[...].astype(bf16), w_hy, preferred_element_type=f32)[0:1]
    prev_row = jnp.where(i == 0, 0.0, prev_row)
    next_row = jnp.where(i == nt - 1, 0.0, next_row)
    tl = u.shape[0]
    rows = lax.broadcasted_iota(jnp.int32, u.shape, 0)
    um1 = jnp.where(rows == 0, prev_row, pltpu.roll(u, 1, axis=0))
    up1 = jnp.where(rows == tl - 1, next_row, pltpu.roll(u, tl - 1, axis=0))
    cw = cw_ref[...]
    hy = um1 * cw[0:1, :] + u * cw[1:2, :] + up1 * cw[2:3, :] + cb_ref[...]
    hyw = v_ref.shape[1]
    v_ref[...] = hy[:, :hyw]
    g1_ref[...] = hy[:, hyw:2 * hyw]
    g2_ref[...] = hy[:, 2 * hyw:]


def _inproj(x2, w_in, conv_w, conv_b, B, L, G, H, hyw):
    n, D = x2.shape
    cols = w_in.shape[1]
    s5w = G * H
    o1, o2 = s5w, s5w + 3 * hyw
    o3 = o2 + D
    tm = min(ROW_TILE, L)
    nt = L // tm
    r8 = tm // 8
    row = lambda b, i: (b * nt + i, 0)
    prev = lambda b, i: (jnp.maximum((b * nt + i) * r8 - 1, 0), 0)
    nxt = lambda b, i: (jnp.minimum((b * nt + i + 1) * r8, n // 8 - 1), 0)
    hy_out = jax.ShapeDtypeStruct((n, hyw), f32)
    return pl.pallas_call(
        functools.partial(_inproj_body, o1=o1, o2=o2, o3=o3, nt=nt, H=H),
        grid=(B, nt),
        in_specs=[pl.BlockSpec((8, D), prev), pl.BlockSpec((tm, D), row), pl.BlockSpec((8, D), nxt),
                  _const_spec((D, cols)), _const_spec((3, 3 * hyw)), _const_spec((1, 3 * hyw))],
        out_specs=[pl.BlockSpec((G, tm // S5_CHUNK, S5_CHUNK * H), lambda b, i: (0, b * nt + i, 0)),
                   pl.BlockSpec((tm, hyw), row), pl.BlockSpec((tm, hyw), row), pl.BlockSpec((tm, hyw), row),
                   pl.BlockSpec((tm, D), row), pl.BlockSpec((tm, D), row)],
        out_shape=[jax.ShapeDtypeStruct((G, n // S5_CHUNK, S5_CHUNK * H), bf16), hy_out, hy_out, hy_out,
                   jax.ShapeDtypeStruct((n, D), f32), jax.ShapeDtypeStruct((n, D), f32)],
        scratch_shapes=[pltpu.VMEM((s5w // LANE, tm, LANE), f32)],
        compiler_params=_cparams("parallel", "parallel"),
        name="inproj",
    )(x2, x2, x2, w_in, conv_w, conv_b.reshape(1, 3 * hyw))


S5_CHUNK = 16
S5_ROWS = 8
LANE = 128


def _s5_body(u_ref, kmat_ref, winf_ref, winb_ref, woutf_ref, woutb_ref, af_ref, ab_ref, y_ref,
             vf_ref, vb_ref, sf_ref, sb_ref, *, R, nc, chained):
    u = u_ref[0]
    hw = sf_ref.shape[1]
    seq_rows = lambda b: pl.ds(b, nc, stride=R)
    for v_ref, w_ref in ((vf_ref, winf_ref), (vb_ref, winb_ref)):
        v = jnp.dot(u, w_ref[0], preferred_element_type=f32)
        for b in range(R):
            v_ref[0, seq_rows(b), :] = v[b * nc:(b + 1) * nc, :hw]
            v_ref[1, seq_rows(b), :] = v[b * nc:(b + 1) * nc, hw:]
    bc = lambda ref, row, lo: jnp.broadcast_to(ref[0, row:row + 1, lo:lo + hw], (R, hw))
    decay = lambda ref, row: ((bc(ref, row, 0), bc(ref, row, hw)), (bc(ref, row + 1, 0), bc(ref, row + 1, hw)))
    f_a1, f_a2 = decay(af_ref, 0)
    b_a1, b_a2 = decay(ab_ref, 0)

    def mul_add(s0, s1, a1, a2, v0, v1):
        return a1[0] * s0 + a2[0] * s1 + v0, a1[1] * s1 + a2[1] * s0 + v1

    def scan(init, store):
        def step(i, carry):
            f0, f1, b0, b1 = carry
            rf = pl.ds(pl.multiple_of(i * R, R), R)
            rb = pl.ds(pl.multiple_of((nc - 1 - i) * R, R), R)
            if store:
                sf_ref[rf, :] = f0
                sb_ref[rb, :] = b0
            f0, f1 = mul_add(f0, f1, f_a1, f_a2, vf_ref[0, rf, :], vf_ref[1, rf, :])
            b0, b1 = mul_add(b0, b1, b_a1, b_a2, vb_ref[0, rb, :], vb_ref[1, rb, :])
            return f0, f1, b0, b1
        return lax.fori_loop(0, nc, step, init, unroll=4)

    zero = jnp.zeros((R, hw), f32)
    init = (zero, zero, zero, zero)
    if chained:
        ef0, ef1, eb0, eb1 = scan(init, store=False)
        fn_a1, fn_a2 = decay(af_ref, 2)
        bn_a1, bn_a2 = decay(ab_ref, 2)
        row = lax.broadcasted_iota(jnp.int32, (R, hw), 0)
        down = lambda v: jnp.where(row == 0, 0.0, pltpu.roll(v, 1, axis=0))
        up = lambda v: jnp.where(row == R - 1, 0.0, pltpu.roll(v, R - 1, axis=0))
        f0, f1, b0, b1 = init
        for _ in range(R - 1):
            t0, t1 = mul_add(f0, f1, fn_a1, fn_a2, ef0, ef1)
            f0, f1 = down(t0), down(t1)
            t0, t1 = mul_add(b0, b1, bn_a1, bn_a2, eb0, eb1)
            b0, b1 = up(t0), up(t1)
        init = (f0, f1, b0, b1)
    scan(init, store=True)
    y = jnp.dot(u, kmat_ref[0], preferred_element_type=f32)
    for s_ref, w_ref in ((sf_ref, woutf_ref), (sb_ref, woutb_ref)):
        states = jnp.concatenate([s_ref[seq_rows(b), :] for b in range(R)], axis=0)
        y = y + jnp.dot(states.astype(bf16), w_ref[0], preferred_element_type=f32)
    y_ref[0] = y


def _s5(u_g, mats, R, chained):
    G, M, W = u_g.shape
    kmat, winf, winb, woutf, woutb, af, ab = mats
    hw = woutf.shape[1]
    grp = lambda shape: pl.BlockSpec((1,) + shape, lambda g: (g, 0, 0))
    return pl.pallas_call(
        functools.partial(_s5_body, R=R, nc=M // R, chained=chained),
        grid=(G,),
        in_specs=[grp((M, W)), grp((W, W)), grp((W, 2 * hw)), grp((W, 2 * hw)), grp((hw, W)), grp((hw, W)),
                  grp((4, 2 * hw)), grp((4, 2 * hw))],
        out_specs=grp((M, W)),
        out_shape=jax.ShapeDtypeStruct((G, M, W), f32),
        scratch_shapes=[pltpu.VMEM((2, M, hw), f32), pltpu.VMEM((2, M, hw), f32),
                        pltpu.VMEM((M, hw), f32), pltpu.VMEM((M, hw), f32)],
        compiler_params=_cparams("parallel"),
        name="s5_chunked",
    )(u_g, kmat, winf, winb, woutf, woutb, af, ab)


def _rows_to_chunks(x, stage_ref, o_ref, H):
    Tc = S5_CHUNK
    nr = x.shape[0] // Tc
    gpt, per_tile = LANE // H, LANE // H
    for lt in range(x.shape[1] // LANE):
        stage_ref[lt] = x[:, lt * LANE:(lt + 1) * LANE]
        rows_t = [stage_ref[lt, pl.ds(t, nr, stride=Tc), :] for t in range(Tc)]
        for g in range(gpt):
            for j in range(Tc // per_tile):
                tile = jnp.concatenate([rows_t[per_tile * j + k][:, H * g:H * (g + 1)] for k in range(per_tile)], axis=1)
                o_ref[lt * gpt + g, :, j * LANE:(j + 1) * LANE] = tile.astype(o_ref.dtype)


def _chunks_to_rows(y_ref, stage_ref, H):
    Tc = S5_CHUNK
    nr = y_ref.shape[1]
    gpt = LANE // H
    for lt in range(stage_ref.shape[0]):
        for t in range(Tc):
            tile = jnp.concatenate([y_ref[lt * gpt + g, :, H * t:H * (t + 1)] for g in range(gpt)], axis=1)
            stage_ref[lt, pl.ds(t, nr, stride=Tc), :] = tile
    return jnp.concatenate([stage_ref[lt] for lt in range(stage_ref.shape[0])], axis=1)


def _s5_discretise(lam_re, lam_im, log_dt, b_re, b_im, powers):
    lam = lax.complex(-jnp.abs(lam_re.astype(f32)), lam_im.astype(f32))
    dt = jnp.exp(log_dt.astype(f32))[:, None]
    a_bar = jnp.exp(lam * dt)
    k = jnp.asarray(powers, f32)[None, :, None]
    apow = jnp.exp((lam * dt)[:, None, :] * k)
    b_bar = ((a_bar - 1.0) / lam)[..., None] * lax.complex(b_re.astype(f32), b_im.astype(f32))
    return apow, b_bar


def _s5_matrices(p, seg_steps):
    Tc = S5_CHUNK
    powers = list(range(Tc + 1)) + list(seg_steps)
    apf, bbf = _s5_discretise(p['s5_lambda_re'][0], p['s5_lambda_im'][0], p['s5_log_dt'][0],
                              p['s5_b_re'][0], p['s5_b_im'][0], powers)
    apb, bbb = _s5_discretise(p['s5_lambda_re'][1], p['s5_lambda_im'][1], p['s5_log_dt'][1],
                              p['s5_b_re'][1], p['s5_b_im'][1], powers)
    c = lax.complex(p['s5_c_re'].astype(f32), p['s5_c_im'].astype(f32))
    G, H, P = c.shape
    W = Tc * H
    kf = jnp.real(jnp.einsum('ghp,gtp,gpk->gthk', c, apf[:, :Tc], bbf))
    kb = jnp.real(jnp.einsum('ghp,gtp,gpk->gthk', c, apb[:, :Tc], bbb))
    tau = jnp.arange(Tc)[None, :] - jnp.arange(Tc)[:, None]
    blk = (jnp.where((tau >= 0)[None, :, :, None, None], kf[:, jnp.abs(tau)], 0.0)
           + jnp.where((tau <= 0)[None, :, :, None, None], kb[:, jnp.abs(tau)], 0.0))
    kmat = jnp.transpose(blk, (0, 1, 4, 2, 3)).reshape(G, W, W)
    kmat = kmat + jnp.eye(W, dtype=f32)[None] * jnp.tile(p['s5_d'].astype(f32), (1, Tc))[:, None, :]
    lanes = lambda z: jnp.concatenate([jnp.real(z), jnp.imag(z), jnp.imag(z), jnp.real(z)], axis=-1)
    win = lambda ap, bb: lanes(jnp.einsum('gsp,gpk->gskp', ap, bb)).reshape(G, W, 4 * P)
    winf = win(apf[:, Tc - 1::-1][:, :Tc], bbf)
    winb = win(apb[:, :Tc], bbb)
    def wout(ap):
        z = jnp.einsum('ghp,gtp->gpth', c, ap).reshape(G, P, W)
        return jnp.concatenate([jnp.real(z), -jnp.imag(z)], axis=1)
    woutf = wout(apf[:, 1:Tc + 1])
    woutb = wout(apb[:, Tc:0:-1])
    def chunk_decay(ap, seg):
        rows = []
        for k in (Tc, Tc + 1 + seg):
            ar, ai = jnp.real(ap[:, k]), jnp.imag(ap[:, k])
            rows += [jnp.concatenate([ar, ar, ar, ar], -1), jnp.concatenate([-ai, ai, ai, -ai], -1)]
        return jnp.stack(rows, axis=1)
    cast = lambda m: m.astype(bf16)
    shared = (cast(kmat), cast(winf), cast(winb), cast(woutf), cast(woutb))
    return [shared + (chunk_decay(apf, i), chunk_decay(apb, i)) for i in range(len(seg_steps))]


MERGE_SPLIT = 2


def _layer_norm(v, g, b):
    mu = jnp.mean(v, axis=-1, keepdims=True)
    c = v - mu
    var = jnp.mean(c * c, axis=-1, keepdims=True)
    return c * lax.rsqrt(var + LN_EPS) * g + b


def _merge_body(x_ref, yg_ref, z_ref, gs_ref, gh_ref, wglu_ref, wsp_ref, whp_ref, wout_ref,
                g_ref, b_ref, wrh_ref, wrl_ref, x1_ref, aff_ref, stage_ref, *, alpha, H):
    ys_all = jax.nn.gelu(_chunks_to_rows(yg_ref, stage_ref, H))
    tm = x_ref.shape[0]
    for r0 in range(0, tm, tm // MERGE_SPLIT):
        rs = slice(r0, r0 + tm // MERGE_SPLIT)
        ys = ys_all[rs]
        gate = jax.nn.sigmoid(jnp.dot(ys.astype(bf16), wglu_ref[...], preferred_element_type=f32))
        branch_s = jnp.dot((ys * gate).astype(bf16), wsp_ref[...], preferred_element_type=f32)
        branch_h = jnp.dot(z_ref[rs, :].astype(bf16), whp_ref[...], preferred_element_type=f32)
        mix = gs_ref[rs, :] * branch_s + gh_ref[rs, :] * branch_h
        mix = jnp.dot(mix.astype(bf16), wout_ref[...], preferred_element_type=f32)
        x1 = _layer_norm(alpha * x_ref[rs, :] + mix, g_ref[...], b_ref[...])
        x1_ref[rs, :] = x1
        x1_hi = x1.astype(bf16)
        x1_lo = (x1 - x1_hi.astype(f32)).astype(bf16)
        logits = (jnp.dot(x1_hi, wrh_ref[...], preferred_element_type=f32)
                  + jnp.dot(x1_lo, wrh_ref[...], preferred_element_type=f32)
                  + jnp.dot(x1_hi, wrl_ref[...], preferred_element_type=f32))
        m = jnp.max(logits, axis=-1, keepdims=True)
        e = jnp.exp(logits - m)
        aff_ref[rs, :] = e / jnp.sum(e, axis=-1, keepdims=True)


def _merge(x2, y_g, z, g_s, g_h, w_glu, w_sp, w_hp, w_out, ln_g, ln_b, w_router, B, L, alpha):
    n, D = x2.shape
    G, _, W = y_g.shape
    H = W // S5_CHUNK
    s5w, hyw = G * H, z.shape[1]
    E = w_router.shape[1]
    tm = min(ROW_TILE, L)
    nt = L // tm
    row = lambda b, i: (b * nt + i, 0)
    wr_hi = w_router.astype(bf16)
    wr_lo = (w_router - wr_hi.astype(f32)).astype(bf16)
    return pl.pallas_call(
        functools.partial(_merge_body, alpha=alpha, H=H),
        grid=(B, nt),
        in_specs=[pl.BlockSpec((tm, D), row),
                  pl.BlockSpec((G, tm // S5_CHUNK, W), lambda b, i: (0, b * nt + i, 0)),
                  pl.BlockSpec((tm, hyw), row),
                  pl.BlockSpec((tm, D), row),
                  pl.BlockSpec((tm, D), row),
                  _const_spec(w_glu.shape), _const_spec(w_sp.shape), _const_spec(w_hp.shape),
                  _const_spec(w_out.shape), _const_spec((1, D)), _const_spec((1, D)),
                  _const_spec(w_router.shape), _const_spec(w_router.shape)],
        out_specs=[pl.BlockSpec((tm, D), row), pl.BlockSpec((tm, E), row)],
        out_shape=[jax.ShapeDtypeStruct((n, D), f32), jax.ShapeDtypeStruct((n, E), f32)],
        scratch_shapes=[pltpu.VMEM((s5w // LANE, tm, LANE), f32)],
        compiler_params=_cparams("parallel", "parallel"),
        name="merge",
    )(x2, y_g, z, g_s, g_h, w_glu, w_sp, w_hp, w_out, ln_g, ln_b, wr_hi, wr_lo)


ROUTE_BLOCK = 256
ROUTE_SLOTS = 64
ROW_ALIGN = 8


def _select_body(aff_ref, sel_ref, *, cap, idx_bits):
    bits = pltpu.bitcast(aff_ref[...], jnp.int32)
    E = bits.shape[0]
    count = lambda m: jnp.sum(jnp.where(m, 1.0, 0.0), axis=1, keepdims=True)

    def value_bit(i, prefix):
        cand = prefix | jnp.left_shift(jnp.int32(1), 30 - i)
        return jnp.where(count(bits >= cand) >= cap, cand, prefix)

    thr = lax.fori_loop(0, 31, value_bit, jnp.zeros((E, 1), jnp.int32))
    need = cap - count(bits > thr)
    idx = lax.broadcasted_iota(jnp.int32, bits.shape, 1)
    tie_idx = jnp.where(bits == thr, idx, jnp.int32(2 ** 30))

    def index_bit(i, bound):
        cand = bound | jnp.left_shift(jnp.int32(1), idx_bits - 1 - i)
        return jnp.where(count(tie_idx < cand) <= need, cand, bound)

    bound = lax.fori_loop(0, idx_bits, index_bit, jnp.zeros((E, 1), jnp.int32))
    sel_ref[...] = jnp.where(bits > thr, 1.0, jnp.where(tie_idx < bound, 1.0, 0.0))


def _select(aff_t, cap):
    E, n = aff_t.shape
    return pl.pallas_call(
        functools.partial(_select_body, cap=float(cap), idx_bits=int(n).bit_length()),
        out_shape=jax.ShapeDtypeStruct((E, n), f32),
        compiler_params=pltpu.CompilerParams(vmem_limit_bytes=VMEM_LIMIT_BYTES),
        name="expert_select",
    )(aff_t)


def _slot_onehot(sel, p_ref, first_slot, E, weight=None):
    Tb = sel.shape[1]
    S = p_ref.shape[0] // E
    r = lax.broadcasted_iota(jnp.int32, (Tb, Tb), 0)
    c = lax.broadcasted_iota(jnp.int32, (Tb, Tb), 1)
    tri = jnp.where(r <= c, 1.0, 0.0).astype(bf16)
    incl = jnp.dot(sel.astype(bf16), tri, preferred_element_type=f32)
    slot = jnp.where(sel > 0.0, incl - 1.0, -1.0)
    want = (lax.broadcasted_iota(jnp.int32, (S, Tb), 0) + first_slot).astype(f32)
    for e in range(E):
        hit = slot[e:e + 1] == want
        entry = 1.0 if weight is None else weight[e:e + 1]
        p_ref[e * S:(e + 1) * S, :] = jnp.where(hit, entry, 0.0).astype(bf16)


def _gather_copies(stage_ref, xe_hbm, sem, off_ref, j, first_slot, E, S, nblk):
    copies = []
    for e in range(E):
        row = pl.multiple_of(off_ref[e * nblk + j] + first_slot, ROW_ALIGN)
        copies.append(pltpu.make_async_copy(stage_ref.at[e * S:(e + 1) * S], xe_hbm.at[e, pl.ds(row, S)], sem))
    return copies


def _gather_body(off_ref, rounds_ref, total_ref, sel_ref, x_ref, xe_hbm, p_ref, stage_ref, sem, nwrites_ref,
                 *, E, nblk, cap):
    j = pl.program_id(0)
    S = p_ref.shape[0] // E
    D = x_ref.shape[1]

    @pl.when(j == 0)
    def _():
        nwrites_ref[0] = 0

    x = x_ref[...].astype(bf16)

    def one_round(r):
        w = nwrites_ref[0]
        slot = lax.rem(w, 2)
        first_slot = r * S
        _slot_onehot(sel_ref[...], p_ref, first_slot, E)
        stage_ref[slot] = jnp.dot(p_ref[...], x, preferred_element_type=f32)

        @pl.when(w > 0)
        def _():
            for cp in _gather_copies(stage_ref.at[1 - slot], xe_hbm, sem.at[1 - slot], off_ref, j, 0, E, S, nblk):
                cp.wait()

        for cp in _gather_copies(stage_ref.at[slot], xe_hbm, sem.at[slot], off_ref, j, first_slot, E, S, nblk):
            cp.start()
        nwrites_ref[0] = w + 1

    one_round(0)

    def extra(r, carry):
        one_round(r)
        return carry

    lax.fori_loop(1, rounds_ref[j], extra, 0)

    @pl.when(j == nblk - 1)
    def _():
        last = lax.rem(nwrites_ref[0] - 1, 2)
        for cp in _gather_copies(stage_ref.at[last], xe_hbm, sem.at[last], off_ref, j, 0, E, S, nblk):
            cp.wait()
        rows_pad = xe_hbm.shape[1]
        zeros_ref = stage_ref.at[0, :S]
        zeros_ref[...] = jnp.zeros((S, D), f32)
        nfill = -(-(rows_pad - cap) // S)

        def fill(e, row):
            return pltpu.make_async_copy(zeros_ref, xe_hbm.at[e, pl.ds(pl.multiple_of(row, ROW_ALIGN), S)], sem.at[0])

        def whole_chunks(e, action):
            def body(k, c):
                row = total_ref[e] + k * S

                @pl.when(row + S <= rows_pad)
                def _():
                    action(fill(e, row))
                return c
            lax.fori_loop(0, nfill, body, 0)

        for e in range(E):
            whole_chunks(e, lambda cp: cp.start())
        for e in range(E):
            whole_chunks(e, lambda cp: cp.wait())
        for e in range(E):
            fill(e, rows_pad - S).start()
        for e in range(E):
            fill(e, rows_pad - S).wait()


def _route_gather(sel, x1, off, rounds, total, rows_pad):
    E, n = sel.shape
    D = x1.shape[1]
    Tb, S = ROUTE_BLOCK, ROUTE_SLOTS
    nblk = n // Tb
    return pl.pallas_call(
        functools.partial(_gather_body, E=E, nblk=nblk, cap=EC_CAPACITY * n // E),
        grid_spec=pltpu.PrefetchScalarGridSpec(
            num_scalar_prefetch=3,
            grid=(nblk,),
            in_specs=[pl.BlockSpec((E, Tb), lambda j, *_: (0, j)),
                      pl.BlockSpec((Tb, D), lambda j, *_: (j, 0))],
            out_specs=pl.BlockSpec(memory_space=pl.ANY),
            scratch_shapes=[pltpu.VMEM((E * S, Tb), bf16),
                            pltpu.VMEM((2, E * S, D), f32),
                            pltpu.SemaphoreType.DMA((2,)),
                            pltpu.SMEM((1,), jnp.int32)]),
        out_shape=jax.ShapeDtypeStruct((E, rows_pad, D), f32),
        compiler_params=_cparams("arbitrary"),
        name="route_gather",
    )(off, rounds, total, sel, x1)


def _ffn_body(total_ref, xe_ref, w1_ref, w3_ref, w2_ref, ye_ref):
    e, r = pl.program_id(0), pl.program_id(1)
    tr = xe_ref.shape[1]
    D = ye_ref.shape[2]

    @pl.when(r * tr < total_ref[e])
    def _():
        xe = xe_ref[0].astype(bf16)
        h1 = jnp.dot(xe, w1_ref[0], preferred_element_type=f32)
        h3 = jnp.dot(xe, w3_ref[0], preferred_element_type=f32)
        h = (jax.nn.silu(h1) * h3).astype(bf16)
        ye = jnp.dot(h, w2_ref[0], preferred_element_type=f32)
        row = r * tr + lax.broadcasted_iota(jnp.int32, ye.shape, 0)
        ye_ref[0] = jnp.where(row < total_ref[e], ye, 0.0)

    @pl.when(r * tr >= total_ref[e])
    def _():
        ye_ref[...] = jnp.zeros_like(ye_ref)


def _ffn(total, xe, w1, w3, w2):
    E, rows_pad, Dx = xe.shape
    D, F = w1.shape[1:]
    tr = ROW_TILE
    last = lambda e, tot: (tot[e] - 1) // tr
    rowmap = lambda e, r, tot: (e, jnp.minimum(r, last(e, tot)), 0)
    wmap = lambda e, r, tot: (e, 0, 0)
    return pl.pallas_call(
        _ffn_body,
        grid_spec=pltpu.PrefetchScalarGridSpec(
            num_scalar_prefetch=1,
            grid=(E, rows_pad // tr),
            in_specs=[pl.BlockSpec((1, tr, Dx), rowmap),
                      pl.BlockSpec((1, D, F), wmap), pl.BlockSpec((1, D, F), wmap),
                      pl.BlockSpec((1, F, D), wmap)],
            out_specs=pl.BlockSpec((1, tr, D), lambda e, r, tot: (e, r, 0))),
        out_shape=jax.ShapeDtypeStruct((E, rows_pad, D), f32),
        compiler_params=_cparams("arbitrary", "arbitrary"),
        name="expert_ffn",
    )(total, xe, w1, w3, w2)


def _scatter_copies(ye_hbm, buf_ref, sem, off_ref, j, first_slot, E, S, nblk):
    copies = []
    for e in range(E):
        row = pl.multiple_of(off_ref[e * nblk + j] + first_slot, ROW_ALIGN)
        copies.append(pltpu.make_async_copy(ye_hbm.at[e, pl.ds(row, S)], buf_ref.at[e * S:(e + 1) * S], sem))
    return copies


def _scatter_body(off_ref, rounds_ref, sel_ref, aff_ref, x1_ref, g_ref, b_ref, ye_hbm, o_ref, p_ref, buf_ref, sem,
                  *, E, nblk, alpha):
    j = pl.program_id(0)
    S = p_ref.shape[0] // E
    slot = lax.rem(j, 2)

    def fetch(jj, first_slot, s):
        return _scatter_copies(ye_hbm, buf_ref.at[s], sem.at[s], off_ref, jj, first_slot, E, S, nblk)

    @pl.when(j == 0)
    def _():
        for cp in fetch(j, 0, 0):
            cp.start()

    @pl.when(j + 1 < nblk)
    def _():
        for cp in fetch(j + 1, 0, 1 - slot):
            cp.start()

    def one_round(r, s):
        _slot_onehot(sel_ref[...], p_ref, r * S, E, weight=aff_ref[...])
        ye = buf_ref[s].astype(bf16)
        return lax.dot_general(p_ref[...], ye, (((0,), (0,)), ((), ())), preferred_element_type=f32)

    for cp in fetch(j, 0, slot):
        cp.wait()
    moe = one_round(jnp.int32(0), slot)

    def extra(r, acc):
        for cp in fetch(j, r * S, 2):
            cp.start()
        for cp in fetch(j, r * S, 2):
            cp.wait()
        return acc + one_round(r, 2)

    moe = lax.fori_loop(1, rounds_ref[j], extra, moe)
    o_ref[...] = _layer_norm(alpha * x1_ref[...] + moe, g_ref[...], b_ref[...])


def _route_scatter(sel, aff_t, x1, ye, off, rounds, ln_g, ln_b, alpha):
    E, n = sel.shape
    D = x1.shape[1]
    Tb, S = ROUTE_BLOCK, ROUTE_SLOTS
    nblk = n // Tb
    return pl.pallas_call(
        functools.partial(_scatter_body, E=E, nblk=nblk, alpha=alpha),
        grid_spec=pltpu.PrefetchScalarGridSpec(
            num_scalar_prefetch=2,
            grid=(nblk,),
            in_specs=[pl.BlockSpec((E, Tb), lambda j, *_: (0, j)),
                      pl.BlockSpec((E, Tb), lambda j, *_: (0, j)),
                      pl.BlockSpec((Tb, D), lambda j, *_: (j, 0)),
                      pl.BlockSpec((1, D), lambda j, *_: (0, 0)),
                      pl.BlockSpec((1, D), lambda j, *_: (0, 0)),
                      pl.BlockSpec(memory_space=pl.ANY)],
            out_specs=pl.BlockSpec((Tb, D), lambda j, *_: (j, 0)),
            scratch_shapes=[pltpu.VMEM((E * S, Tb), bf16),
                            pltpu.VMEM((3, E * S, D), f32),
                            pltpu.SemaphoreType.DMA((3,))]),
        out_shape=jax.ShapeDtypeStruct((n, D), f32),
        compiler_params=_cparams("arbitrary"),
        name="route_scatter",
    )(off, rounds, sel, aff_t, x1, ln_g, ln_b, ye)


def _expert_choice(x1, aff, w1, w3, w2, ln_g, ln_b, alpha):
    n, D = x1.shape
    E = aff.shape[1]
    cap = EC_CAPACITY * n // E
    Tb, S = ROUTE_BLOCK, ROUTE_SLOTS
    nblk = n // Tb
    aff_t = aff.T
    sel = _select(aff_t, cap)
    cnt = sel.reshape(E, nblk, Tb).sum(-1).astype(jnp.int32)
    cnt_al = (cnt + ROW_ALIGN - 1) // ROW_ALIGN * ROW_ALIGN
    end = jnp.cumsum(cnt_al, axis=1)
    off = (end - cnt_al).reshape(E * nblk)
    total = end[:, -1]
    rounds = jnp.maximum((jnp.max(cnt, axis=0) + S - 1) // S, 1)
    rows_pad = -(-(cap + ROW_ALIGN * nblk + Tb) // ROW_TILE) * ROW_TILE
    xe = _route_gather(sel, x1, off, rounds, total, rows_pad)
    ye = _ffn(total, xe, w1, w3, w2)
    return _route_scatter(sel, aff_t, x1, ye, off, rounds, ln_g, ln_b, alpha)


FILT_FEAT_PAD = 32


def _lag_features(L):
    n = np.arange(2 * L, dtype=np.float64)
    pos = np.where(n < L, n, 2 * L - n)
    t = pos / (L - 1)
    bands = np.linspace(1e-4, FILT_BANDS - 1, FILT_BANDS)
    ang = 2.0 * math.pi * (pos / L)[None, :] * bands[:, None]
    z = np.concatenate([t[None, :], np.cos(ang), -np.sin(ang)], axis=0)
    out = np.zeros((FILT_FEAT_PAD, 2 * L), np.float32)
    out[:z.shape[0]] = z
    return out, t.astype(np.float32)[:, None]


def _filter_body(z_ref, t_ref, w1_ref, b1_ref, f1_ref, w2_ref, b2_ref, f2_ref, w3_ref, dec_ref, o_ref, *, L):
    hp = lax.Precision.HIGHEST
    h = jnp.sin(f1_ref[...] * (jnp.dot(w1_ref[...], z_ref[...], precision=hp, preferred_element_type=f32) + b1_ref[...]))
    h = jnp.sin(f2_ref[...] * (jnp.dot(w2_ref[...], h, precision=hp, preferred_element_type=f32) + b2_ref[...]))
    h = lax.dot_general(h, w3_ref[...], (((0,), (0,)), ((), ())), precision=hp, preferred_element_type=f32)
    window = jnp.exp(-t_ref[...] * jnp.abs(dec_ref[...])) + FILT_SHIFT
    tl = h.shape[0]
    row = pl.program_id(0) * tl + lax.broadcasted_iota(jnp.int32, h.shape, 0)
    o_ref[...] = jnp.where(row == L, 0.0, h * window)


def _circ_filters(L, w1, b1, f1, w2, b2, f2, w3, decay):
    nh = w1.shape[1]
    oc = w3.shape[1] // 2
    feats, t_col = _lag_features(L)
    w1t = jnp.zeros((nh, FILT_FEAT_PAD), f32).at[:, :w1.shape[0]].set(w1.T)
    tl = min(ROW_TILE, L)
    half = L // tl
    dirsel = lambda i: (i // half, 0, 0)
    col = lambda a: a.reshape(nh, 1)
    return pl.pallas_call(
        functools.partial(_filter_body, L=L),
        grid=(2 * L // tl,),
        in_specs=[pl.BlockSpec((FILT_FEAT_PAD, tl), lambda i: (0, i)), pl.BlockSpec((tl, 1), lambda i: (i, 0)),
                  _const_spec((nh, FILT_FEAT_PAD)), _const_spec((nh, 1)), _const_spec((nh, 1)),
                  _const_spec((nh, nh)), _const_spec((nh, 1)), _const_spec((nh, 1)),
                  pl.BlockSpec((None, nh, oc), dirsel),
                  pl.BlockSpec((None, 1, oc), dirsel)],
        out_specs=pl.BlockSpec((tl, oc), lambda i: (i, 0)),
        out_shape=jax.ShapeDtypeStruct((2 * L, oc), f32),
        compiler_params=_cparams("parallel"),
        name="hyena_filter",
    )(jnp.asarray(feats), jnp.asarray(t_col), w1t, col(b1), col(f1), w2.T, col(b2), col(f2),
      w3.reshape(nh, 2, oc).transpose(1, 0, 2), decay.reshape(2, 1, oc))


DFT_N2 = 128
FREQ_CHAINS = 8


def _dft_tables(L):
    N, N2 = 2 * L, DFT_N2
    N1 = N // N2
    H1 = N1 // 2
    K1n = H1 + 1
    K1p = -(-K1n // 8) * 8
    k1 = np.arange(K1n)
    n1 = np.arange(N1)
    ang = 2.0 * math.pi * ((k1[:, None] * n1[None, :]) % N1) / N1
    f1 = np.zeros((2 * K1p, N1))
    f1[:K1n] = np.cos(ang)
    f1[K1p:K1p + K1n] = -np.sin(ang)
    wgt = np.full(K1n, 2.0)
    wgt[0] = wgt[H1] = 1.0
    ang = 2.0 * math.pi * ((np.arange(H1)[:, None] * k1[None, :]) % N1) / N1
    if1 = np.zeros((H1, 2 * K1p))
    if1[:, :K1n] = wgt * np.cos(ang) / N
    if1[:, K1p:K1p + K1n] = -wgt * np.sin(ang) / N
    k2 = np.arange(N2)
    ang = 2.0 * math.pi * ((k2[:, None] * k2[None, :]) % N2) / N2
    f2r, f2i = np.cos(ang), -np.sin(ang)
    ang = 2.0 * math.pi * (k1[:, None] * k2[None, :]) / N
    twr = np.zeros((K1p, 1, N2))
    twi = np.zeros((K1p, 1, N2))
    twr[:K1n, 0], twi[:K1n, 0] = np.cos(ang), -np.sin(ang)
    c = lambda a, dt: jnp.asarray(a.astype(np.float32)).astype(dt)
    return dict(N1=N1, H1=H1, K1p=K1p, f1=c(f1, bf16), if1=c(if1, bf16), f2r=c(f2r, f32), f2i=c(f2i, f32),
                twr=c(twr, f32), twi=c(twi, f32))


def _build_stage2_matrix(f2r_ref, f2i_ref, twr_ref, twi_ref, m_ref, mt_ref):
    n2 = f2r_ref.shape[0]
    twr, twi = twr_ref[...], twi_ref[...]
    re = f2r_ref[...] * twr - f2i_ref[...] * twi
    im = f2r_ref[...] * twi + f2i_ref[...] * twr
    m_ref[:n2, :n2] = re.astype(bf16)
    m_ref[:n2, n2:] = (-im).astype(bf16)
    m_ref[n2:, :n2] = im.astype(bf16)
    m_ref[n2:, n2:] = re.astype(bf16)
    if mt_ref is not None:
        ret, imt = re.T, im.T
        mt_ref[:n2, :n2] = ret.astype(bf16)
        mt_ref[:n2, n2:] = imt.astype(bf16)
        mt_ref[n2:, :n2] = (-imt).astype(bf16)
        mt_ref[n2:, n2:] = ret.astype(bf16)


def _spectrum_body(f2r_ref, f2i_ref, twr_ref, twi_ref, a_ref, x_ref, m_ref):
    _, nk, n2, c = a_ref.shape
    for kk in range(nk):
        _build_stage2_matrix(f2r_ref, f2i_ref, twr_ref.at[kk], twi_ref.at[kk], m_ref.at[kk], None)
        x = jnp.dot(m_ref[kk], a_ref[:, kk].reshape(2 * n2, c), preferred_element_type=f32)
        x_ref[kk] = x.reshape(2, n2, c)


def _filter_spectrum(tab, a5):
    NO, _, K1p, N2, C = a5.shape
    nk = FREQ_CHAINS
    tw_spec = pl.BlockSpec((nk, 1, N2), lambda k, o: (k, 0, 0))
    return pl.pallas_call(
        _spectrum_body,
        grid=(K1p // nk, NO),
        in_specs=[_const_spec((N2, N2)), _const_spec((N2, N2)), tw_spec, tw_spec,
                  pl.BlockSpec((None, 2, nk, N2, C), lambda k, o: (o, 0, k, 0, 0))],
        out_specs=pl.BlockSpec((nk, 2, N2, C), lambda k, o: (k, 0, 0, o)),
        out_shape=jax.ShapeDtypeStruct((K1p, 2, N2, NO * C), f32),
        scratch_shapes=[pltpu.VMEM((nk, 2 * N2, 2 * N2), bf16)],
        compiler_params=_cparams("parallel", "parallel"),
        name="filter_spectrum",
    )(tab['f2r'], tab['f2i'], tab['twr'], tab['twi'], a5)


def _freq_body(f2r_ref, f2i_ref, twr_ref, twi_ref, a_ref, h_ref, g_ref, m_ref, mt_ref):
    nb, _, nk, n2, c = a_ref.shape
    for kk in range(nk):
        _build_stage2_matrix(f2r_ref, f2i_ref, twr_ref.at[kk], twi_ref.at[kk], m_ref.at[kk], mt_ref.at[kk])
        hr, hi = h_ref[kk, 0], h_ref[kk, 1]
        for b in range(nb):
            x = jnp.dot(m_ref[kk], a_ref[b, :, kk].reshape(2 * n2, c), preferred_element_type=f32)
            xr, xi = x[:n2], x[n2:]
            y = jnp.concatenate([xr * hr - xi * hi, xr * hi + xi * hr], axis=0).astype(bf16)
            g = jnp.dot(mt_ref[kk], y, preferred_element_type=f32)
            g_ref[b, :, kk] = g.reshape(2, n2, c).astype(bf16)


def _freq_stage(tab, a5, hspec, order):
    B, _, K1p, N2, C = a5.shape
    nb = min(B, FREQ_CHAINS)
    nk = FREQ_CHAINS // nb
    tw_spec = pl.BlockSpec((nk, 1, N2), lambda k, b: (k, 0, 0))
    slab = pl.BlockSpec((nb, 2, nk, N2, C), lambda k, b: (b, 0, k, 0, 0))
    return pl.pallas_call(
        _freq_body,
        grid=(K1p // nk, B // nb),
        in_specs=[_const_spec((N2, N2)), _const_spec((N2, N2)), tw_spec, tw_spec, slab,
                  pl.BlockSpec((nk, 2, N2, C), lambda k, b: (k, 0, 0, order))],
        out_specs=slab,
        out_shape=jax.ShapeDtypeStruct(a5.shape, bf16),
        scratch_shapes=[pltpu.VMEM((nk, 2 * N2, 2 * N2), bf16), pltpu.VMEM((nk, 2 * N2, 2 * N2), bf16)],
        compiler_params=_cparams("parallel", "parallel"),
        name="freq_stage",
    )(tab['f2r'], tab['f2i'], tab['twr'], tab['twi'], a5, hspec)


DFT_SLABS = 16


def _slabs_to_matrix(load_tile, rows, S, C, stage_ref):
    for lt in range(C // LANE):
        stage_ref[lt, :rows * S] = load_tile(lt)
    cols = [stage_ref[lt, pl.ds(sl, rows, stride=S), :] for sl in range(S) for lt in range(C // LANE)]
    return jnp.concatenate(cols, axis=1)


def _matrix_to_slabs(mat, rows, S, C, stage_ref):
    for sl in range(S):
        for lt in range(C // LANE):
            lo = sl * C + lt * LANE
            stage_ref[lt, pl.ds(sl, rows, stride=S), :] = mat[:, lo:lo + LANE]
    return [stage_ref[lt, :rows * S].reshape(rows, S, LANE) for lt in range(C // LANE)]


def _stage1_body(w_ref, z_ref, o_ref, stage_ref):
    H1, S, C = z_ref.shape
    K1p = o_ref.shape[1]
    zmat = _slabs_to_matrix(lambda lt: z_ref[:, :, lt * LANE:(lt + 1) * LANE].reshape(H1 * S, LANE), H1, S, C, stage_ref)
    a = jnp.dot(w_ref[...], zmat.astype(bf16), preferred_element_type=f32)
    for lt, tile in enumerate(_matrix_to_slabs(a, 2 * K1p, S, C, stage_ref)):
        o_ref[0, :, :, lt * LANE:(lt + 1) * LANE] = tile[:K1p].astype(bf16)
        o_ref[1, :, :, lt * LANE:(lt + 1) * LANE] = tile[K1p:].astype(bf16)


def _data_stage1(w, z4, K1p, C):
    B, R, N2, Cw = z4.shape
    nlb = Cw // C
    S = DFT_SLABS
    return pl.pallas_call(
        _stage1_body,
        grid=(B, nlb, N2 // S),
        in_specs=[_const_spec(w.shape), pl.BlockSpec((None, R, S, C), lambda b, lb, j: (b, 0, j, lb))],
        out_specs=pl.BlockSpec((None, 2, K1p, S, C), lambda b, lb, j: (b * nlb + lb, 0, 0, j, 0)),
        out_shape=jax.ShapeDtypeStruct((B * nlb, 2, K1p, N2, C), bf16),
        scratch_shapes=[pltpu.VMEM((C // LANE, max(R, 2 * K1p) * S, LANE), f32)],
        compiler_params=_cparams("parallel", "parallel", "parallel"),
        name="dft_data_stage1",
    )(w, z4)


def _conv_out_body(w_ref, g_ref, z_ref, gate_ref, bias_ref, o_ref, stage_ref):
    _, K1p, S, C = g_ref.shape
    H1 = z_ref.shape[0]

    def load_tile(lt):
        parts = [g_ref[ri, :, :, lt * LANE:(lt + 1) * LANE].astype(f32).reshape(K1p * S, LANE) for ri in range(2)]
        return jnp.concatenate(parts, axis=0)

    gmat = _slabs_to_matrix(load_tile, 2 * K1p, S, C, stage_ref)
    y = jnp.dot(w_ref[...], gmat.astype(bf16), preferred_element_type=f32)
    for lt, tile in enumerate(_matrix_to_slabs(y, H1, S, C, stage_ref)):
        sl = slice(lt * LANE, (lt + 1) * LANE)
        o_ref[:, :, sl] = gate_ref[:, :, sl] * (tile + z_ref[:, :, sl] * bias_ref[:, sl].reshape(1, 1, LANE))


def _conv_out(w, g5, z4, gate4, bias):
    B, H1, N2, C = z4.shape
    K1p = g5.shape[2]
    S = DFT_SLABS
    blk = pl.BlockSpec((None, H1, S, C), lambda b, j: (b, 0, j, 0))
    return pl.pallas_call(
        _conv_out_body,
        grid=(B, N2 // S),
        in_specs=[_const_spec(w.shape), pl.BlockSpec((None, 2, K1p, S, C), lambda b, j: (b, 0, 0, j, 0)), blk, blk,
                  _const_spec((1, C))],
        out_specs=blk,
        out_shape=jax.ShapeDtypeStruct(z4.shape, f32),
        scratch_shapes=[pltpu.VMEM((C // LANE, 2 * K1p * S, LANE), f32)],
        compiler_params=_cparams("parallel", "parallel"),
        name="conv_out",
    )(w, g5, z4, gate4, bias)


def _long_conv(v, g1, g2, circ, hy_bias, B, L, C):
    tab = _dft_tables(L)
    N2, N1, H1, K1p = DFT_N2, tab['N1'], tab['H1'], tab['K1p']
    hspec = _filter_spectrum(tab, _data_stage1(tab['f1'], circ.reshape(1, N1, N2, circ.shape[1]), K1p, C))
    z4 = v.reshape(B, H1, N2, C)
    gates = (g1.reshape(B, H1, N2, C), g2.reshape(B, H1, N2, C))
    f1h = tab['f1'][:, :H1]
    for o in range(HY_ORDER):
        g5 = _freq_stage(tab, _data_stage1(f1h, z4, K1p, C), hspec, o)
        z4 = _conv_out(tab['if1'], g5, z4, gates[o], hy_bias[o].reshape(1, C))
    return z4.reshape(B * L, C)


def _s5_rows(B, L):
    return (True, S5_ROWS, L // S5_ROWS) if B == 1 else (False, B, L)


def _encoder_layer(x, p, s5_mats, alpha):
    B, L, D = x.shape
    n = B * L
    x2 = x.reshape(n, D)
    G, P, H = p['s5_b_re'].shape[1:]
    s5w = G * H
    hyw = p['w_hy_proj'].shape[0]

    u_g, v, g1, g2, g_s, g_h = _inproj(x2, p['w_in'].astype(bf16), p['hy_short_w'], p['hy_short_b'], B, L, G, H, hyw)

    chained, Bs, _ = _s5_rows(B, L)
    y_g = _s5(u_g, s5_mats, Bs, chained)

    circ = _circ_filters(L, p['filt_w1'], p['filt_b1'], p['filt_freq1'], p['filt_w2'], p['filt_b2'],
                         p['filt_freq2'], p['filt_w3'], p['filt_decay'])
    z = _long_conv(v, g1, g2, circ, p['hy_bias'], B, L, hyw)

    x1, aff = _merge(x2, y_g, z, g_s, g_h, p['s5_w_glu'].astype(bf16), p['w_s5_proj'].astype(bf16),
                     p['w_hy_proj'].astype(bf16), p['w_out'].astype(bf16),
                     p['ln1_g'].reshape(1, D), p['ln1_b'].reshape(1, D), p['w_router'], B, L, alpha)

    out = _expert_choice(x1, aff, p['ex_w1_bf16'], p['ex_w3_bf16'], p['ex_w2_bf16'],
                         p['ln2_g'].reshape(1, D), p['ln2_b'].reshape(1, D), alpha)
    return out.reshape(B, L, D)


_PARAM_NAMES = ('w_in', 's5_lambda_re', 's5_lambda_im', 's5_log_dt', 's5_b_re', 's5_b_im',
                's5_c_re', 's5_c_im', 's5_d', 's5_w_glu', 'w_s5_proj',
                'hy_short_w', 'hy_short_b', 'filt_w1', 'filt_b1', 'filt_freq1', 'filt_w2', 'filt_b2',
                'filt_freq2', 'filt_w3', 'filt_decay', 'hy_bias', 'w_hy_proj', 'w_out',
                'ln1_g', 'ln1_b', 'w_router', 'ex_w1', 'ex_w3', 'ex_w2', 'ln2_g', 'ln2_b')


def kernel(x_prompt, x_sample, w_in, s5_lambda_re, s5_lambda_im, s5_log_dt, s5_b_re, s5_b_im, s5_c_re, s5_c_im, s5_d, s5_w_glu, w_s5_proj, hy_short_w, hy_short_b, filt_w1, filt_b1, filt_freq1, filt_w2, filt_b2, filt_freq2, filt_w3, filt_decay, hy_bias, w_hy_proj, w_out, ln1_g, ln1_b, w_router, ex_w1, ex_w3, ex_w2, ln2_g, ln2_b):
    stacked = (w_in, s5_lambda_re, s5_lambda_im, s5_log_dt, s5_b_re, s5_b_im, s5_c_re, s5_c_im, s5_d,
               s5_w_glu, w_s5_proj, hy_short_w, hy_short_b, filt_w1, filt_b1, filt_freq1, filt_w2,
               filt_b2, filt_freq2, filt_w3, filt_decay, hy_bias, w_hy_proj, w_out, ln1_g, ln1_b,
               w_router, ex_w1, ex_w3, ex_w2, ln2_g, ln2_b)
    depth = w_in.shape[0]
    alpha = (2.0 * depth) ** 0.25
    xs = [x_prompt, x_sample]
    for l in range(depth):
        p = {k: v[l] for k, v in zip(_PARAM_NAMES, stacked)}
        for k in ('ex_w1', 'ex_w3', 'ex_w2'):
            p[k + '_bf16'] = p[k].astype(bf16)
        s5_mats = _s5_matrices(p, [_s5_rows(x.shape[0], x.shape[1])[2] for x in xs])
        xs = [_encoder_layer(x, p, mats, alpha) for x, mats in zip(xs, s5_mats)]
    return tuple(xs)
```

```python
import functools
import math

import jax
import jax.numpy as jnp
import numpy as np
from jax import lax
from jax.experimental import pallas as pl
from jax.experimental.pallas import tpu as pltpu

FILT_BANDS = 8
FILT_SHIFT = 0.05
EC_CAPACITY = 2
LN_EPS = 1e-5
HY_ORDER = 2

VMEM_LIMIT_BYTES = 56 * 1024 * 1024
ROW_TILE = 512

bf16 = jnp.bfloat16
f32 = jnp.float32


def _cparams(*sem):
    return pltpu.CompilerParams(dimension_semantics=sem, vmem_limit_bytes=VMEM_LIMIT_BYTES)


def _const_spec(shape):
    return pl.BlockSpec(shape, lambda *_: (0,) * len(shape))


def _inproj_body(xp_ref, x_ref, xn_ref, w_ref, cw_ref, cb_ref, ug_ref, v_ref, g1_ref, g2_ref, gs_ref, gh_ref,
                 stage_ref, *, o1, o2, o3, nt, H):
    i = pl.program_id(1)
    proj = jnp.dot(x_ref[...].astype(bf16), w_ref[...], preferred_element_type=f32)
    _rows_to_chunks(proj[:, :o1], stage_ref, ug_ref, H)
    gs_ref[...] = jax.nn.sigmoid(proj[:, o2:o3])
    gh_ref[...] = jax.nn.sigmoid(proj[:, o3:])
    u = proj[:, o1:o2]
    w_hy = w_ref[:, o1:o2]
    prev_row = jnp.dot(xp_ref[...].astype(bf16), w_hy, preferred_element_type=f32)[7:8]
    next_row = jnp.dot(xn_ref[...].astype(bf16), w_hy, preferred_element_type=f32)[0:1]
    prev_row = jnp.where(i == 0, 0.0, prev_row)
    next_row = jnp.where(i == nt - 1, 0.0, next_row)
    tl = u.shape[0]
    rows = lax.broadcasted_iota(jnp.int32, u.shape, 0)
    um1 = jnp.where(rows == 0, prev_row, pltpu.roll(u, 1, axis=0))
    up1 = jnp.where(rows == tl - 1, next_row, pltpu.roll(u, tl - 1, axis=0))
    cw = cw_ref[...]
    hy = um1 * cw[0:1, :] + u * cw[1:2, :] + up1 * cw[2:3, :] + cb_ref[...]
    hyw = v_ref.shape[1]
    v_ref[...] = hy[:, :hyw]
    g1_ref[...] = hy[:, hyw:2 * hyw]
    g2_ref[...] = hy[:, 2 * hyw:]


def _inproj(x2, w_in, conv_w, conv_b, B, L, G, H, hyw):
    n, D = x2.shape
    cols = w_in.shape[1]
    s5w = G * H
    o1, o2 = s5w, s5w + 3 * hyw
    o3 = o2 + D
    tm = min(ROW_TILE, L)
    nt = L // tm
    r8 = tm // 8
    row = lambda b, i: (b * nt + i, 0)
    prev = lambda b, i: (jnp.maximum((b * nt + i) * r8 - 1, 0), 0)
    nxt = lambda b, i: (jnp.minimum((b * nt + i + 1) * r8, n // 8 - 1), 0)
    hy_out = jax.ShapeDtypeStruct((n, hyw), f32)
    return pl.pallas_call(
        functools.partial(_inproj_body, o1=o1, o2=o2, o3=o3, nt=nt, H=H),
        grid=(B, nt),
        in_specs=[pl.BlockSpec((8, D), prev), pl.BlockSpec((tm, D), row), pl.BlockSpec((8, D), nxt),
                  _const_spec((D, cols)), _const_spec((3, 3 * hyw)), _const_spec((1, 3 * hyw))],
        out_specs=[pl.BlockSpec((G, tm // S5_CHUNK, S5_CHUNK * H), lambda b, i: (0, b * nt + i, 0)),
                   pl.BlockSpec((tm, hyw), row), pl.BlockSpec((tm, hyw), row), pl.BlockSpec((tm, hyw), row),
                   pl.BlockSpec((tm, D), row), pl.BlockSpec((tm, D), row)],
        out_shape=[jax.ShapeDtypeStruct((G, n // S5_CHUNK, S5_CHUNK * H), bf16), hy_out, hy_out, hy_out,
                   jax.ShapeDtypeStruct((n, D), f32), jax.ShapeDtypeStruct((n, D), f32)],
        scratch_shapes=[pltpu.VMEM((s5w // LANE, tm, LANE), f32)],
        compiler_params=_cparams("parallel", "parallel"),
        name="inproj",
    )(x2, x2, x2, w_in, conv_w, conv_b.reshape(1, 3 * hyw))


S5_CHUNK = 16
S5_ROWS = 8
LANE = 128


def _s5_body(u_ref, kmat_ref, winf_ref, winb_ref, woutf_ref, woutb_ref, af_ref, ab_ref, y_ref,
             vf_ref, vb_ref, sf_ref, sb_ref, *, R, nc, chained):
    u = u_ref[0]
    hw = sf_ref.shape[1]
    seq_rows = lambda b: pl.ds(b, nc, stride=R)
    for v_ref, w_ref in ((vf_ref, winf_ref), (vb_ref, winb_ref)):
        v = jnp.dot(u, w_ref[0], preferred_element_type=f32)
        for b in range(R):
            v_ref[0, seq_rows(b), :] = v[b * nc:(b + 1) * nc, :hw]
            v_ref[1, seq_rows(b), :] = v[b * nc:(b + 1) * nc, hw:]
    bc = lambda ref, row, lo: jnp.broadcast_to(ref[0, row:row + 1, lo:lo + hw], (R, hw))
    decay = lambda ref, row: ((bc(ref, row, 0), bc(ref, row, hw)), (bc(ref, row + 1, 0), bc(ref, row + 1, hw)))
    f_a1, f_a2 = decay(af_ref, 0)
    b_a1, b_a2 = decay(ab_ref, 0)

    def mul_add(s0, s1, a1, a2, v0, v1):
        return a1[0] * s0 + a2[0] * s1 + v0, a1[1] * s1 + a2[1] * s0 + v1

    def scan(init, store):
        def step(i, carry):
            f0, f1, b0, b1 = carry
            rf = pl.ds(pl.multiple_of(i * R, R), R)
            rb = pl.ds(pl.multiple_of((nc - 1 - i) * R, R), R)
            if store:
                sf_ref[rf, :] = f0
                sb_ref[rb, :] = b0
            f0, f1 = mul_add(f0, f1, f_a1, f_a2, vf_ref[0, rf, :], vf_ref[1, rf, :])
            b0, b1 = mul_add(b0, b1, b_a1, b_a2, vb_ref[0, rb, :], vb_ref[1, rb, :])
            return f0, f1, b0, b1
        return lax.fori_loop(0, nc, step, init, unroll=4)

    zero = jnp.zeros((R, hw), f32)
    init = (zero, zero, zero, zero)
    if chained:
        ef0, ef1, eb0, eb1 = scan(init, store=False)
        fn_a1, fn_a2 = decay(af_ref, 2)
        bn_a1, bn_a2 = decay(ab_ref, 2)
        row = lax.broadcasted_iota(jnp.int32, (R, hw), 0)
        down = lambda v: jnp.where(row == 0, 0.0, pltpu.roll(v, 1, axis=0))
        up = lambda v: jnp.where(row == R - 1, 0.0, pltpu.roll(v, R - 1, axis=0))
        f0, f1, b0, b1 = init
        for _ in range(R - 1):
            t0, t1 = mul_add(f0, f1, fn_a1, fn_a2, ef0, ef1)
            f0, f1 = down(t0), down(t1)
            t0, t1 = mul_add(b0, b1, bn_a1, bn_a2, eb0, eb1)
            b0, b1 = up(t0), up(t1)
        init = (f0, f1, b0, b1)
    scan(init, store=True)
    y = jnp.dot(u, kmat_ref[0], preferred_element_type=f32)
    for s_ref, w_ref in ((sf_ref, woutf_ref), (sb_ref, woutb_ref)):
        states = jnp.concatenate([s_ref[seq_rows(b), :] for b in range(R)], axis=0)
        y = y + jnp.dot(states.astype(bf16), w_ref[0], preferred_element_type=f32)
    y_ref[0] = y


def _s5(u_g, mats, R, chained):
    G, M, W = u_g.shape
    kmat, winf, winb, woutf, woutb, af, ab = mats
    hw = woutf.shape[1]
    grp = lambda shape: pl.BlockSpec((1,) + shape, lambda g: (g, 0, 0))
    return pl.pallas_call(
        functools.partial(_s5_body, R=R, nc=M // R, chained=chained),
        grid=(G,),
        in_specs=[grp((M, W)), grp((W, W)), grp((W, 2 * hw)), grp((W, 2 * hw)), grp((hw, W)), grp((hw, W)),
                  grp((4, 2 * hw)), grp((4, 2 * hw))],
        out_specs=grp((M, W)),
        out_shape=jax.ShapeDtypeStruct((G, M, W), f32),
        scratch_shapes=[pltpu.VMEM((2, M, hw), f32), pltpu.VMEM((2, M, hw), f32),
                        pltpu.VMEM((M, hw), f32), pltpu.VMEM((M, hw), f32)],
        compiler_params=_cparams("parallel"),
        name="s5_chunked",
    )(u_g, kmat, winf, winb, woutf, woutb, af, ab)


def _rows_to_chunks(x, stage_ref, o_ref, H):
    Tc = S5_CHUNK
    nr = x.shape[0] // Tc
    gpt, per_tile = LANE // H, LANE // H
    for lt in range(x.shape[1] // LANE):
        stage_ref[lt] = x[:, lt * LANE:(lt + 1) * LANE]
        rows_t = [stage_ref[lt, pl.ds(t, nr, stride=Tc), :] for t in range(Tc)]
        for g in range(gpt):
            for j in range(Tc // per_tile):
                tile = jnp.concatenate([rows_t[per_tile * j + k][:, H * g:H * (g + 1)] for k in range(per_tile)], axis=1)
                o_ref[lt * gpt + g, :, j * LANE:(j + 1) * LANE] = tile.astype(o_ref.dtype)


def _chunks_to_rows(y_ref, stage_ref, H):
    Tc = S5_CHUNK
    nr = y_ref.shape[1]
    gpt = LANE // H
    for lt in range(stage_ref.shape[0]):
        for t in range(Tc):
            tile = jnp.concatenate([y_ref[lt * gpt + g, :, H * t:H * (t + 1)] for g in range(gpt)], axis=1)
            stage_ref[lt, pl.ds(t, nr, stride=Tc), :] = tile
    return jnp.concatenate([stage_ref[lt] for lt in range(stage_ref.shape[0])], axis=1)


def _s5_discretise(lam_re, lam_im, log_dt, b_re, b_im, powers):
    lam = lax.complex(-jnp.abs(lam_re.astype(f32)), lam_im.astype(f32))
    dt = jnp.exp(log_dt.astype(f32))[:, None]
    a_bar = jnp.exp(lam * dt)
    k = jnp.asarray(powers, f32)[None, :, None]
    apow = jnp.exp((lam * dt)[:, None, :] * k)
    b_bar = ((a_bar - 1.0) / lam)[..., None] * lax.complex(b_re.astype(f32), b_im.astype(f32))
    return apow, b_bar


def _s5_matrices(p, seg_steps):
    Tc = S5_CHUNK
    powers = list(range(Tc + 1)) + list(seg_steps)
    apf, bbf = _s5_discretise(p['s5_lambda_re'][0], p['s5_lambda_im'][0], p['s5_log_dt'][0],
                              p['s5_b_re'][0], p['s5_b_im'][0], powers)
    apb, bbb = _s5_discretise(p['s5_lambda_re'][1], p['s5_lambda_im'][1], p['s5_log_dt'][1],
                              p['s5_b_re'][1], p['s5_b_im'][1], powers)
    c = lax.complex(p['s5_c_re'].astype(f32), p['s5_c_im'].astype(f32))
    G, H, P = c.shape
    W = Tc * H
    kf = jnp.real(jnp.einsum('ghp,gtp,gpk->gthk', c, apf[:, :Tc], bbf))
    kb = jnp.real(jnp.einsum('ghp,gtp,gpk->gthk', c, apb[:, :Tc], bbb))
    tau = jnp.arange(Tc)[None, :] - jnp.arange(Tc)[:, None]
    blk = (jnp.where((tau >= 0)[None, :, :, None, None], kf[:, jnp.abs(tau)], 0.0)
           + jnp.where((tau <= 0)[None, :, :, None, None], kb[:, jnp.abs(tau)], 0.0))
    kmat = jnp.transpose(blk, (0, 1, 4, 2, 3)).reshape(G, W, W)
    kmat = kmat + jnp.eye(W, dtype=f32)[None] * jnp.tile(p['s5_d'].astype(f32), (1, Tc))[:, None, :]
    lanes = lambda z: jnp.concatenate([jnp.real(z), jnp.imag(z), jnp.imag(z), jnp.real(z)], axis=-1)
    win = lambda ap, bb: lanes(jnp.einsum('gsp,gpk->gskp', ap, bb)).reshape(G, W, 4 * P)
    winf = win(apf[:, Tc - 1::-1][:, :Tc], bbf)
    winb = win(apb[:, :Tc], bbb)
    def wout(ap):
        z = jnp.einsum('ghp,gtp->gpth', c, ap).reshape(G, P, W)
        return jnp.concatenate([jnp.real(z), -jnp.imag(z)], axis=1)
    woutf = wout(apf[:, 1:Tc + 1])
    woutb = wout(apb[:, Tc:0:-1])
    def chunk_decay(ap, seg):
        rows = []
        for k in (Tc, Tc + 1 + seg):
            ar, ai = jnp.real(ap[:, k]), jnp.imag(ap[:, k])
            rows += [jnp.concatenate([ar, ar, ar, ar], -1), jnp.concatenate([-ai, ai, ai, -ai], -1)]
        return jnp.stack(rows, axis=1)
    cast = lambda m: m.astype(bf16)
    shared = (cast(kmat), cast(winf), cast(winb), cast(woutf), cast(woutb))
    return [shared + (chunk_decay(apf, i), chunk_decay(apb, i)) for i in range(len(seg_steps))]


MERGE_SPLIT = 2


def _layer_norm(v, g, b):
    mu = jnp.mean(v, axis=-1, keepdims=True)
    c = v - mu
    var = jnp.mean(c * c, axis=-1, keepdims=True)
    return c * lax.rsqrt(var + LN_EPS) * g + b


def _merge_body(x_ref, yg_ref, z_ref, gs_ref, gh_ref, wglu_ref, wsp_ref, whp_ref, wout_ref,
                g_ref, b_ref, wrh_ref, wrl_ref, x1_ref, aff_ref, stage_ref, *, alpha, H):
    ys_all = jax.nn.gelu(_chunks_to_rows(yg_ref, stage_ref, H))
    tm = x_ref.shape[0]
    for r0 in range(0, tm, tm // MERGE_SPLIT):
        rs = slice(r0, r0 + tm // MERGE_SPLIT)
        ys = ys_all[rs]
        gate = jax.nn.sigmoid(jnp.dot(ys.astype(bf16), wglu_ref[...], preferred_element_type=f32))
        branch_s = jnp.dot((ys * gate).astype(bf16), wsp_ref[...], preferred_element_type=f32)
        branch_h = jnp.dot(z_ref[rs, :].astype(bf16), whp_ref[...], preferred_element_type=f32)
        mix = gs_ref[rs, :] * branch_s + gh_ref[rs, :] * branch_h
        mix = jnp.dot(mix.astype(bf16), wout_ref[...], preferred_element_type=f32)
        x1 = _layer_norm(alpha * x_ref[rs, :] + mix, g_ref[...], b_ref[...])
        x1_ref[rs, :] = x1
        x1_hi = x1.astype(bf16)
        x1_lo = (x1 - x1_hi.astype(f32)).astype(bf16)
        logits = (jnp.dot(x1_hi, wrh_ref[...], preferred_element_type=f32)
                  + jnp.dot(x1_lo, wrh_ref[...], preferred_element_type=f32)
                  + jnp.dot(x1_hi, wrl_ref[...], preferred_element_type=f32))
        m = jnp.max(logits, axis=-1, keepdims=True)
        e = jnp.exp(logits - m)
        aff_ref[rs, :] = e / jnp.sum(e, axis=-1, keepdims=True)


def _merge(x2, y_g, z, g_s, g_h, w_glu, w_sp, w_hp, w_out, ln_g, ln_b, w_router, B, L, alpha):
    n, D = x2.shape
    G, _, W = y_g.shape
    H = W // S5_CHUNK
    s5w, hyw = G * H, z.shape[1]
    E = w_router.shape[1]
    tm = min(ROW_TILE, L)
    nt = L // tm
    row = lambda b, i: (b * nt + i, 0)
    wr_hi = w_router.astype(bf16)
    wr_lo = (w_router - wr_hi.astype(f32)).astype(bf16)
    return pl.pallas_call(
        functools.partial(_merge_body, alpha=alpha, H=H),
        grid=(B, nt),
        in_specs=[pl.BlockSpec((tm, D), row),
                  pl.BlockSpec((G, tm // S5_CHUNK, W), lambda b, i: (0, b * nt + i, 0)),
                  pl.BlockSpec((tm, hyw), row),
                  pl.BlockSpec((tm, D), row),
                  pl.BlockSpec((tm, D), row),
                  _const_spec(w_glu.shape), _const_spec(w_sp.shape), _const_spec(w_hp.shape),
                  _const_spec(w_out.shape), _const_spec((1, D)), _const_spec((1, D)),
                  _const_spec(w_router.shape), _const_spec(w_router.shape)],
        out_specs=[pl.BlockSpec((tm, D), row), pl.BlockSpec((tm, E), row)],
        out_shape=[jax.ShapeDtypeStruct((n, D), f32), jax.ShapeDtypeStruct((n, E), f32)],
        scratch_shapes=[pltpu.VMEM((s5w // LANE, tm, LANE), f32)],
        compiler_params=_cparams("parallel", "parallel"),
        name="merge",
    )(x2, y_g, z, g_s, g_h, w_glu, w_sp, w_hp, w_out, ln_g, ln_b, wr_hi, wr_lo)


ROUTE_BLOCK = 256
ROUTE_SLOTS = 64
ROW_ALIGN = 8


def _select_body(aff_ref, sel_ref, *, cap, idx_bits):
    bits = pltpu.bitcast(aff_ref[...], jnp.int32)
    E = bits.shape[0]
    count = lambda m: jnp.sum(jnp.where(m, 1.0, 0.0), axis=1, keepdims=True)

    def value_bit(i, prefix):
        cand = prefix | jnp.left_shift(jnp.int32(1), 30 - i)
        return jnp.where(count(bits >= cand) >= cap, cand, prefix)

    thr = lax.fori_loop(0, 31, value_bit, jnp.zeros((E, 1), jnp.int32))
    need = cap - count(bits > thr)
    idx = lax.broadcasted_iota(jnp.int32, bits.shape, 1)
    tie_idx = jnp.where(bits == thr, idx, jnp.int32(2 ** 30))

    def index_bit(i, bound):
        cand = bound | jnp.left_shift(jnp.int32(1), idx_bits - 1 - i)
        return jnp.where(count(tie_idx < cand) <= need, cand, bound)

    bound = lax.fori_loop(0, idx_bits, index_bit, jnp.zeros((E, 1), jnp.int32))
    sel_ref[...] = jnp.where(bits > thr, 1.0, jnp.where(tie_idx < bound, 1.0, 0.0))


def _select(aff_t, cap):
    E, n = aff_t.shape
    return pl.pallas_call(
        functools.partial(_select_body, cap=float(cap), idx_bits=int(n).bit_length()),
        out_shape=jax.ShapeDtypeStruct((E, n), f32),
        compiler_params=pltpu.CompilerParams(vmem_limit_bytes=VMEM_LIMIT_BYTES),
        name="expert_select",
    )(aff_t)


def _slot_onehot(sel, p_ref, first_slot, E, weight=None):
    Tb = sel.shape[1]
    S = p_ref.shape[0] // E
    r = lax.broadcasted_iota(jnp.int32, (Tb, Tb), 0)
    c = lax.broadcasted_iota(jnp.int32, (Tb, Tb), 1)
    tri = jnp.where(r <= c, 1.0, 0.0).astype(bf16)
    incl = jnp.dot(sel.astype(bf16), tri, preferred_element_type=f32)
    slot = jnp.where(sel > 0.0, incl - 1.0, -1.0)
    want = (lax.broadcasted_iota(jnp.int32, (S, Tb), 0) + first_slot).astype(f32)
    for e in range(E):
        hit = slot[e:e + 1] == want
        entry = 1.0 if weight is None else weight[e:e + 1]
        p_ref[e * S:(e + 1) * S, :] = jnp.where(hit, entry, 0.0).astype(bf16)


def _gather_copies(stage_ref, xe_hbm, sem, off_ref, j, first_slot, E, S, nblk):
    copies = []
    for e in range(E):
        row = pl.multiple_of(off_ref[e * nblk + j] + first_slot, ROW_ALIGN)
        copies.append(pltpu.make_async_copy(stage_ref.at[e * S:(e + 1) * S], xe_hbm.at[e, pl.ds(row, S)], sem))
    return copies


def _gather_body(off_ref, rounds_ref, total_ref, sel_ref, x_ref, xe_hbm, p_ref, stage_ref, sem, nwrites_ref,
                 *, E, nblk, cap):
    j = pl.program_id(0)
    S = p_ref.shape[0] // E
    D = x_ref.shape[1]

    @pl.when(j == 0)
    def _():
        nwrites_ref[0] = 0

    x = x_ref[...].astype(bf16)

    def one_round(r):
        w = nwrites_ref[0]
        slot = lax.rem(w, 2)
        first_slot = r * S
        _slot_onehot(sel_ref[...], p_ref, first_slot, E)
        stage_ref[slot] = jnp.dot(p_ref[...], x, preferred_element_type=f32)

        @pl.when(w > 0)
        def _():
            for cp in _gather_copies(stage_ref.at[1 - slot], xe_hbm, sem.at[1 - slot], off_ref, j, 0, E, S, nblk):
                cp.wait()

        for cp in _gather_copies(stage_ref.at[slot], xe_hbm, sem.at[slot], off_ref, j, first_slot, E, S, nblk):
            cp.start()
        nwrites_ref[0] = w + 1

    one_round(0)

    def extra(r, carry):
        one_round(r)
        return carry

    lax.fori_loop(1, rounds_ref[j], extra, 0)

    @pl.when(j == nblk - 1)
    def _():
        last = lax.rem(nwrites_ref[0] - 1, 2)
        for cp in _gather_copies(stage_ref.at[last], xe_hbm, sem.at[last], off_ref, j, 0, E, S, nblk):
            cp.wait()
        rows_pad = xe_hbm.shape[1]
        zeros_ref = stage_ref.at[0, :S]
        zeros_ref[...] = jnp.zeros((S, D), f32)
        nfill = -(-(rows_pad - cap) // S)

        def fill(e, row):
            return pltpu.make_async_copy(zeros_ref, xe_hbm.at[e, pl.ds(pl.multiple_of(row, ROW_ALIGN), S)], sem.at[0])

        def whole_chunks(e, action):
            def body(k, c):
                row = total_ref[e] + k * S

                @pl.when(row + S <= rows_pad)
                def _():
                    action(fill(e, row))
                return c
            lax.fori_loop(0, nfill, body, 0)

        for e in range(E):
            whole_chunks(e, lambda cp: cp.start())
        for e in range(E):
            whole_chunks(e, lambda cp: cp.wait())
        for e in range(E):
            fill(e, rows_pad - S).start()
        for e in range(E):
            fill(e, rows_pad - S).wait()


def _route_gather(sel, x1, off, rounds, total, rows_pad):
    E, n = sel.shape
    D = x1.shape[1]
    Tb, S = ROUTE_BLOCK, ROUTE_SLOTS
    nblk = n // Tb
    return pl.pallas_call(
        functools.partial(_gather_body, E=E, nblk=nblk, cap=EC_CAPACITY * n // E),
        grid_spec=pltpu.PrefetchScalarGridSpec(
            num_scalar_prefetch=3,
            grid=(nblk,),
            in_specs=[pl.BlockSpec((E, Tb), lambda j, *_: (0, j)),
                      pl.BlockSpec((Tb, D), lambda j, *_: (j, 0))],
            out_specs=pl.BlockSpec(memory_space=pl.ANY),
            scratch_shapes=[pltpu.VMEM((E * S, Tb), bf16),
                            pltpu.VMEM((2, E * S, D), f32),
                            pltpu.SemaphoreType.DMA((2,)),
                            pltpu.SMEM((1,), jnp.int32)]),
        out_shape=jax.ShapeDtypeStruct((E, rows_pad, D), f32),
        compiler_params=_cparams("arbitrary"),
        name="route_gather",
    )(off, rounds, total, sel, x1)


def _ffn_body(total_ref, xe_ref, w1_ref, w3_ref, w2_ref, ye_ref):
    e, r = pl.program_id(0), pl.program_id(1)
    tr = xe_ref.shape[1]
    D = ye_ref.shape[2]

    @pl.when(r * tr < total_ref[e])
    def _():
        xe = xe_ref[0].astype(bf16)
        h1 = jnp.dot(xe, w1_ref[0], preferred_element_type=f32)
        h3 = jnp.dot(xe, w3_ref[0], preferred_element_type=f32)
        h = (jax.nn.silu(h1) * h3).astype(bf16)
        ye = jnp.dot(h, w2_ref[0], preferred_element_type=f32)
        row = r * tr + lax.broadcasted_iota(jnp.int32, ye.shape, 0)
        ye_ref[0] = jnp.where(row < total_ref[e], ye, 0.0)

    @pl.when(r * tr >= total_ref[e])
    def _():
        ye_ref[...] = jnp.zeros_like(ye_ref)


def _ffn(total, xe, w1, w3, w2):
    E, rows_pad, Dx = xe.shape
    D, F = w1.shape[1:]
    tr = ROW_TILE
    last = lambda e, tot: (tot[e] - 1) // tr
    rowmap = lambda e, r, tot: (e, jnp.minimum(r, last(e, tot)), 0)
    wmap = lambda e, r, tot: (e, 0, 0)
    return pl.pallas_call(
        _ffn_body,
        grid_spec=pltpu.PrefetchScalarGridSpec(
            num_scalar_prefetch=1,
            grid=(E, rows_pad // tr),
            in_specs=[pl.BlockSpec((1, tr, Dx), rowmap),
                      pl.BlockSpec((1, D, F), wmap), pl.BlockSpec((1, D, F), wmap),
                      pl.BlockSpec((1, F, D), wmap)],
            out_specs=pl.BlockSpec((1, tr, D), lambda e, r, tot: (e, r, 0))),
        out_shape=jax.ShapeDtypeStruct((E, rows_pad, D), f32),
        compiler_params=_cparams("arbitrary", "arbitrary"),
        name="expert_ffn",
    )(total, xe, w1, w3, w2)


def _scatter_copies(ye_hbm, buf_ref, sem, off_ref, j, first_slot, E, S, nblk):
    copies = []
    for e in range(E):
        row = pl.multiple_of(off_ref[e * nblk + j] + first_slot, ROW_ALIGN)
        copies.append(pltpu.make_async_copy(ye_hbm.at[e, pl.ds(row, S)], buf_ref.at[e * S:(e + 1) * S], sem))
    return copies


def _scatter_body(off_ref, rounds_ref, sel_ref, aff_ref, x1_ref, g_ref, b_ref, ye_hbm, o_ref, p_ref, buf_ref, sem,
                  *, E, nblk, alpha):
    j = pl.program_id(0)
    S = p_ref.shape[0] // E
    slot = lax.rem(j, 2)

    def fetch(jj, first_slot, s):
        return _scatter_copies(ye_hbm, buf_ref.at[s], sem.at[s], off_ref, jj, first_slot, E, S, nblk)

    @pl.when(j == 0)
    def _():
        for cp in fetch(j, 0, 0):
            cp.start()

    @pl.when(j + 1 < nblk)
    def _():
        for cp in fetch(j + 1, 0, 1 - slot):
            cp.start()

    def one_round(r, s):
        _slot_onehot(sel_ref[...], p_ref, r * S, E, weight=aff_ref[...])
        ye = buf_ref[s].astype(bf16)
        return lax.dot_general(p_ref[...], ye, (((0,), (0,)), ((), ())), preferred_element_type=f32)

    for cp in fetch(j, 0, slot):
        cp.wait()
    moe = one_round(jnp.int32(0), slot)

    def extra(r, acc):
        for cp in fetch(j, r * S, 2):
            cp.start()
        for cp in fetch(j, r * S, 2):
            cp.wait()
        return acc + one_round(r, 2)

    moe = lax.fori_loop(1, rounds_ref[j], extra, moe)
    o_ref[...] = _layer_norm(alpha * x1_ref[...] + moe, g_ref[...], b_ref[...])


def _route_scatter(sel, aff_t, x1, ye, off, rounds, ln_g, ln_b, alpha):
    E, n = sel.shape
    D = x1.shape[1]
    Tb, S = ROUTE_BLOCK, ROUTE_SLOTS
    nblk = n // Tb
    return pl.pallas_call(
        functools.partial(_scatter_body, E=E, nblk=nblk, alpha=alpha),
        grid_spec=pltpu.PrefetchScalarGridSpec(
            num_scalar_prefetch=2,
            grid=(nblk,),
            in_specs=[pl.BlockSpec((E, Tb), lambda j, *_: (0, j)),
                      pl.BlockSpec((E, Tb), lambda j, *_: (0, j)),
                      pl.BlockSpec((Tb, D), lambda j, *_: (j, 0)),
                      pl.BlockSpec((1, D), lambda j, *_: (0, 0)),
                      pl.BlockSpec((1, D), lambda j, *_: (0, 0)),
                      pl.BlockSpec(memory_space=pl.ANY)],
            out_specs=pl.BlockSpec((Tb, D), lambda j, *_: (j, 0)),
            scratch_shapes=[pltpu.VMEM((E * S, Tb), bf16),
                            pltpu.VMEM((3, E * S, D), f32),
                            pltpu.SemaphoreType.DMA((3,))]),
        out_shape=jax.ShapeDtypeStruct((n, D), f32),
        compiler_params=_cparams("arbitrary"),
        name="route_scatter",
    )(off, rounds, sel, aff_t, x1, ln_g, ln_b, ye)


def _expert_choice(x1, aff, w1, w3, w2, ln_g, ln_b, alpha):
    n, D = x1.shape
    E = aff.shape[1]
    cap = EC_CAPACITY * n // E
    Tb, S = ROUTE_BLOCK, ROUTE_SLOTS
    nblk = n // Tb
    aff_t = aff.T
    sel = _select(aff_t, cap)
    cnt = sel.reshape(E, nblk, Tb).sum(-1).astype(jnp.int32)
    cnt_al = (cnt + ROW_ALIGN - 1) // ROW_ALIGN * ROW_ALIGN
    end = jnp.cumsum(cnt_al, axis=1)
    off = (end - cnt_al).reshape(E * nblk)
    total = end[:, -1]
    rounds = jnp.maximum((jnp.max(cnt, axis=0) + S - 1) // S, 1)
    rows_pad = -(-(cap + ROW_ALIGN * nblk + Tb) // ROW_TILE) * ROW_TILE
    xe = _route_gather(sel, x1, off, rounds, total, rows_pad)
    ye = _ffn(total, xe, w1, w3, w2)
    return _route_scatter(sel, aff_t, x1, ye, off, rounds, ln_g, ln_b, alpha)


FILT_FEAT_PAD = 32


def _lag_features(L):
    n = np.arange(2 * L, dtype=np.float64)
    pos = np.where(n < L, n, 2 * L - n)
    t = pos / (L - 1)
    bands = np.linspace(1e-4, FILT_BANDS - 1, FILT_BANDS)
    ang = 2.0 * math.pi * (pos / L)[None, :] * bands[:, None]
    z = np.concatenate([t[None, :], np.cos(ang), -np.sin(ang)], axis=0)
    out = np.zeros((FILT_FEAT_PAD, 2 * L), np.float32)
    out[:z.shape[0]] = z
    return out, t.astype(np.float32)[:, None]


def _filter_body(z_ref, t_ref, w1_ref, b1_ref, f1_ref, w2_ref, b2_ref, f2_ref, w3_ref, dec_ref, o_ref, *, L):
    hp = lax.Precision.HIGHEST
    h = jnp.sin(f1_ref[...] * (jnp.dot(w1_ref[...], z_ref[...], precision=hp, preferred_element_type=f32) + b1_ref[...]))
    h = jnp.sin(f2_ref[...] * (jnp.dot(w2_ref[...], h, precision=hp, preferred_element_type=f32) + b2_ref[...]))
    h = lax.dot_general(h, w3_ref[...], (((0,), (0,)), ((), ())), precision=hp, preferred_element_type=f32)
    window = jnp.exp(-t_ref[...] * jnp.abs(dec_ref[...])) + FILT_SHIFT
    tl = h.shape[0]
    row = pl.program_id(0) * tl + lax.broadcasted_iota(jnp.int32, h.shape, 0)
    o_ref[...] = jnp.where(row == L, 0.0, h * window)


def _circ_filters(L, w1, b1, f1, w2, b2, f2, w3, decay):
    nh = w1.shape[1]
    oc = w3.shape[1] // 2
    feats, t_col = _lag_features(L)
    w1t = jnp.zeros((nh, FILT_FEAT_PAD), f32).at[:, :w1.shape[0]].set(w1.T)
    tl = min(ROW_TILE, L)
    half = L // tl
    dirsel = lambda i: (i // half, 0, 0)
    col = lambda a: a.reshape(nh, 1)
    return pl.pallas_call(
        functools.partial(_filter_body, L=L),
        grid=(2 * L // tl,),
        in_specs=[pl.BlockSpec((FILT_FEAT_PAD, tl), lambda i: (0, i)), pl.BlockSpec((tl, 1), lambda i: (i, 0)),
                  _const_spec((nh, FILT_FEAT_PAD)), _const_spec((nh, 1)), _const_spec((nh, 1)),
                  _const_spec((nh, nh)), _const_spec((nh, 1)), _const_spec((nh, 1)),
                  pl.BlockSpec((None, nh, oc), dirsel),
                  pl.BlockSpec((None, 1, oc), dirsel)],
        out_specs=pl.BlockSpec((tl, oc), lambda i: (i, 0)),
        out_shape=jax.ShapeDtypeStruct((2 * L, oc), f32),
        compiler_params=_cparams("parallel"),
        name="hyena_filter",
    )(jnp.asarray(feats), jnp.asarray(t_col), w1t, col(b1), col(f1), w2.T, col(b2), col(f2),
      w3.reshape(nh, 2, oc).transpose(1, 0, 2), decay.reshape(2, 1, oc))


DFT_N2 = 128
FREQ_CHAINS = 8


def _dft_tables(L):
    N, N2 = 2 * L, DFT_N2
    N1 = N // N2
    H1 = N1 // 2
    K1n = H1 + 1
    K1p = -(-K1n // 8) * 8
    k1 = np.arange(K1n)
    n1 = np.arange(N1)
    ang = 2.0 * math.pi * ((k1[:, None] * n1[None, :]) % N1) / N1
    f1 = np.zeros((2 * K1p, N1))
    f1[:K1n] = np.cos(ang)
    f1[K1p:K1p + K1n] = -np.sin(ang)
    wgt = np.full(K1n, 2.0)
    wgt[0] = wgt[H1] = 1.0
    ang = 2.0 * math.pi * ((np.arange(H1)[:, None] * k1[None, :]) % N1) / N1
    if1 = np.zeros((H1, 2 * K1p))
    if1[:, :K1n] = wgt * np.cos(ang) / N
    if1[:, K1p:K1p + K1n] = -wgt * np.sin(ang) / N
    k2 = np.arange(N2)
    ang = 2.0 * math.pi * ((k2[:, None] * k2[None, :]) % N2) / N2
    f2r, f2i = np.cos(ang), -np.sin(ang)
    ang = 2.0 * math.pi * (k1[:, None] * k2[None, :]) / N
    twr = np.zeros((K1p, 1, N2))
    twi = np.zeros((K1p, 1, N2))
    twr[:K1n, 0], twi[:K1n, 0] = np.cos(ang), -np.sin(ang)
    c = lambda a, dt: jnp.asarray(a.astype(np.float32)).astype(dt)
    return dict(N1=N1, H1=H1, K1p=K1p, f1=c(f1, bf16), if1=c(if1, bf16), f2r=c(f2r, f32), f2i=c(f2i, f32),
                twr=c(twr, f32), twi=c(twi, f32))


def _build_stage2_matrix(f2r_ref, f2i_ref, twr_ref, twi_ref, m_ref, mt_ref):
    n2 = f2r_ref.shape[0]
    twr, twi = twr_ref[...], twi_ref[...]
    re = f2r_ref[...] * twr - f2i_ref[...] * twi
    im = f2r_ref[...] * twi + f2i_ref[...] * twr
    m_ref[:n2, :n2] = re.astype(bf16)
    m_ref[:n2, n2:] = (-im).astype(bf16)
    m_ref[n2:, :n2] = im.astype(bf16)
    m_ref[n2:, n2:] = re.astype(bf16)
    if mt_ref is not None:
        ret, imt = re.T, im.T
        mt_ref[:n2, :n2] = ret.astype(bf16)
        mt_ref[:n2, n2:] = imt.astype(bf16)
        mt_ref[n2:, :n2] = (-imt).astype(bf16)
        mt_ref[n2:, n2:] = ret.astype(bf16)


def _spectrum_body(f2r_ref, f2i_ref, twr_ref, twi_ref, a_ref, x_ref, m_ref):
    _, nk, n2, c = a_ref.shape
    for kk in range(nk):
        _build_stage2_matrix(f2r_ref, f2i_ref, twr_ref.at[kk], twi_ref.at[kk], m_ref.at[kk], None)
        x = jnp.dot(m_ref[kk], a_ref[:, kk].reshape(2 * n2, c), preferred_element_type=f32)
        x_ref[kk] = x.reshape(2, n2, c)


def _filter_spectrum(tab, a5):
    NO, _, K1p, N2, C = a5.shape
    nk = FREQ_CHAINS
    tw_spec = pl.BlockSpec((nk, 1, N2), lambda k, o: (k, 0, 0))
    return pl.pallas_call(
        _spectrum_body,
        grid=(K1p // nk, NO),
        in_specs=[_const_spec((N2, N2)), _const_spec((N2, N2)), tw_spec, tw_spec,
                  pl.BlockSpec((None, 2, nk, N2, C), lambda k, o: (o, 0, k, 0, 0))],
        out_specs=pl.BlockSpec((nk, 2, N2, C), lambda k, o: (k, 0, 0, o)),
        out_shape=jax.ShapeDtypeStruct((K1p, 2, N2, NO * C), f32),
        scratch_shapes=[pltpu.VMEM((nk, 2 * N2, 2 * N2), bf16)],
        compiler_params=_cparams("parallel", "parallel"),
        name="filter_spectrum",
    )(tab['f2r'], tab['f2i'], tab['twr'], tab['twi'], a5)


def _freq_body(f2r_ref, f2i_ref, twr_ref, twi_ref, a_ref, h_ref, g_ref, m_ref, mt_ref):
    nb, _, nk, n2, c = a_ref.shape
    for kk in range(nk):
        _build_stage2_matrix(f2r_ref, f2i_ref, twr_ref.at[kk], twi_ref.at[kk], m_ref.at[kk], mt_ref.at[kk])
        hr, hi = h_ref[kk, 0], h_ref[kk, 1]
        for b in range(nb):
            x = jnp.dot(m_ref[kk], a_ref[b, :, kk].reshape(2 * n2, c), preferred_element_type=f32)
            xr, xi = x[:n2], x[n2:]
            y = jnp.concatenate([xr * hr - xi * hi, xr * hi + xi * hr], axis=0).astype(bf16)
            g = jnp.dot(mt_ref[kk], y, preferred_element_type=f32)
            g_ref[b, :, kk] = g.reshape(2, n2, c).astype(bf16)


def _freq_stage(tab, a5, hspec, order):
    B, _, K1p, N2, C = a5.shape
    nb = min(B, FREQ_CHAINS)
    nk = FREQ_CHAINS // nb
    tw_spec = pl.BlockSpec((nk, 1, N2), lambda k, b: (k, 0, 0))
    slab = pl.BlockSpec((nb, 2, nk, N2, C), lambda k, b: (b, 0, k, 0, 0))
    return pl.pallas_call(
        _freq_body,
        grid=(K1p // nk, B // nb),
        in_specs=[_const_spec((N2, N2)), _const_spec((N2, N2)), tw_spec, tw_spec, slab,
                  pl.BlockSpec((nk, 2, N2, C), lambda k, b: (k, 0, 0, order))],
        out_specs=slab,
        out_shape=jax.ShapeDtypeStruct(a5.shape, bf16),
        scratch_shapes=[pltpu.VMEM((nk, 2 * N2, 2 * N2), bf16), pltpu.VMEM((nk, 2 * N2, 2 * N2), bf16)],
        compiler_params=_cparams("parallel", "parallel"),
        name="freq_stage",
    )(tab['f2r'], tab['f2i'], tab['twr'], tab['twi'], a5, hspec)


DFT_SLABS = 16


def _slabs_to_matrix(load_tile, rows, S, C, stage_ref):
    for lt in range(C // LANE):
        stage_ref[lt, :rows * S] = load_tile(lt)
    cols = [stage_ref[lt, pl.ds(sl, rows, stride=S), :] for sl in range(S) for lt in range(C // LANE)]
    return jnp.concatenate(cols, axis=1)


def _matrix_to_slabs(mat, rows, S, C, stage_ref):
    for sl in range(S):
        for lt in range(C // LANE):
            lo = sl * C + lt * LANE
            stage_ref[lt, pl.ds(sl, rows, stride=S), :] = mat[:, lo:lo + LANE]
    return [stage_ref[lt, :rows * S].reshape(rows, S, LANE) for lt in range(C // LANE)]


STREAM_BLOCK_BYTES = 4 * 1024 * 1024


def _stage1_body(w_ref, z_hbm, o_ref, stage_ref, zbuf, sem, *, nbuf, nlb, nj):
    _, R, S, C = zbuf.shape
    K1p = o_ref.shape[1]
    t = (pl.program_id(0) * nlb + pl.program_id(1)) * nj + pl.program_id(2)
    steps = pl.num_programs(0) * nlb * nj

    def block_copy(step):
        j = lax.rem(step, nj)
        lb = lax.rem(step // nj, nlb)
        b = step // (nj * nlb)
        slot = lax.rem(step, nbuf)
        src = z_hbm.at[b, :, pl.ds(pl.multiple_of(j * S, S), S), pl.ds(pl.multiple_of(lb * C, C), C)]
        return pltpu.make_async_copy(src, zbuf.at[slot], sem.at[slot])

    @pl.when(t == 0)
    def _():
        for k in range(nbuf - 1):
            @pl.when(k < steps)
            def _():
                block_copy(jnp.int32(k)).start()

    @pl.when(t + nbuf - 1 < steps)
    def _():
        block_copy(t + nbuf - 1).start()

    block_copy(t).wait()
    z_ref = zbuf.at[lax.rem(t, nbuf)]
    zmat = _slabs_to_matrix(lambda lt: z_ref[:, :, lt * LANE:(lt + 1) * LANE].reshape(R * S, LANE), R, S, C, stage_ref)
    a = jnp.dot(w_ref[...], zmat.astype(bf16), preferred_element_type=f32)
    for lt, tile in enumerate(_matrix_to_slabs(a, 2 * K1p, S, C, stage_ref)):
        o_ref[0, :, :, lt * LANE:(lt + 1) * LANE] = tile[:K1p].astype(bf16)
        o_ref[1, :, :, lt * LANE:(lt + 1) * LANE] = tile[K1p:].astype(bf16)


def _data_stage1(w, z4, K1p, C):
    B, R, N2, Cw = z4.shape
    nlb = Cw // C
    S = DFT_SLABS
    nbuf = 3 if R * S * C * 4 <= STREAM_BLOCK_BYTES else 2
    return pl.pallas_call(
        functools.partial(_stage1_body, nbuf=nbuf, nlb=nlb, nj=N2 // S),
        grid=(B, nlb, N2 // S),
        in_specs=[_const_spec(w.shape), pl.BlockSpec(memory_space=pl.ANY)],
        out_specs=pl.BlockSpec((None, 2, K1p, S, C), lambda b, lb, j: (b * nlb + lb, 0, 0, j, 0)),
        out_shape=jax.ShapeDtypeStruct((B * nlb, 2, K1p, N2, C), bf16),
        scratch_shapes=[pltpu.VMEM((C // LANE, max(R, 2 * K1p) * S, LANE), f32),
                        pltpu.VMEM((nbuf, R, S, C), f32),
                        pltpu.SemaphoreType.DMA((nbuf,))],
        compiler_params=_cparams("arbitrary", "arbitrary", "arbitrary"),
        name="dft_data_stage1",
    )(w, z4)


def _conv_out_body(w_ref, g_ref, z_ref, gate_ref, bias_ref, o_ref, stage_ref):
    _, K1p, S, C = g_ref.shape
    H1 = z_ref.shape[0]

    def load_tile(lt):
        parts = [g_ref[ri, :, :, lt * LANE:(lt + 1) * LANE].astype(f32).reshape(K1p * S, LANE) for ri in range(2)]
        return jnp.concatenate(parts, axis=0)

    gmat = _slabs_to_matrix(load_tile, 2 * K1p, S, C, stage_ref)
    y = jnp.dot(w_ref[...], gmat.astype(bf16), preferred_element_type=f32)
    for lt, tile in enumerate(_matrix_to_slabs(y, H1, S, C, stage_ref)):
        sl = slice(lt * LANE, (lt + 1) * LANE)
        o_ref[:, :, sl] = gate_ref[:, :, sl] * (tile + z_ref[:, :, sl] * bias_ref[:, sl].reshape(1, 1, LANE))


def _conv_out(w, g5, z4, gate4, bias):
    B, H1, N2, C = z4.shape
    K1p = g5.shape[2]
    S = DFT_SLABS
    blk = pl.BlockSpec((None, H1, S, C), lambda b, j: (b, 0, j, 0))
    return pl.pallas_call(
        _conv_out_body,
        grid=(B, N2 // S),
        in_specs=[_const_spec(w.shape), pl.BlockSpec((None, 2, K1p, S, C), lambda b, j: (b, 0, 0, j, 0)), blk, blk,
                  _const_spec((1, C))],
        out_specs=blk,
        out_shape=jax.ShapeDtypeStruct(z4.shape, f32),
        scratch_shapes=[pltpu.VMEM((C // LANE, 2 * K1p * S, LANE), f32)],
        compiler_params=_cparams("parallel", "parallel"),
        name="conv_out",
    )(w, g5, z4, gate4, bias)


def _long_conv(v, g1, g2, circ, hy_bias, B, L, C):
    tab = _dft_tables(L)
    N2, N1, H1, K1p = DFT_N2, tab['N1'], tab['H1'], tab['K1p']
    hspec = _filter_spectrum(tab, _data_stage1(tab['f1'], circ.reshape(1, N1, N2, circ.shape[1]), K1p, C))
    z4 = v.reshape(B, H1, N2, C)
    gates = (g1.reshape(B, H1, N2, C), g2.reshape(B, H1, N2, C))
    f1h = tab['f1'][:, :H1]
    for o in range(HY_ORDER):
        g5 = _freq_stage(tab, _data_stage1(f1h, z4, K1p, C), hspec, o)
        z4 = _conv_out(tab['if1'], g5, z4, gates[o], hy_bias[o].reshape(1, C))
    return z4.reshape(B * L, C)


def _s5_rows(B, L):
    return (True, S5_ROWS, L // S5_ROWS) if B == 1 else (False, B, L)


def _encoder_layer(x, p, s5_mats, alpha):
    B, L, D = x.shape
    n = B * L
    x2 = x.reshape(n, D)
    G, P, H = p['s5_b_re'].shape[1:]
    s5w = G * H
    hyw = p['w_hy_proj'].shape[0]

    u_g, v, g1, g2, g_s, g_h = _inproj(x2, p['w_in'].astype(bf16), p['hy_short_w'], p['hy_short_b'], B, L, G, H, hyw)

    chained, Bs, _ = _s5_rows(B, L)
    y_g = _s5(u_g, s5_mats, Bs, chained)

    circ = _circ_filters(L, p['filt_w1'], p['filt_b1'], p['filt_freq1'], p['filt_w2'], p['filt_b2'],
                         p['filt_freq2'], p['filt_w3'], p['filt_decay'])
    z = _long_conv(v, g1, g2, circ, p['hy_bias'], B, L, hyw)

    x1, aff = _merge(x2, y_g, z, g_s, g_h, p['s5_w_glu'].astype(bf16), p['w_s5_proj'].astype(bf16),
                     p['w_hy_proj'].astype(bf16), p['w_out'].astype(bf16),
                     p['ln1_g'].reshape(1, D), p['ln1_b'].reshape(1, D), p['w_router'], B, L, alpha)

    out = _expert_choice(x1, aff, p['ex_w1_bf16'], p['ex_w3_bf16'], p['ex_w2_bf16'],
                         p['ln2_g'].reshape(1, D), p['ln2_b'].reshape(1, D), alpha)
    return out.reshape(B, L, D)


_PARAM_NAMES = ('w_in', 's5_lambda_re', 's5_lambda_im', 's5_log_dt', 's5_b_re', 's5_b_im',
                's5_c_re', 's5_c_im', 's5_d', 's5_w_glu', 'w_s5_proj',
                'hy_short_w', 'hy_short_b', 'filt_w1', 'filt_b1', 'filt_freq1', 'filt_w2', 'filt_b2',
                'filt_freq2', 'filt_w3', 'filt_decay', 'hy_bias', 'w_hy_proj', 'w_out',
                'ln1_g', 'ln1_b', 'w_router', 'ex_w1', 'ex_w3', 'ex_w2', 'ln2_g', 'ln2_b')


def kernel(x_prompt, x_sample, w_in, s5_lambda_re, s5_lambda_im, s5_log_dt, s5_b_re, s5_b_im, s5_c_re, s5_c_im, s5_d, s5_w_glu, w_s5_proj, hy_short_w, hy_short_b, filt_w1, filt_b1, filt_freq1, filt_w2, filt_b2, filt_freq2, filt_w3, filt_decay, hy_bias, w_hy_proj, w_out, ln1_g, ln1_b, w_router, ex_w1, ex_w3, ex_w2, ln2_g, ln2_b):
    stacked = (w_in, s5_lambda_re, s5_lambda_im, s5_log_dt, s5_b_re, s5_b_im, s5_c_re, s5_c_im, s5_d,
               s5_w_glu, w_s5_proj, hy_short_w, hy_short_b, filt_w1, filt_b1, filt_freq1, filt_w2,
               filt_b2, filt_freq2, filt_w3, filt_decay, hy_bias, w_hy_proj, w_out, ln1_g, ln1_b,
               w_router, ex_w1, ex_w3, ex_w2, ln2_g, ln2_b)
    depth = w_in.shape[0]
    alpha = (2.0 * depth) ** 0.25
    xs = [x_prompt, x_sample]
    for l in range(depth):
        p = {k: v[l] for k, v in zip(_PARAM_NAMES, stacked)}
        for k in ('ex_w1', 'ex_w3', 'ex_w2'):
            p[k + '_bf16'] = p[k].astype(bf16)
        s5_mats = _s5_matrices(p, [_s5_rows(x.shape[0], x.shape[1])[2] for x in xs])
        xs = [_encoder_layer(x, p, mats, alpha) for x, mats in zip(xs, s5_mats)]
    return tuple(xs)
```
